```python
import jax, jax.numpy as jnp
from jax import lax
import numpy as np

D_MODEL = 1024
BATCH = 8
SEQ = 2048
DEPTH = 2
DEC_BATCH = 4
DEC_SEQ = 8192
PAST_LEN = 128

HEAD_DIM = 64
ATT_Q_HEADS = 8
ATT_KV_HEADS = 2
ATT_GROUP = ATT_Q_HEADS // ATT_KV_HEADS
ATT_Q_W = ATT_Q_HEADS * HEAD_DIM
ATT_KV_W = ATT_KV_HEADS * HEAD_DIM
WINDOW = 128
BLOCK = 128
RET_HEADS = 4
RET_HEAD_DIM = 128
RET_WIDTH = RET_HEADS * RET_HEAD_DIM
CHUNK = 128
N_BRANCH = 2
D_FF = 2816
CONV_WIDTH = 3
ROPE_THETA = 10000.0
EPS = 1e-6
NEG_INF = -1e30
IN_WIDTH = ATT_Q_W + 2 * ATT_KV_W + 4 * RET_WIDTH + N_BRANCH * D_MODEL

kernel_name = "hybrid_swa_retention_gated_encoder"


def rmsnorm(x, g):
    xf = x.astype(jnp.float32)
    y = xf * lax.rsqrt(jnp.mean(xf * xf, axis=-1, keepdims=True) + EPS)
    return (y * g.astype(jnp.float32)).astype(x.dtype)


def rope(x):
    S, D = x.shape[1], x.shape[-1]
    half = D // 2
    freqs = ROPE_THETA ** (-jnp.arange(half, dtype=jnp.float32) / half)
    ang = jnp.arange(S, dtype=jnp.float32)[:, None] * freqs[None, :]
    cos = jnp.cos(ang)[None, :, None, :]
    sin = jnp.sin(ang)[None, :, None, :]
    xf = x.astype(jnp.float32)
    x1, x2 = xf[..., :half], xf[..., half:]
    out = jnp.concatenate([x1 * cos - x2 * sin, x2 * cos + x1 * sin], axis=-1)
    return out.astype(x.dtype)


def windowed_gqa(q, k, v, sink):
    B, S = q.shape[0], q.shape[1]
    nb = S // BLOCK
    qb = q.reshape(B, nb, BLOCK, ATT_KV_HEADS, ATT_GROUP, HEAD_DIM)
    pad = ((0, 0), (BLOCK, BLOCK), (0, 0), (0, 0))
    kp = jnp.pad(k, pad).reshape(B, nb + 2, BLOCK, ATT_KV_HEADS, HEAD_DIM)
    vp = jnp.pad(v, pad).reshape(B, nb + 2, BLOCK, ATT_KV_HEADS, HEAD_DIM)
    kw = jnp.concatenate([kp[:, :-2], kp[:, 1:-1], kp[:, 2:]], axis=2)
    vw = jnp.concatenate([vp[:, :-2], vp[:, 1:-1], vp[:, 2:]], axis=2)
    scores = jnp.einsum('bnqhgd,bnkhd->bnhgqk', qb, kw).astype(jnp.float32) * (HEAD_DIM ** -0.5)
    r = jnp.arange(BLOCK)[:, None]
    c = jnp.arange(3 * BLOCK)[None, :]
    rel = c - BLOCK - r
    kpos = jnp.arange(nb)[:, None, None] * BLOCK + c[None] - BLOCK
    mask = (jnp.abs(rel) <= WINDOW)[None] & (kpos >= 0) & (kpos < S)
    scores = jnp.where(mask[None, :, None, None], scores, NEG_INF)
    sink_l = sink.astype(jnp.float32).reshape(ATT_KV_HEADS, ATT_GROUP)[None, None, :, :, None, None]
    m = jnp.maximum(jnp.max(scores, axis=-1, keepdims=True), sink_l)
    p = jnp.exp(scores - m)
    denom = jnp.sum(p, axis=-1, keepdims=True) + jnp.exp(sink_l - m)
    out = jnp.einsum('bnhgqk,bnkhd->bnqhgd', (p / denom).astype(v.dtype), vw)
    return out.reshape(B, S, ATT_Q_W)


def retention_dir(q, k, v, log_gamma, strict):
    B, S, H, dk = q.shape
    dv = v.shape[-1]
    nc = S // CHUNK
    qc = q.reshape(B, nc, CHUNK, H, dk)
    kc = k.reshape(B, nc, CHUNK, H, dk)
    vc = v.reshape(B, nc, CHUNK, H, dv)
    idx = jnp.arange(CHUNK, dtype=jnp.float32)
    diff = idx[:, None] - idx[None, :]
    within = (diff > 0) if strict else (diff >= 0)
    dmat = jnp.where(within[None], jnp.exp(log_gamma[:, None, None] * jnp.maximum(diff, 0.0)[None]), 0.0)
    inner = jnp.einsum('bnqhd,bnkhd->bnhqk', qc, kc) * dmat[None, None]
    o_inner = jnp.einsum('bnhqk,bnkhe->bnqhe', inner, vc)
    zeta = jnp.exp(log_gamma[:, None] * (CHUNK - 1 - idx)[None, :])
    kv_chunk = jnp.einsum('bnkhd,hk,bnkhe->bnhde', kc, zeta, vc)
    chunk_decay = jnp.exp(log_gamma * CHUNK)[None, :, None, None]

    def step(state, kv):
        return state * chunk_decay + kv, state

    _, r_prev = lax.scan(step, jnp.zeros((B, H, dk, dv), jnp.float32), jnp.moveaxis(kv_chunk, 1, 0))
    r_prev = jnp.moveaxis(r_prev, 0, 1)
    xi = jnp.exp(log_gamma[:, None] * (idx + 1.0)[None, :])
    o_cross = jnp.einsum('bnqhd,bnhde->bnqhe', qc, r_prev) * xi.T[None, None, :, :, None]
    return (o_inner + o_cross).reshape(B, S, H, dv)


def retention(q, k, v, g, log_decay_f, log_decay_b, norm_g):
    B, S = q.shape[0], q.shape[1]
    dtype = q.dtype
    qh = rope(q.reshape(B, S, RET_HEADS, RET_HEAD_DIM)).astype(jnp.float32)
    kh = rope(k.reshape(B, S, RET_HEADS, RET_HEAD_DIM)).astype(jnp.float32) * (RET_HEAD_DIM ** -0.5)
    vh = v.reshape(B, S, RET_HEADS, RET_HEAD_DIM).astype(jnp.float32)
    lf = log_decay_f.astype(jnp.float32)
    lb = log_decay_b.astype(jnp.float32)
    fwd = retention_dir(qh, kh, vh, lf, strict=False)
    bwd = jnp.flip(retention_dir(jnp.flip(qh, 1), jnp.flip(kh, 1), jnp.flip(vh, 1), lb, strict=True), 1)
    o = fwd + bwd
    o = o * lax.rsqrt(jnp.mean(o * o, axis=-1, keepdims=True) + EPS)
    o = o.reshape(B, S, RET_WIDTH) * norm_g.astype(jnp.float32)
    o = o * jax.nn.silu(g.astype(jnp.float32))
    return o.astype(dtype)


def dwconv_centred(a, w, b):
    S = a.shape[1]
    half = CONV_WIDTH // 2
    ap = jnp.pad(a, ((0, 0), (half, half), (0, 0)))
    out = b[None, None, :]
    for j in range(CONV_WIDTH):
        out = out + ap[:, j:j + S] * w[j][None, None, :]
    return out


def layer(x, norm_mix_g, w_in, attn_sink, log_decay_f, log_decay_b, ret_norm_g,
          w_branch_attn, w_branch_ret, w_out, norm_ffn_g, w_ffn_in, conv_w, conv_b, w_ffn_out):
    B, S = x.shape[0], x.shape[1]
    h = rmsnorm(x, norm_mix_g)
    proj = h @ w_in
    sizes = [ATT_Q_W, ATT_KV_W, ATT_KV_W, RET_WIDTH, RET_WIDTH, RET_WIDTH, RET_WIDTH]
    cuts = []
    acc = 0
    for s_ in sizes:
        acc += s_
        cuts.append(acc)
    q_a, k_a, v_a, q_r, k_r, v_r, g_r, gate = jnp.split(proj, cuts, axis=-1)
    q_a = rope(q_a.reshape(B, S, ATT_Q_HEADS, HEAD_DIM))
    k_a = rope(k_a.reshape(B, S, ATT_KV_HEADS, HEAD_DIM))
    v_a = v_a.reshape(B, S, ATT_KV_HEADS, HEAD_DIM)
    att = windowed_gqa(q_a, k_a, v_a, attn_sink)
    ret = retention(q_r, k_r, v_r, g_r, log_decay_f, log_decay_b, ret_norm_g)
    gates = jax.nn.sigmoid(gate.astype(jnp.float32)).astype(x.dtype).reshape(B, S, N_BRANCH, D_MODEL)
    merged = gates[:, :, 0] * (att @ w_branch_attn) + gates[:, :, 1] * (ret @ w_branch_ret)
    x = x + merged @ w_out
    h = rmsnorm(x, norm_ffn_g)
    a, u = jnp.split(h @ w_ffn_in, 2, axis=-1)
    a = dwconv_centred(a, conv_w, conv_b)
    x = x + (jax.nn.gelu(a, approximate=False) * u) @ w_ffn_out
    return x


def setup_inputs(seed: int = 0) -> dict:
    key = jax.random.key(seed)
    ks = jax.random.split(key, 20)
    f32 = jnp.float32
    nrm = lambda k_, shape, scale: jax.random.normal(k_, shape, f32) * scale
    base_decay = jnp.log(1.0 - 2.0 ** (-5.0 - jnp.arange(RET_HEADS, dtype=f32)))
    return {
        "x_prompt": nrm(ks[0], (BATCH, SEQ, D_MODEL), 1.0),
        "x_sample": nrm(ks[1], (DEC_BATCH, DEC_SEQ, D_MODEL), 1.0),
        "norm_mix_g": 1.0 + nrm(ks[2], (DEPTH, D_MODEL), 0.02),
        "w_in": nrm(ks[3], (DEPTH, D_MODEL, IN_WIDTH), D_MODEL ** -0.5),
        "attn_sink": nrm(ks[4], (DEPTH, ATT_Q_HEADS), 0.5),
        "ret_log_decay_f": base_decay[None, :] * (1.0 + nrm(ks[5], (DEPTH, RET_HEADS), 0.05)),
        "ret_log_decay_b": base_decay[None, :] * (1.0 + nrm(ks[6], (DEPTH, RET_HEADS), 0.05)),
        "ret_norm_g": 1.0 + nrm(ks[7], (DEPTH, RET_WIDTH), 0.02),
        "w_branch_attn": nrm(ks[8], (DEPTH, ATT_Q_W, D_MODEL), ATT_Q_W ** -0.5),
        "w_branch_ret": nrm(ks[9], (DEPTH, RET_WIDTH, D_MODEL), RET_WIDTH ** -0.5),
        "w_out": nrm(ks[10], (DEPTH, D_MODEL, D_MODEL), D_MODEL ** -0.5),
        "norm_ffn_g": 1.0 + nrm(ks[11], (DEPTH, D_MODEL), 0.02),
        "w_ffn_in": nrm(ks[12], (DEPTH, D_MODEL, 2 * D_FF), D_MODEL ** -0.5),
        "conv_w": nrm(ks[13], (DEPTH, CONV_WIDTH, D_FF), CONV_WIDTH ** -0.5),
        "conv_b": nrm(ks[14], (DEPTH, D_FF), 0.01),
        "w_ffn_out": nrm(ks[15], (DEPTH, D_FF, D_MODEL), D_FF ** -0.5),
        "final_norm_g": 1.0 + nrm(ks[16], (D_MODEL,), 0.02),
    }


def reference(x_prompt, x_sample, norm_mix_g, w_in, attn_sink, ret_log_decay_f, ret_log_decay_b,
              ret_norm_g, w_branch_attn, w_branch_ret, w_out, norm_ffn_g, w_ffn_in, conv_w, conv_b,
              w_ffn_out, final_norm_g):
    yp = x_prompt
    ys = x_sample
    for l in range(DEPTH):
        args = (norm_mix_g[l], w_in[l], attn_sink[l], ret_log_decay_f[l], ret_log_decay_b[l], ret_norm_g[l],
                w_branch_attn[l], w_branch_ret[l], w_out[l], norm_ffn_g[l], w_ffn_in[l], conv_w[l], conv_b[l],
                w_ffn_out[l])
        yp = layer(yp, *args)
        ys = layer(ys, *args)
    y_prompt = rmsnorm(yp, final_norm_g)
    y_sample = rmsnorm(ys, final_norm_g)
    return (y_prompt, y_sample)
```

```python
import functools

import jax
import jax.numpy as jnp
from jax import lax
from jax.experimental import pallas as pl
from jax.experimental.pallas import tpu as pltpu

D_MODEL = 1024
HEAD_DIM = 64
ATT_Q_HEADS = 8
ATT_KV_HEADS = 2
ATT_GROUP = ATT_Q_HEADS // ATT_KV_HEADS
ATT_Q_W = ATT_Q_HEADS * HEAD_DIM
ATT_KV_W = ATT_KV_HEADS * HEAD_DIM
WINDOW = 128
BLOCK = 128
RET_HEADS = 4
RET_HEAD_DIM = 128
RET_WIDTH = RET_HEADS * RET_HEAD_DIM
CHUNK = 128
D_FF = 2816
CONV_WIDTH = 3
ROPE_THETA = 10000.0
EPS = 1e-6
NEG_INF = -1e30
IN_WIDTH = ATT_Q_W + 2 * ATT_KV_W + 4 * RET_WIDTH + 2 * D_MODEL

LANES = 128
BF16_ROWS = 16
VMEM_LIMIT = 56 * 1024 * 1024

TOK_TILE = 512
FF_CHUNK = 256
STATE_CHUNKS = 4

F32 = jnp.float32
BF16 = jnp.bfloat16

_C_QA = 0
_C_KA = _C_QA + ATT_Q_W
_C_VA = _C_KA + ATT_KV_W
_C_QR = _C_VA + ATT_KV_W
_C_KR = _C_QR + RET_WIDTH
_C_VR = _C_KR + RET_WIDTH
_C_GR = _C_VR + RET_WIDTH
_C_GATE = _C_GR + RET_WIDTH


def _rmsnorm(x, g):
    return (x * lax.rsqrt(jnp.mean(x * x, axis=-1, keepdims=True) + EPS)) * g


def _dot(a, b):
    return jnp.dot(a, b, preferred_element_type=F32)


def _dot_nt(a, b):
    return lax.dot_general(a, b, (((1,), (1,)), ((), ())), preferred_element_type=F32)


def _dot_tn(a, b):
    return lax.dot_general(a, b, (((0,), (0,)), ((), ())), preferred_element_type=F32)


def _const_spec(shape):
    nd = len(shape)
    return pl.BlockSpec(shape, lambda *_: (0,) * nd, pipeline_mode=pl.Buffered(1))


def _inproj_kernel(x_ref, g_ref, w_ref, ca_ref, sa_ref, cr_ref, sr_ref,
                   qa_ref, ka_ref, va_ref, qr_ref, kr_ref, vr_ref, gr_ref, gate_ref):
    x = x_ref[0]
    hb = _rmsnorm(x, g_ref[...]).astype(BF16)
    t = x.shape[0]
    lane = lax.broadcasted_iota(jnp.int32, (t, LANES), 1)
    first_half = (lane % HEAD_DIM) < (HEAD_DIM // 2)
    ca, sa, cr, sr = ca_ref[...], sa_ref[...], cr_ref[...], sr_ref[...]

    def mm(c0, n):
        return _dot(hb, w_ref[:, c0:c0 + n])

    def rope_a(y):
        rot = jnp.where(first_half, pltpu.roll(y, LANES - HEAD_DIM // 2, 1),
                        pltpu.roll(y, HEAD_DIM // 2, 1))
        return y * ca + rot * sa

    def rope_r(y):
        return y * cr + pltpu.roll(y, RET_HEAD_DIM // 2, 1) * sr

    def slab(y, s):
        return y[:, s * LANES:(s + 1) * LANES]

    y = mm(_C_QA, ATT_Q_W)
    for s in range(ATT_Q_W // LANES):
        qa_ref[0, :, s * LANES:(s + 1) * LANES] = (rope_a(slab(y, s)) * (HEAD_DIM ** -0.5)).astype(BF16)
    y = mm(_C_KA, 2 * ATT_KV_W)
    ka_ref[0] = rope_a(slab(y, 0)).astype(BF16)
    va_ref[0] = slab(y, 1).astype(BF16)
    y = mm(_C_QR, RET_WIDTH)
    for s in range(RET_HEADS):
        qr_ref[0, :, s * LANES:(s + 1) * LANES] = rope_r(slab(y, s)).astype(BF16)
    y = mm(_C_KR, RET_WIDTH)
    for s in range(RET_HEADS):
        kr_ref[0, :, s * LANES:(s + 1) * LANES] = (rope_r(slab(y, s)) * (RET_HEAD_DIM ** -0.5)).astype(BF16)
    vr_ref[0] = mm(_C_VR, RET_WIDTH).astype(BF16)
    gr_ref[0] = mm(_C_GR, RET_WIDTH).astype(BF16)
    for c in range(4):
        n = 2 * D_MODEL // 4
        gate_ref[0, :, c * n:(c + 1) * n] = mm(_C_GATE + c * n, n).astype(BF16)


def _inproj(x, g, w, ca, sa, cr, sr):
    b, s, d = x.shape
    t = TOK_TILE
    tok = lambda width: pl.BlockSpec((1, t, width), lambda bi, i: (bi, i, 0))
    tab = pl.BlockSpec((t, LANES), lambda bi, i: (i, 0))
    widths = [ATT_Q_W, ATT_KV_W, ATT_KV_W, RET_WIDTH, RET_WIDTH, RET_WIDTH, RET_WIDTH, 2 * D_MODEL]
    return pl.pallas_call(
        _inproj_kernel,
        grid=(b, s // t),
        in_specs=[tok(d), _const_spec((1, d)), _const_spec((d, IN_WIDTH)), tab, tab, tab, tab],
        out_specs=[tok(wd) for wd in widths],
        out_shape=[jax.ShapeDtypeStruct((b, s, wd), BF16) for wd in widths],
        compiler_params=pltpu.CompilerParams(
            dimension_semantics=("arbitrary", "arbitrary"), vmem_limit_bytes=VMEM_LIMIT),
        name="inproj",
    )(x, g, w, ca, sa, cr, sr)


def _state_kernel(kf_ref, vf_ref, kb_ref, vb_ref, lf_ref, lb_ref, rf_ref, rb_ref, st_ref):
    @pl.when(pl.program_id(1) == 0)
    def _():
        st_ref[...] = jnp.zeros_like(st_ref)

    row = lax.broadcasted_iota(jnp.int32, (CHUNK, RET_WIDTH), 0).astype(F32)
    lf = lf_ref[...]
    lb = lb_ref[...]
    zeta_f = jnp.exp(lf * (CHUNK - 1.0 - row))
    zeta_b = jnp.exp(lb * row)
    dec_f = jnp.exp(lf * float(CHUNK))
    dec_b = jnp.exp(lb * float(CHUNK))

    def update(k_ref, v_ref, out_ref, c, zeta, dec, d):
        rows = slice(c * CHUNK, (c + 1) * CHUNK)
        for h in range(RET_HEADS):
            cols = slice(h * RET_HEAD_DIM, (h + 1) * RET_HEAD_DIM)
            st = st_ref[d, h]
            out_ref[0, c, h] = st.astype(BF16)
            zv = (v_ref[0, rows, cols].astype(F32) * zeta[:, cols]).astype(BF16)
            kv = _dot_tn(k_ref[0, rows, cols], zv)
            st_ref[d, h] = st * dec[:, cols] + kv

    for c in range(STATE_CHUNKS):
        update(kf_ref, vf_ref, rf_ref, c, zeta_f, dec_f, 0)
    for c in reversed(range(STATE_CHUNKS)):
        update(kb_ref, vb_ref, rb_ref, c, zeta_b, dec_b, 1)


def _states(kr, vr, lf_l, lb_l):
    b, s, _ = kr.shape
    nc = s // CHUNK
    ns = nc // STATE_CHUNKS
    t = STATE_CHUNKS * CHUNK
    fwd = pl.BlockSpec((1, t, RET_WIDTH), lambda bi, i: (bi, i, 0))
    bwd = pl.BlockSpec((1, t, RET_WIDTH), lambda bi, i: (bi, ns - 1 - i, 0))
    st_shape = (1, STATE_CHUNKS, RET_HEADS, RET_HEAD_DIM, RET_HEAD_DIM)
    out_f = pl.BlockSpec(st_shape, lambda bi, i: (bi, i, 0, 0, 0))
    out_b = pl.BlockSpec(st_shape, lambda bi, i: (bi, ns - 1 - i, 0, 0, 0))
    full = jax.ShapeDtypeStruct((b, nc, RET_HEADS, RET_HEAD_DIM, RET_HEAD_DIM), BF16)
    return pl.pallas_call(
        _state_kernel,
        grid=(b, ns),
        in_specs=[fwd, fwd, bwd, bwd, _const_spec((1, RET_WIDTH)), _const_spec((1, RET_WIDTH))],
        out_specs=[out_f, out_b],
        out_shape=[full, full],
        scratch_shapes=[pltpu.VMEM((2, RET_HEADS, RET_HEAD_DIM, RET_HEAD_DIM), F32)],
        compiler_params=pltpu.CompilerParams(
            dimension_semantics=("arbitrary", "arbitrary"), vmem_limit_bytes=VMEM_LIMIT),
        name="ret_states",
    )(kr, vr, kr, vr, lf_l, lb_l)


def _mixer_kernel(nb_total,
                  qa_ref, kp_ref, kc_ref, kn_ref, vp_ref, vc_ref, vn_ref,
                  qr_ref, kr_ref, vr_ref, gr_ref, gate_ref, rf_ref, rb_ref, x_ref,
                  sink_ref, lf_ref, lb_ref, rng_ref, wba_ref, wbr_ref, wo_ref,
                  o_ref,
                  kv_s, vv_s, dm_s, xi_s, att_s, ret_s, mrg_s):
    t = x_ref.shape[1]
    nblk = t // BLOCK
    tile = pl.program_id(1)

    lane_k = lax.broadcasted_iota(jnp.int32, (t + 2 * BLOCK, LANES), 1)
    low = lane_k < HEAD_DIM
    for src, dst in ((jnp.concatenate([kp_ref[0], kc_ref[0], kn_ref[0]], axis=0), kv_s),
                     (jnp.concatenate([vp_ref[0], vc_ref[0], vn_ref[0]], axis=0), vv_s)):
        src = src.astype(F32)
        h0 = jnp.where(low, src, 0.0)
        h1 = jnp.where(low, 0.0, src)
        dst[0] = h0.astype(BF16)
        dst[1] = pltpu.roll(h0, HEAD_DIM, 1).astype(BF16)
        dst[2] = pltpu.roll(h1, HEAD_DIM, 1).astype(BF16)
        dst[3] = h1.astype(BF16)

    ri = lax.broadcasted_iota(jnp.int32, (CHUNK, CHUNK), 0).astype(F32)
    ci = lax.broadcasted_iota(jnp.int32, (CHUNK, CHUNK), 1).astype(F32)
    diff = ri - ci
    for h in range(RET_HEADS):
        lf = lf_ref[h:h + 1, :]
        lb = lb_ref[h:h + 1, :]
        dm_s[h] = (jnp.where(diff >= 0, jnp.exp(lf * jnp.maximum(diff, 0.0)), 0.0)
                   + jnp.where(diff < 0, jnp.exp(lb * jnp.maximum(-diff, 0.0)), 0.0))
        xi_s[0, h] = jnp.exp(lf * (ri + 1.0))
        xi_s[1, h] = jnp.exp(lb * (float(CHUNK) - ri))

    qi = lax.broadcasted_iota(jnp.int32, (BLOCK, 3 * BLOCK), 0)
    kj = lax.broadcasted_iota(jnp.int32, (BLOCK, 3 * BLOCK), 1)
    rel = kj - BLOCK - qi
    band = jnp.abs(rel) <= WINDOW
    lane_o = lax.broadcasted_iota(jnp.int32, (BLOCK, LANES), 1)
    low_o = lane_o < HEAD_DIM

    def block_body(j, carry):
        r0 = pl.multiple_of(j * BLOCK, BLOCK)
        rows = pl.ds(r0, BLOCK)
        krows = pl.ds(r0, 3 * BLOCK)
        gblk = tile * nblk + j
        mask = band & ((kj >= BLOCK) | (gblk > 0)) & ((kj < 2 * BLOCK) | (gblk < nb_total - 1))

        for s in range(ATT_Q_HEADS // 2):
            q = qa_ref[0, rows, s * LANES:(s + 1) * LANES]
            kvh = (2 * s) // ATT_GROUP
            acc = None
            inv = []
            for pos in range(2):
                hq = 2 * s + pos
                sc = _dot_nt(q, kv_s[2 * kvh + pos, krows, :])
                sc = jnp.where(mask, sc, NEG_INF)
                sink = sink_ref[hq]
                m = jnp.maximum(jnp.max(sc, axis=-1, keepdims=True), sink)
                p = jnp.exp(sc - m)
                den = jnp.sum(p, axis=-1, keepdims=True) + jnp.exp(sink - m)
                inv.append(1.0 / den)
                pv = _dot(p.astype(BF16), vv_s[2 * kvh + pos, krows, :])
                acc = pv if acc is None else acc + pv
            att_s[rows, s * LANES:(s + 1) * LANES] = (acc * jnp.where(low_o, inv[0], inv[1])).astype(BF16)

        for h in range(RET_HEADS):
            cols = slice(h * LANES, (h + 1) * LANES)
            q = qr_ref[0, rows, cols]
            k = kr_ref[0, rows, cols]
            v = vr_ref[0, rows, cols]
            inner = (_dot_nt(q, k) * dm_s[h]).astype(BF16)
            o = _dot(inner, v)
            cross = _dot(q, jnp.concatenate([rf_ref[0, j, h], rb_ref[0, j, h]], axis=1))
            o = o + cross[:, :LANES] * xi_s[0, h] + cross[:, LANES:] * xi_s[1, h]
            o = o * lax.rsqrt(jnp.mean(o * o, axis=-1, keepdims=True) + EPS)
            g = gr_ref[0, rows, cols].astype(F32)
            o = (o * rng_ref[:, cols]) * (g * jax.nn.sigmoid(g))
            ret_s[rows, cols] = o.astype(BF16)
        return carry

    lax.fori_loop(0, nblk, block_body, 0)

    nchunk = 256
    for c in range(D_MODEL // nchunk):
        cols = slice(c * nchunk, (c + 1) * nchunk)
        ba = _dot(att_s[...], wba_ref[:, cols])
        br = _dot(ret_s[...], wbr_ref[:, cols])
        g0 = jax.nn.sigmoid(gate_ref[0, :, c * nchunk:(c + 1) * nchunk].astype(F32))
        g1 = jax.nn.sigmoid(gate_ref[0, :, D_MODEL + c * nchunk:D_MODEL + (c + 1) * nchunk].astype(F32))
        mrg_s[:, cols] = (g0 * ba + g1 * br).astype(BF16)
    o_ref[0] = x_ref[0] + _dot(mrg_s[...], wo_ref[...])


def _mixer(qa, ka, va, qr, kr, vr, gr, gate, rf, rb, x, sink, lf_h, lb_h, rng, wba, wbr, wo):
    b, s, d = x.shape
    t = TOK_TILE
    nblk = t // BLOCK
    nb_total = s // BLOCK
    tok = lambda width: pl.BlockSpec((1, t, width), lambda bi, i: (bi, i, 0))
    prev = pl.BlockSpec((1, BLOCK, ATT_KV_W), lambda bi, i: (bi, jnp.maximum(i * nblk - 1, 0), 0))
    nxt = pl.BlockSpec((1, BLOCK, ATT_KV_W), lambda bi, i: (bi, jnp.minimum((i + 1) * nblk, nb_total - 1), 0))
    st = pl.BlockSpec((1, nblk, RET_HEADS, RET_HEAD_DIM, RET_HEAD_DIM), lambda bi, i: (bi, i, 0, 0, 0))
    smem = pl.BlockSpec(memory_space=pltpu.SMEM)
    return pl.pallas_call(
        functools.partial(_mixer_kernel, nb_total),
        grid=(b, s // t),
        in_specs=[tok(ATT_Q_W), prev, tok(ATT_KV_W), nxt, prev, tok(ATT_KV_W), nxt,
                  tok(RET_WIDTH), tok(RET_WIDTH), tok(RET_WIDTH), tok(RET_WIDTH), tok(2 * D_MODEL),
                  st, st, tok(d),
                  smem, _const_spec((RET_HEADS, LANES)), _const_spec((RET_HEADS, LANES)),
                  _const_spec((1, RET_WIDTH)),
                  _const_spec((ATT_Q_W, d)), _const_spec((RET_WIDTH, d)), _const_spec((d, d))],
        out_specs=tok(d),
        out_shape=jax.ShapeDtypeStruct((b, s, d), F32),
        scratch_shapes=[
            pltpu.VMEM((4, t + 2 * BLOCK, LANES), BF16),
            pltpu.VMEM((4, t + 2 * BLOCK, LANES), BF16),
            pltpu.VMEM((RET_HEADS, CHUNK, CHUNK), F32),
            pltpu.VMEM((2, RET_HEADS, CHUNK, CHUNK), F32),
            pltpu.VMEM((t, ATT_Q_W), BF16),
            pltpu.VMEM((t, RET_WIDTH), BF16),
            pltpu.VMEM((t, d), BF16),
        ],
        compiler_params=pltpu.CompilerParams(
            dimension_semantics=("arbitrary", "arbitrary"), vmem_limit_bytes=VMEM_LIMIT),
        name="mixer",
    )(qa, ka, ka, ka, va, va, va, qr, kr, vr, gr, gate, rf, rb, x, sink, lf_h, lb_h, rng, wba, wbr, wo)


def _ffn_kernel(final_norm, xp_ref, x_ref, xn_ref, g_ref, wi_ref, cw_ref, cb_ref, wo_ref, fg_ref,
                o_ref, gu_s):
    t = x_ref.shape[1]
    i = pl.program_id(1)
    n = pl.num_programs(1)
    g = g_ref[...]
    x = x_ref[0]
    hp = jnp.where(i > 0, _rmsnorm(xp_ref[0], g), 0.0)
    hn = jnp.where(i < n - 1, _rmsnorm(xn_ref[0], g), 0.0)
    h = _rmsnorm(x, g).astype(BF16)
    h_ext = jnp.concatenate([hp.astype(BF16), h, hn.astype(BF16)], axis=0)

    def chunk_body(c, carry):
        c0 = pl.multiple_of(c * FF_CHUNK, FF_CHUNK)
        a_ext = _dot(h_ext, wi_ref[:, pl.ds(c0, FF_CHUNK)])
        u = _dot(h, wi_ref[:, pl.ds(D_FF + c0, FF_CHUNK)])
        w = cw_ref[:, pl.ds(c0, FF_CHUNK)]
        lo = BF16_ROWS
        a = (cb_ref[:, pl.ds(c0, FF_CHUNK)]
             + a_ext[lo - 1:lo - 1 + t] * w[0:1]
             + a_ext[lo:lo + t] * w[1:2]
             + a_ext[lo + 1:lo + 1 + t] * w[2:3])
        gelu = 0.5 * a * (1.0 + lax.erf(a * (2.0 ** -0.5)))
        gu_s[:, pl.ds(c0, FF_CHUNK)] = (gelu * u).astype(BF16)
        return carry

    lax.fori_loop(0, D_FF // FF_CHUNK, chunk_body, 0)
    y = x + _dot(gu_s[...], wo_ref[...])
    if final_norm:
        y = _rmsnorm(y, fg_ref[...])
    o_ref[0] = y


def _ffn(x, g, wi, cw, cb, wo, fg, final_norm):
    b, s, d = x.shape
    t = TOK_TILE
    hb = t // BF16_ROWS
    nh = s // BF16_ROWS
    tok = pl.BlockSpec((1, t, d), lambda bi, i: (bi, i, 0))
    prev = pl.BlockSpec((1, BF16_ROWS, d), lambda bi, i: (bi, jnp.maximum(i * hb - 1, 0), 0))
    nxt = pl.BlockSpec((1, BF16_ROWS, d), lambda bi, i: (bi, jnp.minimum((i + 1) * hb, nh - 1), 0))
    return pl.pallas_call(
        functools.partial(_ffn_kernel, final_norm),
        grid=(b, s // t),
        in_specs=[prev, tok, nxt, _const_spec((1, d)), _const_spec((d, 2 * D_FF)),
                  _const_spec((CONV_WIDTH, D_FF)), _const_spec((1, D_FF)), _const_spec((D_FF, d)),
                  _const_spec((1, d))],
        out_specs=tok,
        out_shape=jax.ShapeDtypeStruct((b, s, d), F32),
        scratch_shapes=[pltpu.VMEM((t, D_FF), BF16)],
        compiler_params=pltpu.CompilerParams(
            dimension_semantics=("arbitrary", "arbitrary"), vmem_limit_bytes=VMEM_LIMIT),
        name="ffn",
    )(x, x, x, g, wi, cw, cb, wo, fg)


def _rope_tables(s, half, reps):
    freqs = ROPE_THETA ** (-jnp.arange(half, dtype=F32) / half)
    ang = jnp.arange(s, dtype=F32)[:, None] * freqs[None, :]
    cos = jnp.cos(ang)
    sin = jnp.sin(ang)
    cos_t = jnp.tile(jnp.concatenate([cos, cos], axis=-1), (1, reps))
    sin_t = jnp.tile(jnp.concatenate([-sin, sin], axis=-1), (1, reps))
    return cos_t, sin_t


def _layer(x, p, tabs, final_g, final_norm):
    ca, sa, cr, sr = tabs
    qa, ka, va, qr, kr, vr, gr, gate = _inproj(x, p["norm_mix_g"], p["w_in"], ca, sa, cr, sr)
    rf, rb = _states(kr, vr, p["lf_lanes"], p["lb_lanes"])
    x = _mixer(qa, ka, va, qr, kr, vr, gr, gate, rf, rb, x, p["sink"], p["lf_heads"], p["lb_heads"],
               p["ret_norm_g"], p["w_branch_attn"], p["w_branch_ret"], p["w_out"])
    return _ffn(x, p["norm_ffn_g"], p["w_ffn_in"], p["conv_w"], p["conv_b"], p["w_ffn_out"],
                final_g, final_norm)


def kernel(x_prompt, x_sample, norm_mix_g, w_in, attn_sink, ret_log_decay_f, ret_log_decay_b, ret_norm_g,
           w_branch_attn, w_branch_ret, w_out, norm_ffn_g, w_ffn_in, conv_w, conv_b, w_ffn_out, final_norm_g):
    depth = w_in.shape[0]
    layers = []
    for l in range(depth):
        lf = ret_log_decay_f[l].astype(F32)
        lb = ret_log_decay_b[l].astype(F32)
        layers.append(dict(
            norm_mix_g=norm_mix_g[l].reshape(1, D_MODEL),
            w_in=w_in[l].astype(BF16),
            sink=attn_sink[l].astype(F32),
            lf_lanes=jnp.repeat(lf, RET_HEAD_DIM).reshape(1, RET_WIDTH),
            lb_lanes=jnp.repeat(lb, RET_HEAD_DIM).reshape(1, RET_WIDTH),
            lf_heads=jnp.broadcast_to(lf[:, None], (RET_HEADS, LANES)),
            lb_heads=jnp.broadcast_to(lb[:, None], (RET_HEADS, LANES)),
            ret_norm_g=ret_norm_g[l].reshape(1, RET_WIDTH),
            w_branch_attn=w_branch_attn[l].astype(BF16),
            w_branch_ret=w_branch_ret[l].astype(BF16),
            w_out=w_out[l].astype(BF16),
            norm_ffn_g=norm_ffn_g[l].reshape(1, D_MODEL),
            w_ffn_in=w_ffn_in[l].astype(BF16),
            conv_w=conv_w[l],
            conv_b=conv_b[l].reshape(1, D_FF),
            w_ffn_out=w_ffn_out[l].astype(BF16),
        ))
    fg = final_norm_g.reshape(1, D_MODEL)
    outs = []
    for x in (x_prompt, x_sample):
        s = x.shape[1]
        tabs = _rope_tables(s, HEAD_DIM // 2, LANES // HEAD_DIM) + _rope_tables(s, RET_HEAD_DIM // 2, 1)
        for l in range(depth):
            x = _layer(x, layers[l], tabs, fg, l == depth - 1)
        outs.append(x)
    return tuple(outs)
```

```python
import functools

import jax
import jax.numpy as jnp
from jax import lax
from jax.experimental import pallas as pl
from jax.experimental.pallas import tpu as pltpu

D_MODEL = 1024
HEAD_DIM = 64
ATT_Q_HEADS = 8
ATT_KV_HEADS = 2
ATT_GROUP = ATT_Q_HEADS // ATT_KV_HEADS
ATT_Q_W = ATT_Q_HEADS * HEAD_DIM
ATT_KV_W = ATT_KV_HEADS * HEAD_DIM
WINDOW = 128
BLOCK = 128
RET_HEADS = 4
RET_HEAD_DIM = 128
RET_WIDTH = RET_HEADS * RET_HEAD_DIM
CHUNK = 128
D_FF = 2816
CONV_WIDTH = 3
ROPE_THETA = 10000.0
EPS = 1e-6
NEG_INF = -1e30
IN_WIDTH = ATT_Q_W + 2 * ATT_KV_W + 4 * RET_WIDTH + 2 * D_MODEL

LANES = 128
BF16_ROWS = 16
VMEM_LIMIT = 56 * 1024 * 1024

TOK_TILE = 512
FF_CHUNK = 256
STATE_CHUNKS = 4

F32 = jnp.float32
BF16 = jnp.bfloat16

_C_QA = 0
_C_KA = _C_QA + ATT_Q_W
_C_VA = _C_KA + ATT_KV_W
_C_QR = _C_VA + ATT_KV_W
_C_KR = _C_QR + RET_WIDTH
_C_VR = _C_KR + RET_WIDTH
_C_GR = _C_VR + RET_WIDTH
_C_GATE = _C_GR + RET_WIDTH


def _rmsnorm(x, g):
    return (x * lax.rsqrt(jnp.mean(x * x, axis=-1, keepdims=True) + EPS)) * g


def _dot(a, b):
    return jnp.dot(a, b, preferred_element_type=F32)


def _dot_nt(a, b):
    return lax.dot_general(a, b, (((1,), (1,)), ((), ())), preferred_element_type=F32)


def _dot_tn(a, b):
    return lax.dot_general(a, b, (((0,), (0,)), ((), ())), preferred_element_type=F32)


def _const_spec(shape):
    nd = len(shape)
    return pl.BlockSpec(shape, lambda *_: (0,) * nd, pipeline_mode=pl.Buffered(1))


def _inproj_kernel(x_ref, g_ref, w_ref, ca_ref, sa_ref, cr_ref, sr_ref,
                   qa_ref, ka_ref, va_ref, qr_ref, kr_ref, vr_ref, gr_ref, gate_ref):
    x = x_ref[0]
    hb = _rmsnorm(x, g_ref[...]).astype(BF16)
    t = x.shape[0]
    lane = lax.broadcasted_iota(jnp.int32, (t, LANES), 1)
    first_half = (lane % HEAD_DIM) < (HEAD_DIM // 2)
    ca, sa, cr, sr = ca_ref[...], sa_ref[...], cr_ref[...], sr_ref[...]

    def mm(c0, n):
        return _dot(hb, w_ref[:, c0:c0 + n])

    def rope_a(y):
        rot = jnp.where(first_half, pltpu.roll(y, LANES - HEAD_DIM // 2, 1),
                        pltpu.roll(y, HEAD_DIM // 2, 1))
        return y * ca + rot * sa

    def rope_r(y):
        return y * cr + pltpu.roll(y, RET_HEAD_DIM // 2, 1) * sr

    def slab(y, s):
        return y[:, s * LANES:(s + 1) * LANES]

    y = mm(_C_QA, ATT_Q_W)
    for s in range(ATT_Q_W // LANES):
        qa_ref[0, :, s * LANES:(s + 1) * LANES] = (rope_a(slab(y, s)) * (HEAD_DIM ** -0.5)).astype(BF16)
    y = mm(_C_KA, 2 * ATT_KV_W)
    ka_ref[0] = rope_a(slab(y, 0)).astype(BF16)
    va_ref[0] = slab(y, 1).astype(BF16)
    y = mm(_C_QR, RET_WIDTH)
    for s in range(RET_HEADS):
        qr_ref[0, :, s * LANES:(s + 1) * LANES] = rope_r(slab(y, s)).astype(BF16)
    y = mm(_C_KR, RET_WIDTH)
    for s in range(RET_HEADS):
        kr_ref[0, :, s * LANES:(s + 1) * LANES] = (rope_r(slab(y, s)) * (RET_HEAD_DIM ** -0.5)).astype(BF16)
    vr_ref[0] = mm(_C_VR, RET_WIDTH).astype(BF16)
    gr_ref[0] = mm(_C_GR, RET_WIDTH).astype(BF16)
    for c in range(4):
        n = 2 * D_MODEL // 4
        gate_ref[0, :, c * n:(c + 1) * n] = mm(_C_GATE + c * n, n).astype(BF16)


def _inproj(x, g, w, ca, sa, cr, sr):
    b, s, d = x.shape
    t = TOK_TILE
    tok = lambda width: pl.BlockSpec((1, t, width), lambda bi, i: (bi, i, 0))
    tab = pl.BlockSpec((t, LANES), lambda bi, i: (i, 0))
    widths = [ATT_Q_W, ATT_KV_W, ATT_KV_W, RET_WIDTH, RET_WIDTH, RET_WIDTH, RET_WIDTH, 2 * D_MODEL]
    return pl.pallas_call(
        _inproj_kernel,
        grid=(b, s // t),
        in_specs=[tok(d), _const_spec((1, d)), _const_spec((d, IN_WIDTH)), tab, tab, tab, tab],
        out_specs=[tok(wd) for wd in widths],
        out_shape=[jax.ShapeDtypeStruct((b, s, wd), BF16) for wd in widths],
        compiler_params=pltpu.CompilerParams(
            dimension_semantics=("arbitrary", "arbitrary"), vmem_limit_bytes=VMEM_LIMIT),
        name="inproj",
    )(x, g, w, ca, sa, cr, sr)


def _state_kernel(kf_ref, vf_ref, kb_ref, vb_ref, lf_ref, lb_ref, rf_ref, rb_ref, st_ref):
    @pl.when(pl.program_id(1) == 0)
    def _():
        st_ref[...] = jnp.zeros_like(st_ref)

    row = lax.broadcasted_iota(jnp.int32, (CHUNK, RET_WIDTH), 0).astype(F32)
    lf = lf_ref[...]
    lb = lb_ref[...]
    zeta_f = jnp.exp(lf * (CHUNK - 1.0 - row))
    zeta_b = jnp.exp(lb * row)
    dec_f = jnp.exp(lf * float(CHUNK))
    dec_b = jnp.exp(lb * float(CHUNK))

    def update(k_ref, v_ref, out_ref, c, zeta, dec, d):
        rows = slice(c * CHUNK, (c + 1) * CHUNK)
        for h in range(RET_HEADS):
            cols = slice(h * RET_HEAD_DIM, (h + 1) * RET_HEAD_DIM)
            st = st_ref[d, h]
            out_ref[0, c, h] = st.astype(BF16)
            zv = (v_ref[0, rows, cols].astype(F32) * zeta[:, cols]).astype(BF16)
            kv = _dot_tn(k_ref[0, rows, cols], zv)
            st_ref[d, h] = st * dec[:, cols] + kv

    for c in range(STATE_CHUNKS):
        update(kf_ref, vf_ref, rf_ref, c, zeta_f, dec_f, 0)
    for c in reversed(range(STATE_CHUNKS)):
        update(kb_ref, vb_ref, rb_ref, c, zeta_b, dec_b, 1)


def _states(kr, vr, lf_l, lb_l):
    b, s, _ = kr.shape
    nc = s // CHUNK
    ns = nc // STATE_CHUNKS
    t = STATE_CHUNKS * CHUNK
    fwd = pl.BlockSpec((1, t, RET_WIDTH), lambda bi, i: (bi, i, 0))
    bwd = pl.BlockSpec((1, t, RET_WIDTH), lambda bi, i: (bi, ns - 1 - i, 0))
    st_shape = (1, STATE_CHUNKS, RET_HEADS, RET_HEAD_DIM, RET_HEAD_DIM)
    out_f = pl.BlockSpec(st_shape, lambda bi, i: (bi, i, 0, 0, 0))
    out_b = pl.BlockSpec(st_shape, lambda bi, i: (bi, ns - 1 - i, 0, 0, 0))
    full = jax.ShapeDtypeStruct((b, nc, RET_HEADS, RET_HEAD_DIM, RET_HEAD_DIM), BF16)
    return pl.pallas_call(
        _state_kernel,
        grid=(b, ns),
        in_specs=[fwd, fwd, bwd, bwd, _const_spec((1, RET_WIDTH)), _const_spec((1, RET_WIDTH))],
        out_specs=[out_f, out_b],
        out_shape=[full, full],
        scratch_shapes=[pltpu.VMEM((2, RET_HEADS, RET_HEAD_DIM, RET_HEAD_DIM), F32)],
        compiler_params=pltpu.CompilerParams(
            dimension_semantics=("arbitrary", "arbitrary"), vmem_limit_bytes=VMEM_LIMIT),
        name="ret_states",
    )(kr, vr, kr, vr, lf_l, lb_l)


def _mixer_kernel(nb_total,
                  qa_ref, kp_ref, kc_ref, kn_ref, vp_ref, vc_ref, vn_ref,
                  qr_ref, kr_ref, vr_ref, gr_ref, gate_ref, rf_ref, rb_ref, x_ref,
                  sink_ref, lf_ref, lb_ref, rng_ref, wba_ref, wbr_ref, wo_ref,
                  o_ref,
                  kv_s, vv_s, dm_s, xi_s, att_s, ret_s, mrg_s):
    t = x_ref.shape[1]
    nblk = t // BLOCK
    tile = pl.program_id(1)

    lane_k = lax.broadcasted_iota(jnp.int32, (t + 2 * BLOCK, LANES), 1)
    low = lane_k < HEAD_DIM
    for src, dst in ((jnp.concatenate([kp_ref[0], kc_ref[0], kn_ref[0]], axis=0), kv_s),
                     (jnp.concatenate([vp_ref[0], vc_ref[0], vn_ref[0]], axis=0), vv_s)):
        src = src.astype(F32)
        h0 = jnp.where(low, src, 0.0)
        h1 = jnp.where(low, 0.0, src)
        dst[0] = h0.astype(BF16)
        dst[1] = pltpu.roll(h0, HEAD_DIM, 1).astype(BF16)
        dst[2] = pltpu.roll(h1, HEAD_DIM, 1).astype(BF16)
        dst[3] = h1.astype(BF16)

    ri = lax.broadcasted_iota(jnp.int32, (CHUNK, CHUNK), 0).astype(F32)
    ci = lax.broadcasted_iota(jnp.int32, (CHUNK, CHUNK), 1).astype(F32)
    diff = ri - ci
    for h in range(RET_HEADS):
        lf = lf_ref[h:h + 1, :]
        lb = lb_ref[h:h + 1, :]
        dm_s[h] = (jnp.where(diff >= 0, jnp.exp(lf * jnp.maximum(diff, 0.0)), 0.0)
                   + jnp.where(diff < 0, jnp.exp(lb * jnp.maximum(-diff, 0.0)), 0.0))
        xi_s[0, h] = jnp.exp(lf * (ri + 1.0))
        xi_s[1, h] = jnp.exp(lb * (float(CHUNK) - ri))

    qi = lax.broadcasted_iota(jnp.int32, (BLOCK, 3 * BLOCK), 0)
    kj = lax.broadcasted_iota(jnp.int32, (BLOCK, 3 * BLOCK), 1)
    rel = kj - BLOCK - qi
    band = jnp.abs(rel) <= WINDOW
    lane_o = lax.broadcasted_iota(jnp.int32, (BLOCK, LANES), 1)
    low_o = lane_o < HEAD_DIM

    def block_body(j, carry):
        r0 = pl.multiple_of(j * BLOCK, BLOCK)
        rows = pl.ds(r0, BLOCK)
        krows = pl.ds(r0, 3 * BLOCK)
        gblk = tile * nblk + j
        mask = band & ((kj >= BLOCK) | (gblk > 0)) & ((kj < 2 * BLOCK) | (gblk < nb_total - 1))

        for s in range(ATT_Q_HEADS // 2):
            q = qa_ref[0, rows, s * LANES:(s + 1) * LANES]
            kvh = (2 * s) // ATT_GROUP
            acc = None
            inv = []
            for pos in range(2):
                hq = 2 * s + pos
                sc = _dot_nt(q, kv_s[2 * kvh + pos, krows, :])
                sc = jnp.where(mask, sc, NEG_INF)
                sink = sink_ref[hq]
                m = jnp.maximum(jnp.max(sc, axis=-1, keepdims=True), sink)
                p = jnp.exp(sc - m)
                den = jnp.sum(p, axis=-1, keepdims=True) + jnp.exp(sink - m)
                inv.append(1.0 / den)
                pv = _dot(p.astype(BF16), vv_s[2 * kvh + pos, krows, :])
                acc = pv if acc is None else acc + pv
            att_s[rows, s * LANES:(s + 1) * LANES] = (acc * jnp.where(low_o, inv[0], inv[1])).astype(BF16)

        for h in range(RET_HEADS):
            cols = slice(h * LANES, (h + 1) * LANES)
            q = qr_ref[0, rows, cols]
            k = kr_ref[0, rows, cols]
            v = vr_ref[0, rows, cols]
            inner = (_dot_nt(q, k) * dm_s[h]).astype(BF16)
            o = _dot(inner, v)
            cross = _dot(q, jnp.concatenate([rf_ref[0, j, h], rb_ref[0, j, h]], axis=1))
            o = o + cross[:, :LANES] * xi_s[0, h] + cross[:, LANES:] * xi_s[1, h]
            o = o * lax.rsqrt(jnp.mean(o * o, axis=-1, keepdims=True) + EPS)
            g = gr_ref[0, rows, cols].astype(F32)
            o = (o * rng_ref[:, cols]) * (g * jax.nn.sigmoid(g))
            ret_s[rows, cols] = o.astype(BF16)
        return carry

    lax.fori_loop(0, nblk, block_body, 0, unroll=True)

    nchunk = 256
    for c in range(D_MODEL // nchunk):
        cols = slice(c * nchunk, (c + 1) * nchunk)
        ba = _dot(att_s[...], wba_ref[:, cols])
        br = _dot(ret_s[...], wbr_ref[:, cols])
        g0 = jax.nn.sigmoid(gate_ref[0, :, c * nchunk:(c + 1) * nchunk].astype(F32))
        g1 = jax.nn.sigmoid(gate_ref[0, :, D_MODEL + c * nchunk:D_MODEL + (c + 1) * nchunk].astype(F32))
        mrg_s[:, cols] = (g0 * ba + g1 * br).astype(BF16)
    o_ref[0] = x_ref[0] + _dot(mrg_s[...], wo_ref[...])


def _mixer(qa, ka, va, qr, kr, vr, gr, gate, rf, rb, x, sink, lf_h, lb_h, rng, wba, wbr, wo):
    b, s, d = x.shape
    t = TOK_TILE
    nblk = t // BLOCK
    nb_total = s // BLOCK
    tok = lambda width: pl.BlockSpec((1, t, width), lambda bi, i: (bi, i, 0))
    prev = pl.BlockSpec((1, BLOCK, ATT_KV_W), lambda bi, i: (bi, jnp.maximum(i * nblk - 1, 0), 0))
    nxt = pl.BlockSpec((1, BLOCK, ATT_KV_W), lambda bi, i: (bi, jnp.minimum((i + 1) * nblk, nb_total - 1), 0))
    st = pl.BlockSpec((1, nblk, RET_HEADS, RET_HEAD_DIM, RET_HEAD_DIM), lambda bi, i: (bi, i, 0, 0, 0))
    smem = pl.BlockSpec(memory_space=pltpu.SMEM)
    return pl.pallas_call(
        functools.partial(_mixer_kernel, nb_total),
        grid=(b, s // t),
        in_specs=[tok(ATT_Q_W), prev, tok(ATT_KV_W), nxt, prev, tok(ATT_KV_W), nxt,
                  tok(RET_WIDTH), tok(RET_WIDTH), tok(RET_WIDTH), tok(RET_WIDTH), tok(2 * D_MODEL),
                  st, st, tok(d),
                  smem, _const_spec((RET_HEADS, LANES)), _const_spec((RET_HEADS, LANES)),
                  _const_spec((1, RET_WIDTH)),
                  _const_spec((ATT_Q_W, d)), _const_spec((RET_WIDTH, d)), _const_spec((d, d))],
        out_specs=tok(d),
        out_shape=jax.ShapeDtypeStruct((b, s, d), F32),
        scratch_shapes=[
            pltpu.VMEM((4, t + 2 * BLOCK, LANES), BF16),
            pltpu.VMEM((4, t + 2 * BLOCK, LANES), BF16),
            pltpu.VMEM((RET_HEADS, CHUNK, CHUNK), F32),
            pltpu.VMEM((2, RET_HEADS, CHUNK, CHUNK), F32),
            pltpu.VMEM((t, ATT_Q_W), BF16),
            pltpu.VMEM((t, RET_WIDTH), BF16),
            pltpu.VMEM((t, d), BF16),
        ],
        compiler_params=pltpu.CompilerParams(
            dimension_semantics=("arbitrary", "arbitrary"), vmem_limit_bytes=VMEM_LIMIT),
        name="mixer",
    )(qa, ka, ka, ka, va, va, va, qr, kr, vr, gr, gate, rf, rb, x, sink, lf_h, lb_h, rng, wba, wbr, wo)


def _ffn_kernel(final_norm, xp_ref, x_ref, xn_ref, g_ref, wi_ref, cw_ref, cb_ref, wo_ref, fg_ref,
                o_ref, gu_s):
    t = x_ref.shape[1]
    i = pl.program_id(1)
    n = pl.num_programs(1)
    g = g_ref[...]
    x = x_ref[0]
    hp = jnp.where(i > 0, _rmsnorm(xp_ref[0], g), 0.0)
    hn = jnp.where(i < n - 1, _rmsnorm(xn_ref[0], g), 0.0)
    h = _rmsnorm(x, g).astype(BF16)
    h_ext = jnp.concatenate([hp.astype(BF16), h, hn.astype(BF16)], axis=0)

    def chunk_body(c, carry):
        c0 = pl.multiple_of(c * FF_CHUNK, FF_CHUNK)
        a_ext = _dot(h_ext, wi_ref[:, pl.ds(c0, FF_CHUNK)])
        u = _dot(h, wi_ref[:, pl.ds(D_FF + c0, FF_CHUNK)])
        w = cw_ref[:, pl.ds(c0, FF_CHUNK)]
        lo = BF16_ROWS
        a = (cb_ref[:, pl.ds(c0, FF_CHUNK)]
             + a_ext[lo - 1:lo - 1 + t] * w[0:1]
             + a_ext[lo:lo + t] * w[1:2]
             + a_ext[lo + 1:lo + 1 + t] * w[2:3])
        gelu = 0.5 * a * (1.0 + lax.erf(a * (2.0 ** -0.5)))
        gu_s[:, pl.ds(c0, FF_CHUNK)] = (gelu * u).astype(BF16)
        return carry

    lax.fori_loop(0, D_FF // FF_CHUNK, chunk_body, 0, unroll=True)
    y = x + _dot(gu_s[...], wo_ref[...])
    if final_norm:
        y = _rmsnorm(y, fg_ref[...])
    o_ref[0] = y


def _ffn(x, g, wi, cw, cb, wo, fg, final_norm):
    b, s, d = x.shape
    t = TOK_TILE
    hb = t // BF16_ROWS
    nh = s // BF16_ROWS
    tok = pl.BlockSpec((1, t, d), lambda bi, i: (bi, i, 0))
    prev = pl.BlockSpec((1, BF16_ROWS, d), lambda bi, i: (bi, jnp.maximum(i * hb - 1, 0), 0))
    nxt = pl.BlockSpec((1, BF16_ROWS, d), lambda bi, i: (bi, jnp.minimum((i + 1) * hb, nh - 1), 0))
    return pl.pallas_call(
        functools.partial(_ffn_kernel, final_norm),
        grid=(b, s // t),
        in_specs=[prev, tok, nxt, _const_spec((1, d)), _const_spec((d, 2 * D_FF)),
                  _const_spec((CONV_WIDTH, D_FF)), _const_spec((1, D_FF)), _const_spec((D_FF, d)),
                  _const_spec((1, d))],
        out_specs=tok,
        out_shape=jax.ShapeDtypeStruct((b, s, d), F32),
        scratch_shapes=[pltpu.VMEM((t, D_FF), BF16)],
        compiler_params=pltpu.CompilerParams(
            dimension_semantics=("arbitrary", "arbitrary"), vmem_limit_bytes=VMEM_LIMIT),
        name="ffn",
    )(x, x, x, g, wi, cw, cb, wo, fg)


def _rope_tables(s, half, reps):
    freqs = ROPE_THETA ** (-jnp.arange(half, dtype=F32) / half)
    ang = jnp.arange(s, dtype=F32)[:, None] * freqs[None, :]
    cos = jnp.cos(ang)
    sin = jnp.sin(ang)
    cos_t = jnp.tile(jnp.concatenate([cos, cos], axis=-1), (1, reps))
    sin_t = jnp.tile(jnp.concatenate([-sin, sin], axis=-1), (1, reps))
    return cos_t, sin_t


def _layer(x, p, tabs, final_g, final_norm):
    ca, sa, cr, sr = tabs
    qa, ka, va, qr, kr, vr, gr, gate = _inproj(x, p["norm_mix_g"], p["w_in"], ca, sa, cr, sr)
    rf, rb = _states(kr, vr, p["lf_lanes"], p["lb_lanes"])
    x = _mixer(qa, ka, va, qr, kr, vr, gr, gate, rf, rb, x, p["sink"], p["lf_heads"], p["lb_heads"],
               p["ret_norm_g"], p["w_branch_attn"], p["w_branch_ret"], p["w_out"])
    return _ffn(x, p["norm_ffn_g"], p["w_ffn_in"], p["conv_w"], p["conv_b"], p["w_ffn_out"],
                final_g, final_norm)


def kernel(x_prompt, x_sample, norm_mix_g, w_in, attn_sink, ret_log_decay_f, ret_log_decay_b, ret_norm_g,
           w_branch_attn, w_branch_ret, w_out, norm_ffn_g, w_ffn_in, conv_w, conv_b, w_ffn_out, final_norm_g):
    depth = w_in.shape[0]
    layers = []
    for l in range(depth):
        lf = ret_log_decay_f[l].astype(F32)
        lb = ret_log_decay_b[l].astype(F32)
        layers.append(dict(
            norm_mix_g=norm_mix_g[l].reshape(1, D_MODEL),
            w_in=w_in[l].astype(BF16),
            sink=attn_sink[l].astype(F32),
            lf_lanes=jnp.repeat(lf, RET_HEAD_DIM).reshape(1, RET_WIDTH),
            lb_lanes=jnp.repeat(lb, RET_HEAD_DIM).reshape(1, RET_WIDTH),
            lf_heads=jnp.broadcast_to(lf[:, None], (RET_HEADS, LANES)),
            lb_heads=jnp.broadcast_to(lb[:, None], (RET_HEADS, LANES)),
            ret_norm_g=ret_norm_g[l].reshape(1, RET_WIDTH),
            w_branch_attn=w_branch_attn[l].astype(BF16),
            w_branch_ret=w_branch_ret[l].astype(BF16),
            w_out=w_out[l].astype(BF16),
            norm_ffn_g=norm_ffn_g[l].reshape(1, D_MODEL),
            w_ffn_in=w_ffn_in[l].astype(BF16),
            conv_w=conv_w[l],
            conv_b=conv_b[l].reshape(1, D_FF),
            w_ffn_out=w_ffn_out[l].astype(BF16),
        ))
    fg = final_norm_g.reshape(1, D_MODEL)
    outs = []
    for x in (x_prompt, x_sample):
        s = x.shape[1]
        tabs = _rope_tables(s, HEAD_DIM // 2, LANES // HEAD_DIM) + _rope_tables(s, RET_HEAD_DIM // 2, 1)
        for l in range(depth):
            x = _layer(x, layers[l], tabs, fg, l == depth - 1)
        outs.append(x)
    return tuple(outs)
```

```python
import functools

import jax
import jax.numpy as jnp
from jax import lax
from jax.experimental import pallas as pl
from jax.experimental.pallas import tpu as pltpu

D_MODEL = 1024
HEAD_DIM = 64
ATT_Q_HEADS = 8
ATT_KV_HEADS = 2
ATT_GROUP = ATT_Q_HEADS // ATT_KV_HEADS
ATT_Q_W = ATT_Q_HEADS * HEAD_DIM
ATT_KV_W = ATT_KV_HEADS * HEAD_DIM
WINDOW = 128
BLOCK = 128
RET_HEADS = 4
RET_HEAD_DIM = 128
RET_WIDTH = RET_HEADS * RET_HEAD_DIM
CHUNK = 128
D_FF = 2816
CONV_WIDTH = 3
ROPE_THETA = 10000.0
EPS = 1e-6
NEG_INF = -1e30
IN_WIDTH = ATT_Q_W + 2 * ATT_KV_W + 4 * RET_WIDTH + 2 * D_MODEL

LANES = 128
BF16_ROWS = 16
VMEM_LIMIT = 56 * 1024 * 1024

TOK_TILE = 512
FF_CHUNK = 256
STATE_CHUNKS = 4

F32 = jnp.float32
BF16 = jnp.bfloat16

_C_QA = 0
_C_KA = _C_QA + ATT_Q_W
_C_VA = _C_KA + ATT_KV_W
_C_QR = _C_VA + ATT_KV_W
_C_KR = _C_QR + RET_WIDTH
_C_VR = _C_KR + RET_WIDTH
_C_GR = _C_VR + RET_WIDTH
_C_GATE = _C_GR + RET_WIDTH


def _rmsnorm(x, g):
    return (x * lax.rsqrt(jnp.mean(x * x, axis=-1, keepdims=True) + EPS)) * g


def _dot(a, b):
    return jnp.dot(a, b, preferred_element_type=F32)


def _dot_nt(a, b):
    return lax.dot_general(a, b, (((1,), (1,)), ((), ())), preferred_element_type=F32)


def _dot_tn(a, b):
    return lax.dot_general(a, b, (((0,), (0,)), ((), ())), preferred_element_type=F32)


def _const_spec(shape):
    nd = len(shape)
    return pl.BlockSpec(shape, lambda *_: (0,) * nd, pipeline_mode=pl.Buffered(1))


def _inproj_kernel(x_ref, g_ref, w_ref, ca_ref, sa_ref, cr_ref, sr_ref,
                   qa_ref, ka_ref, va_ref, qr_ref, kr_ref, vr_ref, gr_ref, gate_ref):
    x = x_ref[0]
    hb = _rmsnorm(x, g_ref[...]).astype(BF16)
    t = x.shape[0]
    lane = lax.broadcasted_iota(jnp.int32, (t, LANES), 1)
    first_half = (lane % HEAD_DIM) < (HEAD_DIM // 2)
    ca, sa, cr, sr = ca_ref[...], sa_ref[...], cr_ref[...], sr_ref[...]

    def mm(c0, n):
        return _dot(hb, w_ref[:, c0:c0 + n])

    def rope_a(y):
        rot = jnp.where(first_half, pltpu.roll(y, LANES - HEAD_DIM // 2, 1),
                        pltpu.roll(y, HEAD_DIM // 2, 1))
        return y * ca + rot * sa

    def rope_r(y):
        return y * cr + pltpu.roll(y, RET_HEAD_DIM // 2, 1) * sr

    def slab(y, s):
        return y[:, s * LANES:(s + 1) * LANES]

    low = lane < HEAD_DIM
    y = mm(_C_QA, ATT_Q_W)
    for s in range(ATT_Q_W // LANES):
        r = rope_a(slab(y, s)) * (HEAD_DIM ** -0.5)
        r_sw = pltpu.roll(r, HEAD_DIM, 1)
        kv_low = (2 * s) // ATT_GROUP == 0
        h0 = jnp.where(low, r, 0.0) if kv_low else jnp.where(low, 0.0, r_sw)
        h1 = jnp.where(low, r_sw, 0.0) if kv_low else jnp.where(low, 0.0, r)
        qa_ref[0, :, (2 * s) * LANES:(2 * s + 1) * LANES] = h0.astype(BF16)
        qa_ref[0, :, (2 * s + 1) * LANES:(2 * s + 2) * LANES] = h1.astype(BF16)
    y = mm(_C_KA, 2 * ATT_KV_W)
    ka_ref[0] = rope_a(slab(y, 0)).astype(BF16)
    va_ref[0] = slab(y, 1).astype(BF16)
    y = mm(_C_QR, RET_WIDTH)
    for s in range(RET_HEADS):
        qr_ref[0, :, s * LANES:(s + 1) * LANES] = rope_r(slab(y, s)).astype(BF16)
    y = mm(_C_KR, RET_WIDTH)
    for s in range(RET_HEADS):
        kr_ref[0, :, s * LANES:(s + 1) * LANES] = (rope_r(slab(y, s)) * (RET_HEAD_DIM ** -0.5)).astype(BF16)
    vr_ref[0] = mm(_C_VR, RET_WIDTH).astype(BF16)
    gr_ref[0] = mm(_C_GR, RET_WIDTH).astype(BF16)
    for c in range(4):
        n = 2 * D_MODEL // 4
        gate_ref[0, :, c * n:(c + 1) * n] = mm(_C_GATE + c * n, n).astype(BF16)


def _inproj(x, g, w, ca, sa, cr, sr):
    b, s, d = x.shape
    t = TOK_TILE
    tok = lambda width: pl.BlockSpec((1, t, width), lambda bi, i: (bi, i, 0))
    tab = pl.BlockSpec((t, LANES), lambda bi, i: (i, 0))
    widths = [ATT_Q_HEADS * LANES, ATT_KV_W, ATT_KV_W, RET_WIDTH, RET_WIDTH, RET_WIDTH, RET_WIDTH, 2 * D_MODEL]
    return pl.pallas_call(
        _inproj_kernel,
        grid=(b, s // t),
        in_specs=[tok(d), _const_spec((1, d)), _const_spec((d, IN_WIDTH)), tab, tab, tab, tab],
        out_specs=[tok(wd) for wd in widths],
        out_shape=[jax.ShapeDtypeStruct((b, s, wd), BF16) for wd in widths],
        compiler_params=pltpu.CompilerParams(
            dimension_semantics=("arbitrary", "arbitrary"), vmem_limit_bytes=VMEM_LIMIT),
        name="inproj",
    )(x, g, w, ca, sa, cr, sr)


def _state_kernel(kf_ref, vf_ref, kb_ref, vb_ref, lf_ref, lb_ref, rf_ref, rb_ref, st_ref):
    @pl.when(pl.program_id(1) == 0)
    def _():
        st_ref[...] = jnp.zeros_like(st_ref)

    row = lax.broadcasted_iota(jnp.int32, (CHUNK, RET_WIDTH), 0).astype(F32)
    lf = lf_ref[...]
    lb = lb_ref[...]
    zeta_f = jnp.exp(lf * (CHUNK - 1.0 - row))
    zeta_b = jnp.exp(lb * row)
    dec_f = jnp.exp(lf * float(CHUNK))
    dec_b = jnp.exp(lb * float(CHUNK))

    def update(k_ref, v_ref, out_ref, c, zeta, dec, d):
        rows = slice(c * CHUNK, (c + 1) * CHUNK)
        for h in range(RET_HEADS):
            cols = slice(h * RET_HEAD_DIM, (h + 1) * RET_HEAD_DIM)
            st = st_ref[d, h]
            out_ref[0, c, h] = st.astype(BF16)
            zv = (v_ref[0, rows, cols].astype(F32) * zeta[:, cols]).astype(BF16)
            kv = _dot_tn(k_ref[0, rows, cols], zv)
            st_ref[d, h] = st * dec[:, cols] + kv

    for c in range(STATE_CHUNKS):
        update(kf_ref, vf_ref, rf_ref, c, zeta_f, dec_f, 0)
    for c in reversed(range(STATE_CHUNKS)):
        update(kb_ref, vb_ref, rb_ref, c, zeta_b, dec_b, 1)


def _states(kr, vr, lf_l, lb_l):
    b, s, _ = kr.shape
    nc = s // CHUNK
    ns = nc // STATE_CHUNKS
    t = STATE_CHUNKS * CHUNK
    fwd = pl.BlockSpec((1, t, RET_WIDTH), lambda bi, i: (bi, i, 0))
    bwd = pl.BlockSpec((1, t, RET_WIDTH), lambda bi, i: (bi, ns - 1 - i, 0))
    st_shape = (1, STATE_CHUNKS, RET_HEADS, RET_HEAD_DIM, RET_HEAD_DIM)
    out_f = pl.BlockSpec(st_shape, lambda bi, i: (bi, i, 0, 0, 0))
    out_b = pl.BlockSpec(st_shape, lambda bi, i: (bi, ns - 1 - i, 0, 0, 0))
    full = jax.ShapeDtypeStruct((b, nc, RET_HEADS, RET_HEAD_DIM, RET_HEAD_DIM), BF16)
    return pl.pallas_call(
        _state_kernel,
        grid=(b, ns),
        in_specs=[fwd, fwd, bwd, bwd, _const_spec((1, RET_WIDTH)), _const_spec((1, RET_WIDTH))],
        out_specs=[out_f, out_b],
        out_shape=[full, full],
        scratch_shapes=[pltpu.VMEM((2, RET_HEADS, RET_HEAD_DIM, RET_HEAD_DIM), F32)],
        compiler_params=pltpu.CompilerParams(
            dimension_semantics=("arbitrary", "arbitrary"), vmem_limit_bytes=VMEM_LIMIT),
        name="ret_states",
    )(kr, vr, kr, vr, lf_l, lb_l)


def _mixer_kernel(qa_ref, kp_ref, kc_ref, kn_ref, vp_ref, vc_ref, vn_ref,
                  qr_ref, kr_ref, vr_ref, gr_ref, gate_ref, rf_ref, rb_ref, x_ref,
                  sink_ref, lf_ref, lb_ref, rng_ref, wba_ref, wbr_ref, wo_ref,
                  o_ref,
                  k_s, vt_s, dm_s, xi_s, cap_s, s_buf, p_buf, att_s, ret_s, mrg_s):
    t = x_ref.shape[1]
    nblk = t // BLOCK
    tile = pl.program_id(1)

    k_s[0:BLOCK] = kp_ref[0]
    k_s[BLOCK:BLOCK + t] = kc_ref[0]
    k_s[BLOCK + t:] = kn_ref[0]
    for i in range(nblk + 2):
        src = vp_ref if i == 0 else (vn_ref if i == nblk + 1 else vc_ref)
        r0 = 0 if i in (0, nblk + 1) else (i - 1) * BLOCK
        vt_s[:, i * BLOCK:(i + 1) * BLOCK] = src[0, r0:r0 + BLOCK, :].astype(F32).T.astype(BF16)

    ri = lax.broadcasted_iota(jnp.int32, (CHUNK, CHUNK), 0).astype(F32)
    ci = lax.broadcasted_iota(jnp.int32, (CHUNK, CHUNK), 1).astype(F32)
    diff = ri - ci
    for h in range(RET_HEADS):
        lf = lf_ref[h:h + 1, :]
        lb = lb_ref[h:h + 1, :]
        dm_s[h] = (jnp.where(diff >= 0, jnp.exp(lf * jnp.maximum(diff, 0.0)), 0.0)
                   + jnp.where(diff < 0, jnp.exp(lb * jnp.maximum(-diff, 0.0)), 0.0))
        xi_s[0, h] = jnp.exp(lf * (ri + 1.0))
        xi_s[1, h] = jnp.exp(lb * (float(CHUNK) - ri))

    kj = lax.broadcasted_iota(jnp.int32, (3 * BLOCK, BLOCK), 0)
    qi = lax.broadcasted_iota(jnp.int32, (3 * BLOCK, BLOCK), 1)
    band = jnp.abs(kj - BLOCK - qi) <= WINDOW
    first = tile == 0
    last = tile == pl.num_programs(1) - 1
    cap_s[0] = jnp.where(band, jnp.inf, NEG_INF)
    cap_s[1] = jnp.where(band & ((kj >= BLOCK) | jnp.logical_not(first)), jnp.inf, NEG_INF)
    cap_s[2] = jnp.where(band & ((kj < 2 * BLOCK) | jnp.logical_not(last)), jnp.inf, NEG_INF)
    head1 = lax.broadcasted_iota(jnp.int32, (1, 2 * BLOCK), 1) >= BLOCK
    low_o = lax.broadcasted_iota(jnp.int32, (BLOCK, LANES), 1) < HEAD_DIM

    n_slab = ATT_Q_HEADS // 2
    units = [(j, s) for j in range(nblk) for s in range(n_slab)]

    def att_scores(u):
        j, s = units[u]
        rows = slice(j * BLOCK, (j + 1) * BLOCK)
        q2 = jnp.concatenate([qa_ref[0, rows, (2 * s) * LANES:(2 * s + 1) * LANES],
                              qa_ref[0, rows, (2 * s + 1) * LANES:(2 * s + 2) * LANES]], axis=0)
        s_buf[u % 2] = _dot_nt(k_s[j * BLOCK:(j + 3) * BLOCK, :], q2)

    def att_softmax(u):
        j, s = units[u]
        cap = cap_s[1 if j == 0 else (2 if j == nblk - 1 else 0)]
        sc = s_buf[u % 2]
        sc = jnp.concatenate([jnp.minimum(sc[:, :BLOCK], cap), jnp.minimum(sc[:, BLOCK:], cap)], axis=1)
        sink = jnp.where(head1, sink_ref[2 * s + 1], sink_ref[2 * s])
        m = jnp.maximum(jnp.max(sc, axis=0, keepdims=True), sink)
        p = jnp.exp(sc - m)
        den = jnp.sum(p, axis=0, keepdims=True) + jnp.exp(sink - m)
        p_buf[u % 2] = p.astype(BF16)
        return 1.0 / den

    def att_values(u, inv):
        j, s = units[u]
        rows = slice(j * BLOCK, (j + 1) * BLOCK)
        ot = _dot(vt_s[:, j * BLOCK:(j + 3) * BLOCK], p_buf[u % 2]) * inv
        o0 = ot[:, :BLOCK].T
        o1 = ot[:, BLOCK:].T
        if (2 * s) // ATT_GROUP == 0:
            o = jnp.where(low_o, o0, pltpu.roll(o1, HEAD_DIM, 1))
        else:
            o = jnp.where(low_o, pltpu.roll(o0, HEAD_DIM, 1), o1)
        att_s[rows, s * LANES:(s + 1) * LANES] = o.astype(BF16)

    def retention(j, h):
        rows = slice(j * BLOCK, (j + 1) * BLOCK)
        cols = slice(h * LANES, (h + 1) * LANES)
        q = qr_ref[0, rows, cols]
        k = kr_ref[0, rows, cols]
        v = vr_ref[0, rows, cols]
        inner = (_dot_nt(q, k) * dm_s[h]).astype(BF16)
        o = _dot(inner, v)
        cross = _dot(q, jnp.concatenate([rf_ref[0, j, h], rb_ref[0, j, h]], axis=1))
        o = o + cross[:, :LANES] * xi_s[0, h] + cross[:, LANES:] * xi_s[1, h]
        o = o * lax.rsqrt(jnp.mean(o * o, axis=-1, keepdims=True) + EPS)
        g = gr_ref[0, rows, cols].astype(F32)
        o = (o * rng_ref[:, cols]) * (g * jax.nn.sigmoid(g))
        ret_s[rows, cols] = o.astype(BF16)

    ret_units = [(j, h) for j in range(nblk) for h in range(RET_HEADS)]
    n_units = len(units)
    inv = {}
    for k in range(n_units + 2):
        if k < n_units:
            att_scores(k)
        if k >= 2:
            att_values(k - 2, inv.pop(k - 2))
        if k < len(ret_units):
            retention(*ret_units[k])
        if 1 <= k <= n_units:
            inv[k - 1] = att_softmax(k - 1)

    nchunk = 256
    for c in range(D_MODEL // nchunk):
        cols = slice(c * nchunk, (c + 1) * nchunk)
        ba = _dot(att_s[...], wba_ref[:, cols])
        br = _dot(ret_s[...], wbr_ref[:, cols])
        g0 = jax.nn.sigmoid(gate_ref[0, :, c * nchunk:(c + 1) * nchunk].astype(F32))
        g1 = jax.nn.sigmoid(gate_ref[0, :, D_MODEL + c * nchunk:D_MODEL + (c + 1) * nchunk].astype(F32))
        mrg_s[:, cols] = (g0 * ba + g1 * br).astype(BF16)
    o_ref[0] = x_ref[0] + _dot(mrg_s[...], wo_ref[...])


def _mixer(qa, ka, va, qr, kr, vr, gr, gate, rf, rb, x, sink, lf_h, lb_h, rng, wba, wbr, wo):
    b, s, d = x.shape
    t = TOK_TILE
    nblk = t // BLOCK
    nb_total = s // BLOCK
    tok = lambda width: pl.BlockSpec((1, t, width), lambda bi, i: (bi, i, 0))
    prev = pl.BlockSpec((1, BLOCK, ATT_KV_W), lambda bi, i: (bi, jnp.maximum(i * nblk - 1, 0), 0))
    nxt = pl.BlockSpec((1, BLOCK, ATT_KV_W), lambda bi, i: (bi, jnp.minimum((i + 1) * nblk, nb_total - 1), 0))
    st = pl.BlockSpec((1, nblk, RET_HEADS, RET_HEAD_DIM, RET_HEAD_DIM), lambda bi, i: (bi, i, 0, 0, 0))
    smem = pl.BlockSpec(memory_space=pltpu.SMEM)
    return pl.pallas_call(
        _mixer_kernel,
        grid=(b, s // t),
        in_specs=[tok(ATT_Q_HEADS * LANES), prev, tok(ATT_KV_W), nxt, prev, tok(ATT_KV_W), nxt,
                  tok(RET_WIDTH), tok(RET_WIDTH), tok(RET_WIDTH), tok(RET_WIDTH), tok(2 * D_MODEL),
                  st, st, tok(d),
                  smem, _const_spec((RET_HEADS, LANES)), _const_spec((RET_HEADS, LANES)),
                  _const_spec((1, RET_WIDTH)),
                  _const_spec((ATT_Q_W, d)), _const_spec((RET_WIDTH, d)), _const_spec((d, d))],
        out_specs=tok(d),
        out_shape=jax.ShapeDtypeStruct((b, s, d), F32),
        scratch_shapes=[
            pltpu.VMEM((t + 2 * BLOCK, ATT_KV_W), BF16),
            pltpu.VMEM((ATT_KV_W, t + 2 * BLOCK), BF16),
            pltpu.VMEM((RET_HEADS, CHUNK, CHUNK), F32),
            pltpu.VMEM((2, RET_HEADS, CHUNK, CHUNK), F32),
            pltpu.VMEM((3, 3 * BLOCK, BLOCK), F32),
            pltpu.VMEM((2, 3 * BLOCK, 2 * BLOCK), F32),
            pltpu.VMEM((2, 3 * BLOCK, 2 * BLOCK), BF16),
            pltpu.VMEM((t, ATT_Q_W), BF16),
            pltpu.VMEM((t, RET_WIDTH), BF16),
            pltpu.VMEM((t, d), BF16),
        ],
        compiler_params=pltpu.CompilerParams(
            dimension_semantics=("arbitrary", "arbitrary"), vmem_limit_bytes=VMEM_LIMIT),
        name="mixer",
    )(qa, ka, ka, ka, va, va, va, qr, kr, vr, gr, gate, rf, rb, x, sink, lf_h, lb_h, rng, wba, wbr, wo)


def _ffn_kernel(final_norm, xp_ref, x_ref, xn_ref, g_ref, wi_ref, cw_ref, cb_ref, wo_ref, fg_ref,
                o_ref, gu_s):
    t = x_ref.shape[1]
    i = pl.program_id(1)
    n = pl.num_programs(1)
    g = g_ref[...]
    x = x_ref[0]
    hp = jnp.where(i > 0, _rmsnorm(xp_ref[0], g), 0.0)
    hn = jnp.where(i < n - 1, _rmsnorm(xn_ref[0], g), 0.0)
    h = _rmsnorm(x, g).astype(BF16)
    h_ext = jnp.concatenate([hp.astype(BF16), h, hn.astype(BF16)], axis=0)

    def chunk_body(c, carry):
        c0 = pl.multiple_of(c * FF_CHUNK, FF_CHUNK)
        a_ext = _dot(h_ext, wi_ref[:, pl.ds(c0, FF_CHUNK)])
        u = _dot(h, wi_ref[:, pl.ds(D_FF + c0, FF_CHUNK)])
        w = cw_ref[:, pl.ds(c0, FF_CHUNK)]
        lo = BF16_ROWS
        a = (cb_ref[:, pl.ds(c0, FF_CHUNK)]
             + a_ext[lo - 1:lo - 1 + t] * w[0:1]
             + a_ext[lo:lo + t] * w[1:2]
             + a_ext[lo + 1:lo + 1 + t] * w[2:3])
        gelu = 0.5 * a * (1.0 + lax.erf(a * (2.0 ** -0.5)))
        gu_s[:, pl.ds(c0, FF_CHUNK)] = (gelu * u).astype(BF16)
        return carry

    lax.fori_loop(0, D_FF // FF_CHUNK, chunk_body, 0, unroll=True)
    y = x + _dot(gu_s[...], wo_ref[...])
    if final_norm:
        y = _rmsnorm(y, fg_ref[...])
    o_ref[0] = y


def _ffn(x, g, wi, cw, cb, wo, fg, final_norm):
    b, s, d = x.shape
    t = TOK_TILE
    hb = t // BF16_ROWS
    nh = s // BF16_ROWS
    tok = pl.BlockSpec((1, t, d), lambda bi, i: (bi, i, 0))
    prev = pl.BlockSpec((1, BF16_ROWS, d), lambda bi, i: (bi, jnp.maximum(i * hb - 1, 0), 0))
    nxt = pl.BlockSpec((1, BF16_ROWS, d), lambda bi, i: (bi, jnp.minimum((i + 1) * hb, nh - 1), 0))
    return pl.pallas_call(
        functools.partial(_ffn_kernel, final_norm),
        grid=(b, s // t),
        in_specs=[prev, tok, nxt, _const_spec((1, d)), _const_spec((d, 2 * D_FF)),
                  _const_spec((CONV_WIDTH, D_FF)), _const_spec((1, D_FF)), _const_spec((D_FF, d)),
                  _const_spec((1, d))],
        out_specs=tok,
        out_shape=jax.ShapeDtypeStruct((b, s, d), F32),
        scratch_shapes=[pltpu.VMEM((t, D_FF), BF16)],
        compiler_params=pltpu.CompilerParams(
            dimension_semantics=("arbitrary", "arbitrary"), vmem_limit_bytes=VMEM_LIMIT),
        name="ffn",
    )(x, x, x, g, wi, cw, cb, wo, fg)


def _rope_tables(s, half, reps):
    freqs = ROPE_THETA ** (-jnp.arange(half, dtype=F32) / half)
    ang = jnp.arange(s, dtype=F32)[:, None] * freqs[None, :]
    cos = jnp.cos(ang)
    sin = jnp.sin(ang)
    cos_t = jnp.tile(jnp.concatenate([cos, cos], axis=-1), (1, reps))
    sin_t = jnp.tile(jnp.concatenate([-sin, sin], axis=-1), (1, reps))
    return cos_t, sin_t


def _layer(x, p, tabs, final_g, final_norm):
    ca, sa, cr, sr = tabs
    qa, ka, va, qr, kr, vr, gr, gate = _inproj(x, p["norm_mix_g"], p["w_in"], ca, sa, cr, sr)
    rf, rb = _states(kr, vr, p["lf_lanes"], p["lb_lanes"])
    x = _mixer(qa, ka, va, qr, kr, vr, gr, gate, rf, rb, x, p["sink"], p["lf_heads"], p["lb_heads"],
               p["ret_norm_g"], p["w_branch_attn"], p["w_branch_ret"], p["w_out"])
    return _ffn(x, p["norm_ffn_g"], p["w_ffn_in"], p["conv_w"], p["conv_b"], p["w_ffn_out"],
                final_g, final_norm)


def kernel(x_prompt, x_sample, norm_mix_g, w_in, attn_sink, ret_log_decay_f, ret_log_decay_b, ret_norm_g,
           w_branch_attn, w_branch_ret, w_out, norm_ffn_g, w_ffn_in, conv_w, conv_b, w_ffn_out, final_norm_g):
    depth = w_in.shape[0]
    layers = []
    for l in range(depth):
        lf = ret_log_decay_f[l].astype(F32)
        lb = ret_log_decay_b[l].astype(F32)
        layers.append(dict(
            norm_mix_g=norm_mix_g[l].reshape(1, D_MODEL),
            w_in=w_in[l].astype(BF16),
            sink=attn_sink[l].astype(F32),
            lf_lanes=jnp.repeat(lf, RET_HEAD_DIM).reshape(1, RET_WIDTH),
            lb_lanes=jnp.repeat(lb, RET_HEAD_DIM).reshape(1, RET_WIDTH),
            lf_heads=jnp.broadcast_to(lf[:, None], (RET_HEADS, LANES)),
            lb_heads=jnp.broadcast_to(lb[:, None], (RET_HEADS, LANES)),
            ret_norm_g=ret_norm_g[l].reshape(1, RET_WIDTH),
            w_branch_attn=w_branch_attn[l].astype(BF16),
            w_branch_ret=w_branch_ret[l].astype(BF16),
            w_out=w_out[l].astype(BF16),
            norm_ffn_g=norm_ffn_g[l].reshape(1, D_MODEL),
            w_ffn_in=w_ffn_in[l].astype(BF16),
            conv_w=conv_w[l],
            conv_b=conv_b[l].reshape(1, D_FF),
            w_ffn_out=w_ffn_out[l].astype(BF16),
        ))
    fg = final_norm_g.reshape(1, D_MODEL)
    outs = []
    for x in (x_prompt, x_sample):
        s = x.shape[1]
        tabs = _rope_tables(s, HEAD_DIM // 2, LANES // HEAD_DIM) + _rope_tables(s, RET_HEAD_DIM // 2, 1)
        for l in range(depth):
            x = _layer(x, layers[l], tabs, fg, l == depth - 1)
        outs.append(x)
    return tuple(outs)
```

```python
import functools

import jax
import jax.numpy as jnp
from jax import lax
from jax.experimental import pallas as pl
from jax.experimental.pallas import tpu as pltpu

D_MODEL = 1024
HEAD_DIM = 64
ATT_Q_HEADS = 8
ATT_KV_HEADS = 2
ATT_GROUP = ATT_Q_HEADS // ATT_KV_HEADS
ATT_Q_W = ATT_Q_HEADS * HEAD_DIM
ATT_KV_W = ATT_KV_HEADS * HEAD_DIM
WINDOW = 128
BLOCK = 128
RET_HEADS = 4
RET_HEAD_DIM = 128
RET_WIDTH = RET_HEADS * RET_HEAD_DIM
CHUNK = 128
D_FF = 2816
CONV_WIDTH = 3
ROPE_THETA = 10000.0
EPS = 1e-6
NEG_INF = -1e30
LOG2E = 1.4426950408889634
IN_WIDTH = ATT_Q_W + 2 * ATT_KV_W + 4 * RET_WIDTH + 2 * D_MODEL

LANES = 128
BF16_ROWS = 16
VMEM_LIMIT = 56 * 1024 * 1024

INPROJ_TILE = 1024
MIXER_TILE = 512
FFN_TILE = 1024
FF_CHUNK = 256
STATE_CHUNKS = 8
MERGE_COLS = 256

F32 = jnp.float32
BF16 = jnp.bfloat16

_C_QA = 0
_C_KA = _C_QA + ATT_Q_W
_C_VA = _C_KA + ATT_KV_W
_C_QR = _C_VA + ATT_KV_W
_C_KR = _C_QR + RET_WIDTH
_C_VR = _C_KR + RET_WIDTH
_C_GR = _C_VR + RET_WIDTH
_C_GATE = _C_GR + RET_WIDTH


def _rmsnorm(x, g):
    return (x * lax.rsqrt(jnp.mean(x * x, axis=-1, keepdims=True) + EPS)) * g


def _dot(a, b):
    return jnp.dot(a, b, preferred_element_type=F32)


def _dot_nt(a, b):
    return lax.dot_general(a, b, (((1,), (1,)), ((), ())), preferred_element_type=F32)


def _dot_tn(a, b):
    return lax.dot_general(a, b, (((0,), (0,)), ((), ())), preferred_element_type=F32)


def _const_spec(shape):
    nd = len(shape)
    return pl.BlockSpec(shape, lambda *_: (0,) * nd, pipeline_mode=pl.Buffered(1))


def _inproj_kernel(x_ref, g_ref, w_ref, ca_ref, sa_ref, cr_ref, sr_ref,
                   qa_ref, ka_ref, va_ref, qr_ref, kr_ref, vr_ref, gr_ref, gate_ref):
    x = x_ref[0]
    hb = _rmsnorm(x, g_ref[...]).astype(BF16)
    t = x.shape[0]
    lane = lax.broadcasted_iota(jnp.int32, (t, LANES), 1)
    first_half = (lane % HEAD_DIM) < (HEAD_DIM // 2)
    ca, sa, cr, sr = ca_ref[...], sa_ref[...], cr_ref[...], sr_ref[...]

    def mm(c0, n):
        return _dot(hb, w_ref[:, c0:c0 + n])

    def rope_a(y):
        rot = jnp.where(first_half, pltpu.roll(y, LANES - HEAD_DIM // 2, 1),
                        pltpu.roll(y, HEAD_DIM // 2, 1))
        return y * ca + rot * sa

    def rope_r(y):
        return y * cr + pltpu.roll(y, RET_HEAD_DIM // 2, 1) * sr

    def slab(y, s):
        return y[:, s * LANES:(s + 1) * LANES]

    low = lane < HEAD_DIM
    y = mm(_C_QA, ATT_Q_W)
    for s in range(ATT_Q_W // LANES):
        r = rope_a(slab(y, s)) * (HEAD_DIM ** -0.5 * LOG2E)
        r_sw = pltpu.roll(r, HEAD_DIM, 1)
        kv_low = (2 * s) // ATT_GROUP == 0
        h0 = jnp.where(low, r, 0.0) if kv_low else jnp.where(low, 0.0, r_sw)
        h1 = jnp.where(low, r_sw, 0.0) if kv_low else jnp.where(low, 0.0, r)
        qa_ref[0, :, (2 * s) * LANES:(2 * s + 1) * LANES] = h0.astype(BF16)
        qa_ref[0, :, (2 * s + 1) * LANES:(2 * s + 2) * LANES] = h1.astype(BF16)
    y = mm(_C_KA, 2 * ATT_KV_W)
    ka_ref[0] = rope_a(slab(y, 0)).astype(BF16)
    va_ref[0] = slab(y, 1).astype(BF16)
    y = mm(_C_QR, RET_WIDTH)
    for s in range(RET_HEADS):
        qr_ref[0, :, s * LANES:(s + 1) * LANES] = rope_r(slab(y, s)).astype(BF16)
    y = mm(_C_KR, RET_WIDTH)
    for s in range(RET_HEADS):
        kr_ref[0, :, s * LANES:(s + 1) * LANES] = (rope_r(slab(y, s)) * (RET_HEAD_DIM ** -0.5)).astype(BF16)
    vr_ref[0] = mm(_C_VR, RET_WIDTH).astype(BF16)
    gr_ref[0] = mm(_C_GR, RET_WIDTH).astype(BF16)
    for c in range(4):
        n = 2 * D_MODEL // 4
        gate_ref[0, :, c * n:(c + 1) * n] = mm(_C_GATE + c * n, n).astype(BF16)


def _inproj(x, g, w, ca, sa, cr, sr):
    b, s, d = x.shape
    t = min(INPROJ_TILE, s)
    tok = lambda width: pl.BlockSpec((1, t, width), lambda bi, i: (bi, i, 0))
    tab = pl.BlockSpec((t, LANES), lambda bi, i: (i, 0))
    widths = [ATT_Q_HEADS * LANES, ATT_KV_W, ATT_KV_W, RET_WIDTH, RET_WIDTH, RET_WIDTH, RET_WIDTH, 2 * D_MODEL]
    return pl.pallas_call(
        _inproj_kernel,
        grid=(b, s // t),
        in_specs=[tok(d), _const_spec((1, d)), _const_spec((d, IN_WIDTH)), tab, tab, tab, tab],
        out_specs=[tok(wd) for wd in widths],
        out_shape=[jax.ShapeDtypeStruct((b, s, wd), BF16) for wd in widths],
        compiler_params=pltpu.CompilerParams(
            dimension_semantics=("arbitrary", "arbitrary"), vmem_limit_bytes=VMEM_LIMIT),
        name="inproj",
    )(x, g, w, ca, sa, cr, sr)


def _state_kernel(k_ref, v_ref, lb_ref, rb_ref, st_ref):
    @pl.when(pl.program_id(1) == 0)
    def _():
        st_ref[...] = jnp.zeros_like(st_ref)

    row = lax.broadcasted_iota(jnp.int32, (CHUNK, RET_WIDTH), 0).astype(F32)
    lb = lb_ref[...]
    zeta = jnp.exp(lb * row)
    dec = jnp.exp(lb * float(CHUNK))
    for c in reversed(range(k_ref.shape[1] // CHUNK)):
        rows = slice(c * CHUNK, (c + 1) * CHUNK)
        for h in range(RET_HEADS):
            cols = slice(h * RET_HEAD_DIM, (h + 1) * RET_HEAD_DIM)
            st = st_ref[h]
            rb_ref[0, c, h] = st.astype(BF16)
            zv = (v_ref[0, rows, cols].astype(F32) * zeta[:, cols]).astype(BF16)
            st_ref[h] = st * dec[:, cols] + _dot_tn(k_ref[0, rows, cols], zv)


def _states(kr, vr, lb_l):
    b, s, _ = kr.shape
    nc = s // CHUNK
    per_step = min(STATE_CHUNKS, nc)
    ns = nc // per_step
    t = per_step * CHUNK
    bwd = pl.BlockSpec((1, t, RET_WIDTH), lambda bi, i: (bi, ns - 1 - i, 0))
    out_b = pl.BlockSpec((1, per_step, RET_HEADS, RET_HEAD_DIM, RET_HEAD_DIM),
                         lambda bi, i: (bi, ns - 1 - i, 0, 0, 0))
    return pl.pallas_call(
        _state_kernel,
        grid=(b, ns),
        in_specs=[bwd, bwd, _const_spec((1, RET_WIDTH))],
        out_specs=out_b,
        out_shape=jax.ShapeDtypeStruct((b, nc, RET_HEADS, RET_HEAD_DIM, RET_HEAD_DIM), BF16),
        scratch_shapes=[pltpu.VMEM((RET_HEADS, RET_HEAD_DIM, RET_HEAD_DIM), F32)],
        compiler_params=pltpu.CompilerParams(
            dimension_semantics=("arbitrary", "arbitrary"), vmem_limit_bytes=VMEM_LIMIT),
        name="ret_states",
    )(kr, vr, lb_l)


def _mixer_kernel(qa_ref, kp_ref, kc_ref, kn_ref, vp_ref, vc_ref, vn_ref,
                  qr_ref, kr_ref, vr_ref, gr_ref, gate_ref, rb_ref, x_ref,
                  sink_ref, lf_ref, lb_ref, rng_ref, wba_ref, wbr_ref, wo_ref,
                  o_ref,
                  k_s, vt_s, dm_s, xi_s, zeta_s, st_s, cap_s, s_buf, p_buf, att_s, ret_s, mrg_s):
    t = x_ref.shape[1]
    nblk = t // BLOCK
    tile = pl.program_id(1)

    k_s[0:BLOCK] = kp_ref[0]
    k_s[BLOCK:BLOCK + t] = kc_ref[0]
    k_s[BLOCK + t:] = kn_ref[0]
    for i in range(nblk + 2):
        src = vp_ref if i == 0 else (vn_ref if i == nblk + 1 else vc_ref)
        r0 = 0 if i in (0, nblk + 1) else (i - 1) * BLOCK
        vt_s[:, i * BLOCK:(i + 1) * BLOCK] = src[0, r0:r0 + BLOCK, :].astype(F32).T.astype(BF16)

    ri = lax.broadcasted_iota(jnp.int32, (CHUNK, CHUNK), 0).astype(F32)
    ci = lax.broadcasted_iota(jnp.int32, (CHUNK, CHUNK), 1).astype(F32)
    diff = ri - ci
    for h in range(RET_HEADS):
        lf = lf_ref[h:h + 1, :]
        lb = lb_ref[h:h + 1, :]
        dm_s[h] = (jnp.where(diff >= 0, jnp.exp(lf * jnp.maximum(diff, 0.0)), 0.0)
                   + jnp.where(diff < 0, jnp.exp(lb * jnp.maximum(-diff, 0.0)), 0.0))
        xi_s[0, h] = jnp.exp(lf * (ri + 1.0))
        xi_s[1, h] = jnp.exp(lb * (float(CHUNK) - ri))
        zeta_s[h] = jnp.exp(lf * (float(CHUNK) - 1.0 - ri))

    @pl.when(tile == 0)
    def _():
        st_s[...] = jnp.zeros_like(st_s)

    kj = lax.broadcasted_iota(jnp.int32, (3 * BLOCK, BLOCK), 0)
    qi = lax.broadcasted_iota(jnp.int32, (3 * BLOCK, BLOCK), 1)
    band = jnp.abs(kj - BLOCK - qi) <= WINDOW
    first = tile == 0
    last = tile == pl.num_programs(1) - 1
    cap_s[0] = jnp.where(band, jnp.inf, NEG_INF)
    cap_s[1] = jnp.where(band & ((kj >= BLOCK) | jnp.logical_not(first)), jnp.inf, NEG_INF)
    cap_s[2] = jnp.where(band & ((kj < 2 * BLOCK) | jnp.logical_not(last)), jnp.inf, NEG_INF)
    head1 = lax.broadcasted_iota(jnp.int32, (1, 2 * BLOCK), 1) >= BLOCK
    low_o = lax.broadcasted_iota(jnp.int32, (BLOCK, LANES), 1) < HEAD_DIM

    n_slab = ATT_Q_HEADS // 2
    units = [(j, s) for j in range(nblk) for s in range(n_slab)]

    def att_scores(u):
        j, s = units[u]
        rows = slice(j * BLOCK, (j + 1) * BLOCK)
        q2 = jnp.concatenate([qa_ref[0, rows, (2 * s) * LANES:(2 * s + 1) * LANES],
                              qa_ref[0, rows, (2 * s + 1) * LANES:(2 * s + 2) * LANES]], axis=0)
        s_buf[u % 2] = _dot_nt(k_s[j * BLOCK:(j + 3) * BLOCK, :], q2)

    def att_softmax(u):
        j, s = units[u]
        cap = cap_s[1 if j == 0 else (2 if j == nblk - 1 else 0)]
        sc = s_buf[u % 2]
        sc = jnp.concatenate([jnp.minimum(sc[:, :BLOCK], cap), jnp.minimum(sc[:, BLOCK:], cap)], axis=1)
        sink = jnp.where(head1, sink_ref[2 * s + 1], sink_ref[2 * s]) * LOG2E
        m = jnp.maximum(jnp.max(sc, axis=0, keepdims=True), sink)
        p = jnp.exp2(sc - m)
        den = jnp.sum(p, axis=0, keepdims=True) + jnp.exp2(sink - m)
        p_buf[u % 2] = p.astype(BF16)
        return 1.0 / den

    def att_values(u, inv):
        j, s = units[u]
        rows = slice(j * BLOCK, (j + 1) * BLOCK)
        ot = _dot(vt_s[:, j * BLOCK:(j + 3) * BLOCK], p_buf[u % 2]) * inv
        o0 = ot[:, :BLOCK].T
        o1 = ot[:, BLOCK:].T
        if (2 * s) // ATT_GROUP == 0:
            o = jnp.where(low_o, o0, pltpu.roll(o1, HEAD_DIM, 1))
        else:
            o = jnp.where(low_o, pltpu.roll(o0, HEAD_DIM, 1), o1)
        att_s[rows, s * LANES:(s + 1) * LANES] = o.astype(BF16)

    def retention(j, h):
        rows = slice(j * BLOCK, (j + 1) * BLOCK)
        cols = slice(h * LANES, (h + 1) * LANES)
        q = qr_ref[0, rows, cols]
        k = kr_ref[0, rows, cols]
        v = vr_ref[0, rows, cols]
        inner = (_dot_nt(q, k) * dm_s[h]).astype(BF16)
        o = _dot(inner, v)
        st = st_s[h]
        cross = _dot(q, jnp.concatenate([st.astype(BF16), rb_ref[0, j, h]], axis=1))
        zv = (v.astype(F32) * zeta_s[h]).astype(BF16)
        st_s[h] = st * jnp.exp(lf_ref[h:h + 1, :] * float(CHUNK)) + _dot_tn(k, zv)
        o = o + cross[:, :LANES] * xi_s[0, h] + cross[:, LANES:] * xi_s[1, h]
        o = o * lax.rsqrt(jnp.mean(o * o, axis=-1, keepdims=True) + EPS)
        g = gr_ref[0, rows, cols].astype(F32)
        o = (o * rng_ref[:, cols]) * (g * jax.nn.sigmoid(g))
        ret_s[rows, cols] = o.astype(BF16)

    ret_units = [(j, h) for j in range(nblk) for h in range(RET_HEADS)]
    n_units = len(units)
    inv = {}
    for k in range(n_units + 2):
        if k < n_units:
            att_scores(k)
        if k >= 2:
            att_values(k - 2, inv.pop(k - 2))
        if k < len(ret_units):
            retention(*ret_units[k])
        if 1 <= k <= n_units:
            inv[k - 1] = att_softmax(k - 1)

    for c in range(D_MODEL // MERGE_COLS):
        cols = slice(c * MERGE_COLS, (c + 1) * MERGE_COLS)
        ba = _dot(att_s[...], wba_ref[:, cols])
        br = _dot(ret_s[...], wbr_ref[:, cols])
        g0 = jax.nn.sigmoid(gate_ref[0, :, c * MERGE_COLS:(c + 1) * MERGE_COLS].astype(F32))
        g1 = jax.nn.sigmoid(gate_ref[0, :, D_MODEL + c * MERGE_COLS:D_MODEL + (c + 1) * MERGE_COLS].astype(F32))
        mrg_s[:, cols] = (g0 * ba + g1 * br).astype(BF16)
    o_ref[0] = x_ref[0] + _dot(mrg_s[...], wo_ref[...])


def _mixer(qa, ka, va, qr, kr, vr, gr, gate, rb, x, sink, lf_h, lb_h, rng, wba, wbr, wo):
    b, s, d = x.shape
    t = MIXER_TILE
    nblk = t // BLOCK
    nb_total = s // BLOCK
    tok = lambda width: pl.BlockSpec((1, t, width), lambda bi, i: (bi, i, 0))
    prev = pl.BlockSpec((1, BLOCK, ATT_KV_W), lambda bi, i: (bi, jnp.maximum(i * nblk - 1, 0), 0))
    nxt = pl.BlockSpec((1, BLOCK, ATT_KV_W), lambda bi, i: (bi, jnp.minimum((i + 1) * nblk, nb_total - 1), 0))
    st = pl.BlockSpec((1, nblk, RET_HEADS, RET_HEAD_DIM, RET_HEAD_DIM), lambda bi, i: (bi, i, 0, 0, 0))
    smem = pl.BlockSpec(memory_space=pltpu.SMEM)
    return pl.pallas_call(
        _mixer_kernel,
        grid=(b, s // t),
        in_specs=[tok(ATT_Q_HEADS * LANES), prev, tok(ATT_KV_W), nxt, prev, tok(ATT_KV_W), nxt,
                  tok(RET_WIDTH), tok(RET_WIDTH), tok(RET_WIDTH), tok(RET_WIDTH), tok(2 * D_MODEL),
                  st, tok(d),
                  smem, _const_spec((RET_HEADS, LANES)), _const_spec((RET_HEADS, LANES)),
                  _const_spec((1, RET_WIDTH)),
                  _const_spec((ATT_Q_W, d)), _const_spec((RET_WIDTH, d)), _const_spec((d, d))],
        out_specs=tok(d),
        out_shape=jax.ShapeDtypeStruct((b, s, d), F32),
        scratch_shapes=[
            pltpu.VMEM((t + 2 * BLOCK, ATT_KV_W), BF16),
            pltpu.VMEM((ATT_KV_W, t + 2 * BLOCK), BF16),
            pltpu.VMEM((RET_HEADS, CHUNK, CHUNK), F32),
            pltpu.VMEM((2, RET_HEADS, CHUNK, CHUNK), F32),
            pltpu.VMEM((RET_HEADS, CHUNK, CHUNK), F32),
            pltpu.VMEM((RET_HEADS, RET_HEAD_DIM, RET_HEAD_DIM), F32),
            pltpu.VMEM((3, 3 * BLOCK, BLOCK), F32),
            pltpu.VMEM((2, 3 * BLOCK, 2 * BLOCK), F32),
            pltpu.VMEM((2, 3 * BLOCK, 2 * BLOCK), BF16),
            pltpu.VMEM((t, ATT_Q_W), BF16),
            pltpu.VMEM((t, RET_WIDTH), BF16),
            pltpu.VMEM((t, d), BF16),
        ],
        compiler_params=pltpu.CompilerParams(
            dimension_semantics=("arbitrary", "arbitrary"), vmem_limit_bytes=VMEM_LIMIT),
        name="mixer",
    )(qa, ka, ka, ka, va, va, va, qr, kr, vr, gr, gate, rb, x, sink, lf_h, lb_h, rng, wba, wbr, wo)


def _ffn_kernel(final_norm, xp_ref, x_ref, xn_ref, g_ref, wi_ref, cw_ref, cb_ref, wo_ref, fg_ref,
                o_ref, gu_s):
    t = x_ref.shape[1]
    i = pl.program_id(1)
    n = pl.num_programs(1)
    g = g_ref[...]
    x = x_ref[0]
    hp = jnp.where(i > 0, _rmsnorm(xp_ref[0], g), 0.0)
    hn = jnp.where(i < n - 1, _rmsnorm(xn_ref[0], g), 0.0)
    h = _rmsnorm(x, g).astype(BF16)
    h_ext = jnp.concatenate([hp.astype(BF16), h, hn.astype(BF16)], axis=0)

    def chunk_body(c, carry):
        c0 = pl.multiple_of(c * FF_CHUNK, FF_CHUNK)
        a_ext = _dot(h_ext, wi_ref[:, pl.ds(c0, FF_CHUNK)])
        u = _dot(h, wi_ref[:, pl.ds(D_FF + c0, FF_CHUNK)])
        w = cw_ref[:, pl.ds(c0, FF_CHUNK)]
        lo = BF16_ROWS
        a = (cb_ref[:, pl.ds(c0, FF_CHUNK)]
             + a_ext[lo - 1:lo - 1 + t] * w[0:1]
             + a_ext[lo:lo + t] * w[1:2]
             + a_ext[lo + 1:lo + 1 + t] * w[2:3])
        gelu = 0.5 * a * (1.0 + lax.erf(a * (2.0 ** -0.5)))
        gu_s[:, pl.ds(c0, FF_CHUNK)] = (gelu * u).astype(BF16)
        return carry

    lax.fori_loop(0, D_FF // FF_CHUNK, chunk_body, 0, unroll=True)
    y = x + _dot(gu_s[...], wo_ref[...])
    if final_norm:
        y = _rmsnorm(y, fg_ref[...])
    o_ref[0] = y


def _ffn(x, g, wi, cw, cb, wo, fg, final_norm):
    b, s, d = x.shape
    t = min(FFN_TILE, s)
    hb = t // BF16_ROWS
    nh = s // BF16_ROWS
    tok = pl.BlockSpec((1, t, d), lambda bi, i: (bi, i, 0))
    prev = pl.BlockSpec((1, BF16_ROWS, d), lambda bi, i: (bi, jnp.maximum(i * hb - 1, 0), 0))
    nxt = pl.BlockSpec((1, BF16_ROWS, d), lambda bi, i: (bi, jnp.minimum((i + 1) * hb, nh - 1), 0))
    return pl.pallas_call(
        functools.partial(_ffn_kernel, final_norm),
        grid=(b, s // t),
        in_specs=[prev, tok, nxt, _const_spec((1, d)), _const_spec((d, 2 * D_FF)),
                  _const_spec((CONV_WIDTH, D_FF)), _const_spec((1, D_FF)), _const_spec((D_FF, d)),
                  _const_spec((1, d))],
        out_specs=tok,
        out_shape=jax.ShapeDtypeStruct((b, s, d), F32),
        scratch_shapes=[pltpu.VMEM((t, D_FF), BF16)],
        compiler_params=pltpu.CompilerParams(
            dimension_semantics=("arbitrary", "arbitrary"), vmem_limit_bytes=VMEM_LIMIT),
        name="ffn",
    )(x, x, x, g, wi, cw, cb, wo, fg)


def _rope_tables(s, half, reps):
    freqs = ROPE_THETA ** (-jnp.arange(half, dtype=F32) / half)
    ang = jnp.arange(s, dtype=F32)[:, None] * freqs[None, :]
    cos = jnp.cos(ang)
    sin = jnp.sin(ang)
    cos_t = jnp.tile(jnp.concatenate([cos, cos], axis=-1), (1, reps))
    sin_t = jnp.tile(jnp.concatenate([-sin, sin], axis=-1), (1, reps))
    return cos_t, sin_t


def _layer(x, p, tabs, final_g, final_norm):
    ca, sa, cr, sr = tabs
    qa, ka, va, qr, kr, vr, gr, gate = _inproj(x, p["norm_mix_g"], p["w_in"], ca, sa, cr, sr)
    rb = _states(kr, vr, p["lb_lanes"])
    x = _mixer(qa, ka, va, qr, kr, vr, gr, gate, rb, x, p["sink"], p["lf_heads"], p["lb_heads"],
               p["ret_norm_g"], p["w_branch_attn"], p["w_branch_ret"], p["w_out"])
    return _ffn(x, p["norm_ffn_g"], p["w_ffn_in"], p["conv_w"], p["conv_b"], p["w_ffn_out"],
                final_g, final_norm)


def kernel(x_prompt, x_sample, norm_mix_g, w_in, attn_sink, ret_log_decay_f, ret_log_decay_b, ret_norm_g,
           w_branch_attn, w_branch_ret, w_out, norm_ffn_g, w_ffn_in, conv_w, conv_b, w_ffn_out, final_norm_g):
    depth = w_in.shape[0]
    layers = []
    for l in range(depth):
        lf = ret_log_decay_f[l].astype(F32)
        lb = ret_log_decay_b[l].astype(F32)
        layers.append(dict(
            norm_mix_g=norm_mix_g[l].reshape(1, D_MODEL),
            w_in=w_in[l].astype(BF16),
            sink=attn_sink[l].astype(F32),
            lb_lanes=jnp.repeat(lb, RET_HEAD_DIM).reshape(1, RET_WIDTH),
            lf_heads=jnp.broadcast_to(lf[:, None], (RET_HEADS, LANES)),
            lb_heads=jnp.broadcast_to(lb[:, None], (RET_HEADS, LANES)),
            ret_norm_g=ret_norm_g[l].reshape(1, RET_WIDTH),
            w_branch_attn=w_branch_attn[l].astype(BF16),
            w_branch_ret=w_branch_ret[l].astype(BF16),
            w_out=w_out[l].astype(BF16),
            norm_ffn_g=norm_ffn_g[l].reshape(1, D_MODEL),
            w_ffn_in=w_ffn_in[l].astype(BF16),
            conv_w=conv_w[l],
            conv_b=conv_b[l].reshape(1, D_FF),
            w_ffn_out=w_ffn_out[l].astype(BF16),
        ))
    fg = final_norm_g.reshape(1, D_MODEL)
    outs = []
    for x in (x_prompt, x_sample):
        s = x.shape[1]
        tabs = _rope_tables(s, HEAD_DIM // 2, LANES // HEAD_DIM) + _rope_tables(s, RET_HEAD_DIM // 2, 1)
        for l in range(depth):
            x = _layer(x, layers[l], tabs, fg, l == depth - 1)
        outs.append(x)
    return tuple(outs)
```

```python
import functools

import jax
import jax.numpy as jnp
from jax import lax
from jax.experimental import pallas as pl
from jax.experimental.pallas import tpu as pltpu

D_MODEL = 1024
HEAD_DIM = 64
ATT_Q_HEADS = 8
ATT_KV_HEADS = 2
ATT_GROUP = ATT_Q_HEADS // ATT_KV_HEADS
ATT_Q_W = ATT_Q_HEADS * HEAD_DIM
ATT_KV_W = ATT_KV_HEADS * HEAD_DIM
WINDOW = 128
BLOCK = 128
RET_HEADS = 4
RET_HEAD_DIM = 128
RET_WIDTH = RET_HEADS * RET_HEAD_DIM
CHUNK = 128
D_FF = 2816
CONV_WIDTH = 3
ROPE_THETA = 10000.0
EPS = 1e-6
NEG_INF = -1e30
LOG2E = 1.4426950408889634
IN_WIDTH = ATT_Q_W + 2 * ATT_KV_W + 4 * RET_WIDTH + 2 * D_MODEL

LANES = 128
BF16_ROWS = 16
VMEM_LIMIT = 56 * 1024 * 1024

INPROJ_TILE = 1024
MIXER_TILE = 512
FFN_TILE = 1024
FF_CHUNK = 256
STATE_CHUNKS = 8
MERGE_COLS = 256

F32 = jnp.float32
BF16 = jnp.bfloat16

_C_QA = 0
_C_KA = _C_QA + ATT_Q_W
_C_VA = _C_KA + ATT_KV_W
_C_QR = _C_VA + ATT_KV_W
_C_KR = _C_QR + RET_WIDTH
_C_VR = _C_KR + RET_WIDTH
_C_GR = _C_VR + RET_WIDTH
_C_GATE = _C_GR + RET_WIDTH


def _rmsnorm(x, g):
    return (x * lax.rsqrt(jnp.mean(x * x, axis=-1, keepdims=True) + EPS)) * g


def _dot(a, b):
    return jnp.dot(a, b, preferred_element_type=F32)


def _dot_nt(a, b):
    return lax.dot_general(a, b, (((1,), (1,)), ((), ())), preferred_element_type=F32)


def _dot_tn(a, b):
    return lax.dot_general(a, b, (((0,), (0,)), ((), ())), preferred_element_type=F32)


def _const_spec(shape):
    nd = len(shape)
    return pl.BlockSpec(shape, lambda *_: (0,) * nd, pipeline_mode=pl.Buffered(1))


def _inproj_kernel(x_ref, g_ref, w_ref, ca_ref, sa_ref, cr_ref, sr_ref,
                   qa_ref, ka_ref, va_ref, qr_ref, kr_ref, vr_ref, gr_ref, gate_ref):
    x = x_ref[0]
    hb = _rmsnorm(x, g_ref[...]).astype(BF16)
    t = x.shape[0]
    lane = lax.broadcasted_iota(jnp.int32, (t, LANES), 1)
    first_half = (lane % HEAD_DIM) < (HEAD_DIM // 2)
    ca, sa, cr, sr = ca_ref[...], sa_ref[...], cr_ref[...], sr_ref[...]

    def mm(c0, n):
        return _dot(hb, w_ref[:, c0:c0 + n])

    def rope_a(y):
        rot = jnp.where(first_half, pltpu.roll(y, LANES - HEAD_DIM // 2, 1),
                        pltpu.roll(y, HEAD_DIM // 2, 1))
        return y * ca + rot * sa

    def rope_r(y):
        return y * cr + pltpu.roll(y, RET_HEAD_DIM // 2, 1) * sr

    def slab(y, s):
        return y[:, s * LANES:(s + 1) * LANES]

    low = lane < HEAD_DIM
    y = mm(_C_QA, ATT_Q_W)
    for s in range(ATT_Q_W // LANES):
        r = rope_a(slab(y, s)) * (HEAD_DIM ** -0.5 * LOG2E)
        r_sw = pltpu.roll(r, HEAD_DIM, 1)
        kv_low = (2 * s) // ATT_GROUP == 0
        h0 = jnp.where(low, r, 0.0) if kv_low else jnp.where(low, 0.0, r_sw)
        h1 = jnp.where(low, r_sw, 0.0) if kv_low else jnp.where(low, 0.0, r)
        qa_ref[0, :, (2 * s) * LANES:(2 * s + 1) * LANES] = h0.astype(BF16)
        qa_ref[0, :, (2 * s + 1) * LANES:(2 * s + 2) * LANES] = h1.astype(BF16)
    y = mm(_C_KA, 2 * ATT_KV_W)
    ka_ref[0] = rope_a(slab(y, 0)).astype(BF16)
    va_ref[0] = slab(y, 1).astype(BF16)
    y = mm(_C_QR, RET_WIDTH)
    for s in range(RET_HEADS):
        qr_ref[0, :, s * LANES:(s + 1) * LANES] = rope_r(slab(y, s)).astype(BF16)
    y = mm(_C_KR, RET_WIDTH)
    for s in range(RET_HEADS):
        kr_ref[0, :, s * LANES:(s + 1) * LANES] = (rope_r(slab(y, s)) * (RET_HEAD_DIM ** -0.5)).astype(BF16)
    vr_ref[0] = mm(_C_VR, RET_WIDTH).astype(BF16)
    gr_ref[0] = mm(_C_GR, RET_WIDTH).astype(BF16)
    for c in range(4):
        n = 2 * D_MODEL // 4
        gate_ref[0, :, c * n:(c + 1) * n] = mm(_C_GATE + c * n, n).astype(BF16)


def _inproj(x, g, w, ca, sa, cr, sr):
    b, s, d = x.shape
    t = min(INPROJ_TILE, s)
    tok = lambda width: pl.BlockSpec((1, t, width), lambda bi, i: (bi, i, 0))
    tab = pl.BlockSpec((t, LANES), lambda bi, i: (i, 0))
    widths = [ATT_Q_HEADS * LANES, ATT_KV_W, ATT_KV_W, RET_WIDTH, RET_WIDTH, RET_WIDTH, RET_WIDTH, 2 * D_MODEL]
    return pl.pallas_call(
        _inproj_kernel,
        grid=(b, s // t),
        in_specs=[tok(d), _const_spec((1, d)), _const_spec((d, IN_WIDTH)), tab, tab, tab, tab],
        out_specs=[tok(wd) for wd in widths],
        out_shape=[jax.ShapeDtypeStruct((b, s, wd), BF16) for wd in widths],
        compiler_params=pltpu.CompilerParams(
            dimension_semantics=("arbitrary", "arbitrary"), vmem_limit_bytes=VMEM_LIMIT),
        name="inproj",
    )(x, g, w, ca, sa, cr, sr)


def _state_kernel(k_ref, v_ref, lb_ref, rb_ref, st_ref):
    @pl.when(pl.program_id(1) == 0)
    def _():
        st_ref[...] = jnp.zeros_like(st_ref)

    row = lax.broadcasted_iota(jnp.int32, (CHUNK, RET_WIDTH), 0).astype(F32)
    lb = lb_ref[...]
    zeta = jnp.exp(lb * row)
    dec = jnp.exp(lb * float(CHUNK))
    for c in reversed(range(k_ref.shape[1] // CHUNK)):
        rows = slice(c * CHUNK, (c + 1) * CHUNK)
        for h in range(RET_HEADS):
            cols = slice(h * RET_HEAD_DIM, (h + 1) * RET_HEAD_DIM)
            st = st_ref[h]
            rb_ref[0, c, h] = st.astype(BF16)
            zv = (v_ref[0, rows, cols].astype(F32) * zeta[:, cols]).astype(BF16)
            st_ref[h] = st * dec[:, cols] + _dot_tn(k_ref[0, rows, cols], zv)


def _states(kr, vr, lb_l):
    b, s, _ = kr.shape
    nc = s // CHUNK
    per_step = min(STATE_CHUNKS, nc)
    ns = nc // per_step
    t = per_step * CHUNK
    bwd = pl.BlockSpec((1, t, RET_WIDTH), lambda bi, i: (bi, ns - 1 - i, 0))
    out_b = pl.BlockSpec((1, per_step, RET_HEADS, RET_HEAD_DIM, RET_HEAD_DIM),
                         lambda bi, i: (bi, ns - 1 - i, 0, 0, 0))
    return pl.pallas_call(
        _state_kernel,
        grid=(b, ns),
        in_specs=[bwd, bwd, _const_spec((1, RET_WIDTH))],
        out_specs=out_b,
        out_shape=jax.ShapeDtypeStruct((b, nc, RET_HEADS, RET_HEAD_DIM, RET_HEAD_DIM), BF16),
        scratch_shapes=[pltpu.VMEM((RET_HEADS, RET_HEAD_DIM, RET_HEAD_DIM), F32)],
        compiler_params=pltpu.CompilerParams(
            dimension_semantics=("arbitrary", "arbitrary"), vmem_limit_bytes=VMEM_LIMIT),
        name="ret_states",
    )(kr, vr, lb_l)


def _mixer_kernel(qa_ref, kp_ref, kc_ref, kn_ref, vp_ref, vc_ref, vn_ref,
                  qr_ref, kr_ref, vr_ref, gr_ref, gate_ref, rb_ref, x_ref,
                  sink_ref, lf_ref, lb_ref, rng_ref, wba_ref, wbr_ref, wo_ref,
                  o_ref,
                  k_s, vt_s, dm_s, xi_s, zeta_s, st_s, cap_s, s_buf, p_buf, ot_buf, in_buf, cr_buf,
                  att_s, ret_s, mrg_s):
    t = x_ref.shape[1]
    nblk = t // BLOCK
    tile = pl.program_id(1)

    k_s[0:BLOCK] = kp_ref[0]
    k_s[BLOCK:BLOCK + t] = kc_ref[0]
    k_s[BLOCK + t:] = kn_ref[0]
    for i in range(nblk + 2):
        src = vp_ref if i == 0 else (vn_ref if i == nblk + 1 else vc_ref)
        r0 = 0 if i in (0, nblk + 1) else (i - 1) * BLOCK
        vt_s[:, i * BLOCK:(i + 1) * BLOCK] = src[0, r0:r0 + BLOCK, :].astype(F32).T.astype(BF16)

    ri = lax.broadcasted_iota(jnp.int32, (CHUNK, CHUNK), 0).astype(F32)
    ci = lax.broadcasted_iota(jnp.int32, (CHUNK, CHUNK), 1).astype(F32)
    diff = ri - ci
    for h in range(RET_HEADS):
        lf = lf_ref[h:h + 1, :]
        lb = lb_ref[h:h + 1, :]
        dm_s[h] = (jnp.where(diff >= 0, jnp.exp(lf * jnp.maximum(diff, 0.0)), 0.0)
                   + jnp.where(diff < 0, jnp.exp(lb * jnp.maximum(-diff, 0.0)), 0.0))
        xi_s[0, h] = jnp.exp(lf * (ri + 1.0))
        xi_s[1, h] = jnp.exp(lb * (float(CHUNK) - ri))
        zeta_s[h] = jnp.exp(lf * (float(CHUNK) - 1.0 - ri))

    @pl.when(tile == 0)
    def _():
        st_s[...] = jnp.zeros_like(st_s)

    kj = lax.broadcasted_iota(jnp.int32, (3 * BLOCK, BLOCK), 0)
    qi = lax.broadcasted_iota(jnp.int32, (3 * BLOCK, BLOCK), 1)
    band = jnp.abs(kj - BLOCK - qi) <= WINDOW
    first = tile == 0
    last = tile == pl.num_programs(1) - 1
    cap_s[0] = jnp.where(band, jnp.inf, NEG_INF)
    cap_s[1] = jnp.where(band & ((kj >= BLOCK) | jnp.logical_not(first)), jnp.inf, NEG_INF)
    cap_s[2] = jnp.where(band & ((kj < 2 * BLOCK) | jnp.logical_not(last)), jnp.inf, NEG_INF)
    head1 = lax.broadcasted_iota(jnp.int32, (1, 2 * BLOCK), 1) >= BLOCK
    low_o = lax.broadcasted_iota(jnp.int32, (BLOCK, LANES), 1) < HEAD_DIM

    n_slab = ATT_Q_HEADS // 2
    units = [(j, s) for j in range(nblk) for s in range(n_slab)]

    def att_scores(u):
        j, s = units[u]
        rows = slice(j * BLOCK, (j + 1) * BLOCK)
        q2 = jnp.concatenate([qa_ref[0, rows, (2 * s) * LANES:(2 * s + 1) * LANES],
                              qa_ref[0, rows, (2 * s + 1) * LANES:(2 * s + 2) * LANES]], axis=0)
        s_buf[u % 2] = _dot_nt(k_s[j * BLOCK:(j + 3) * BLOCK, :], q2)

    def att_softmax(u):
        j, s = units[u]
        cap = cap_s[1 if j == 0 else (2 if j == nblk - 1 else 0)]
        sc = s_buf[u % 2]
        sc = jnp.concatenate([jnp.minimum(sc[:, :BLOCK], cap), jnp.minimum(sc[:, BLOCK:], cap)], axis=1)
        sink = jnp.where(head1, sink_ref[2 * s + 1], sink_ref[2 * s]) * LOG2E
        m = jnp.maximum(jnp.max(sc, axis=0, keepdims=True), sink)
        p = jnp.exp2(sc - m)
        den = jnp.sum(p, axis=0, keepdims=True) + jnp.exp2(sink - m)
        p_buf[u % 2] = p.astype(BF16)
        return 1.0 / den

    def att_values(u, inv):
        j, s = units[u]
        ot_buf[u % 2] = _dot(vt_s[:, j * BLOCK:(j + 3) * BLOCK], p_buf[u % 2]) * inv

    def att_store(u):
        j, s = units[u]
        rows = slice(j * BLOCK, (j + 1) * BLOCK)
        ot = ot_buf[u % 2]
        o0 = ot[:, :BLOCK].T
        o1 = ot[:, BLOCK:].T
        if (2 * s) // ATT_GROUP == 0:
            o = jnp.where(low_o, o0, pltpu.roll(o1, HEAD_DIM, 1))
        else:
            o = jnp.where(low_o, pltpu.roll(o0, HEAD_DIM, 1), o1)
        att_s[rows, s * LANES:(s + 1) * LANES] = o.astype(BF16)

    ret_units = [(j, h) for j in range(nblk) for h in range(RET_HEADS)]

    def ret_products(u):
        j, h = ret_units[u]
        rows = slice(j * BLOCK, (j + 1) * BLOCK)
        cols = slice(h * LANES, (h + 1) * LANES)
        q = qr_ref[0, rows, cols]
        k = kr_ref[0, rows, cols]
        in_buf[u % 2] = (_dot_nt(q, k) * dm_s[h]).astype(BF16)
        st = st_s[h]
        cr_buf[u % 2] = _dot(q, jnp.concatenate([st.astype(BF16), rb_ref[0, j, h]], axis=1))
        zv = (vr_ref[0, rows, cols].astype(F32) * zeta_s[h]).astype(BF16)
        st_s[h] = st * jnp.exp(lf_ref[h:h + 1, :] * float(CHUNK)) + _dot_tn(k, zv)

    def ret_finish(u):
        j, h = ret_units[u]
        rows = slice(j * BLOCK, (j + 1) * BLOCK)
        cols = slice(h * LANES, (h + 1) * LANES)
        cross = cr_buf[u % 2]
        o = _dot(in_buf[u % 2], vr_ref[0, rows, cols])
        o = o + cross[:, :LANES] * xi_s[0, h] + cross[:, LANES:] * xi_s[1, h]
        o = o * lax.rsqrt(jnp.mean(o * o, axis=-1, keepdims=True) + EPS)
        g = gr_ref[0, rows, cols].astype(F32)
        o = (o * rng_ref[:, cols]) * (g * jax.nn.sigmoid(g))
        ret_s[rows, cols] = o.astype(BF16)

    n_units = len(units)
    inv = {}
    for k in range(n_units + 3):
        if k < n_units:
            att_scores(k)
        if 2 <= k < n_units + 2:
            att_values(k - 2, inv.pop(k - 2))
        if k < len(ret_units):
            ret_products(k)
        if 1 <= k <= len(ret_units):
            ret_finish(k - 1)
        if k >= 3:
            att_store(k - 3)
        if 1 <= k <= n_units:
            inv[k - 1] = att_softmax(k - 1)

    for c in range(D_MODEL // MERGE_COLS):
        cols = slice(c * MERGE_COLS, (c + 1) * MERGE_COLS)
        ba = _dot(att_s[...], wba_ref[:, cols])
        br = _dot(ret_s[...], wbr_ref[:, cols])
        g0 = jax.nn.sigmoid(gate_ref[0, :, c * MERGE_COLS:(c + 1) * MERGE_COLS].astype(F32))
        g1 = jax.nn.sigmoid(gate_ref[0, :, D_MODEL + c * MERGE_COLS:D_MODEL + (c + 1) * MERGE_COLS].astype(F32))
        mrg_s[:, cols] = (g0 * ba + g1 * br).astype(BF16)
    o_ref[0] = x_ref[0] + _dot(mrg_s[...], wo_ref[...])


def _mixer(qa, ka, va, qr, kr, vr, gr, gate, rb, x, sink, lf_h, lb_h, rng, wba, wbr, wo):
    b, s, d = x.shape
    t = MIXER_TILE
    nblk = t // BLOCK
    nb_total = s // BLOCK
    tok = lambda width: pl.BlockSpec((1, t, width), lambda bi, i: (bi, i, 0))
    prev = pl.BlockSpec((1, BLOCK, ATT_KV_W), lambda bi, i: (bi, jnp.maximum(i * nblk - 1, 0), 0))
    nxt = pl.BlockSpec((1, BLOCK, ATT_KV_W), lambda bi, i: (bi, jnp.minimum((i + 1) * nblk, nb_total - 1), 0))
    st = pl.BlockSpec((1, nblk, RET_HEADS, RET_HEAD_DIM, RET_HEAD_DIM), lambda bi, i: (bi, i, 0, 0, 0))
    smem = pl.BlockSpec(memory_space=pltpu.SMEM)
    return pl.pallas_call(
        _mixer_kernel,
        grid=(b, s // t),
        in_specs=[tok(ATT_Q_HEADS * LANES), prev, tok(ATT_KV_W), nxt, prev, tok(ATT_KV_W), nxt,
                  tok(RET_WIDTH), tok(RET_WIDTH), tok(RET_WIDTH), tok(RET_WIDTH), tok(2 * D_MODEL),
                  st, tok(d),
                  smem, _const_spec((RET_HEADS, LANES)), _const_spec((RET_HEADS, LANES)),
                  _const_spec((1, RET_WIDTH)),
                  _const_spec((ATT_Q_W, d)), _const_spec((RET_WIDTH, d)), _const_spec((d, d))],
        out_specs=tok(d),
        out_shape=jax.ShapeDtypeStruct((b, s, d), F32),
        scratch_shapes=[
            pltpu.VMEM((t + 2 * BLOCK, ATT_KV_W), BF16),
            pltpu.VMEM((ATT_KV_W, t + 2 * BLOCK), BF16),
            pltpu.VMEM((RET_HEADS, CHUNK, CHUNK), F32),
            pltpu.VMEM((2, RET_HEADS, CHUNK, CHUNK), F32),
            pltpu.VMEM((RET_HEADS, CHUNK, CHUNK), F32),
            pltpu.VMEM((RET_HEADS, RET_HEAD_DIM, RET_HEAD_DIM), F32),
            pltpu.VMEM((3, 3 * BLOCK, BLOCK), F32),
            pltpu.VMEM((2, 3 * BLOCK, 2 * BLOCK), F32),
            pltpu.VMEM((2, 3 * BLOCK, 2 * BLOCK), BF16),
            pltpu.VMEM((2, ATT_KV_W, 2 * BLOCK), F32),
            pltpu.VMEM((2, CHUNK, CHUNK), BF16),
            pltpu.VMEM((2, CHUNK, 2 * RET_HEAD_DIM), F32),
            pltpu.VMEM((t, ATT_Q_W), BF16),
            pltpu.VMEM((t, RET_WIDTH), BF16),
            pltpu.VMEM((t, d), BF16),
        ],
        compiler_params=pltpu.CompilerParams(
            dimension_semantics=("arbitrary", "arbitrary"), vmem_limit_bytes=VMEM_LIMIT),
        name="mixer",
    )(qa, ka, ka, ka, va, va, va, qr, kr, vr, gr, gate, rb, x, sink, lf_h, lb_h, rng, wba, wbr, wo)


def _ffn_kernel(final_norm, xp_ref, x_ref, xn_ref, g_ref, wi_ref, cw_ref, cb_ref, wo_ref, fg_ref,
                o_ref, gu_s):
    t = x_ref.shape[1]
    i = pl.program_id(1)
    n = pl.num_programs(1)
    g = g_ref[...]
    x = x_ref[0]
    hp = jnp.where(i > 0, _rmsnorm(xp_ref[0], g), 0.0)
    hn = jnp.where(i < n - 1, _rmsnorm(xn_ref[0], g), 0.0)
    h = _rmsnorm(x, g).astype(BF16)
    h_ext = jnp.concatenate([hp.astype(BF16), h, hn.astype(BF16)], axis=0)

    def chunk_body(c, carry):
        c0 = pl.multiple_of(c * FF_CHUNK, FF_CHUNK)
        a_ext = _dot(h_ext, wi_ref[:, pl.ds(c0, FF_CHUNK)])
        u = _dot(h, wi_ref[:, pl.ds(D_FF + c0, FF_CHUNK)])
        w = cw_ref[:, pl.ds(c0, FF_CHUNK)]
        lo = BF16_ROWS
        a = (cb_ref[:, pl.ds(c0, FF_CHUNK)]
             + a_ext[lo - 1:lo - 1 + t] * w[0:1]
             + a_ext[lo:lo + t] * w[1:2]
             + a_ext[lo + 1:lo + 1 + t] * w[2:3])
        gelu = 0.5 * a * (1.0 + lax.erf(a * (2.0 ** -0.5)))
        gu_s[:, pl.ds(c0, FF_CHUNK)] = (gelu * u).astype(BF16)
        return carry

    lax.fori_loop(0, D_FF // FF_CHUNK, chunk_body, 0, unroll=True)
    y = x + _dot(gu_s[...], wo_ref[...])
    if final_norm:
        y = _rmsnorm(y, fg_ref[...])
    o_ref[0] = y


def _ffn(x, g, wi, cw, cb, wo, fg, final_norm):
    b, s, d = x.shape
    t = min(FFN_TILE, s)
    hb = t // BF16_ROWS
    nh = s // BF16_ROWS
    tok = pl.BlockSpec((1, t, d), lambda bi, i: (bi, i, 0))
    prev = pl.BlockSpec((1, BF16_ROWS, d), lambda bi, i: (bi, jnp.maximum(i * hb - 1, 0), 0))
    nxt = pl.BlockSpec((1, BF16_ROWS, d), lambda bi, i: (bi, jnp.minimum((i + 1) * hb, nh - 1), 0))
    return pl.pallas_call(
        functools.partial(_ffn_kernel, final_norm),
        grid=(b, s // t),
        in_specs=[prev, tok, nxt, _const_spec((1, d)), _const_spec((d, 2 * D_FF)),
                  _const_spec((CONV_WIDTH, D_FF)), _const_spec((1, D_FF)), _const_spec((D_FF, d)),
                  _const_spec((1, d))],
        out_specs=tok,
        out_shape=jax.ShapeDtypeStruct((b, s, d), F32),
        scratch_shapes=[pltpu.VMEM((t, D_FF), BF16)],
        compiler_params=pltpu.CompilerParams(
            dimension_semantics=("arbitrary", "arbitrary"), vmem_limit_bytes=VMEM_LIMIT),
        name="ffn",
    )(x, x, x, g, wi, cw, cb, wo, fg)


def _rope_tables(s, half, reps):
    freqs = ROPE_THETA ** (-jnp.arange(half, dtype=F32) / half)
    ang = jnp.arange(s, dtype=F32)[:, None] * freqs[None, :]
    cos = jnp.cos(ang)
    sin = jnp.sin(ang)
    cos_t = jnp.tile(jnp.concatenate([cos, cos], axis=-1), (1, reps))
    sin_t = jnp.tile(jnp.concatenate([-sin, sin], axis=-1), (1, reps))
    return cos_t, sin_t


def _layer(x, p, tabs, final_g, final_norm):
    ca, sa, cr, sr = tabs
    qa, ka, va, qr, kr, vr, gr, gate = _inproj(x, p["norm_mix_g"], p["w_in"], ca, sa, cr, sr)
    rb = _states(kr, vr, p["lb_lanes"])
    x = _mixer(qa, ka, va, qr, kr, vr, gr, gate, rb, x, p["sink"], p["lf_heads"], p["lb_heads"],
               p["ret_norm_g"], p["w_branch_attn"], p["w_branch_ret"], p["w_out"])
    return _ffn(x, p["norm_ffn_g"], p["w_ffn_in"], p["conv_w"], p["conv_b"], p["w_ffn_out"],
                final_g, final_norm)


def kernel(x_prompt, x_sample, norm_mix_g, w_in, attn_sink, ret_log_decay_f, ret_log_decay_b, ret_norm_g,
           w_branch_attn, w_branch_ret, w_out, norm_ffn_g, w_ffn_in, conv_w, conv_b, w_ffn_out, final_norm_g):
    depth = w_in.shape[0]
    layers = []
    for l in range(depth):
        lf = ret_log_decay_f[l].astype(F32)
        lb = ret_log_decay_b[l].astype(F32)
        layers.append(dict(
            norm_mix_g=norm_mix_g[l].reshape(1, D_MODEL),
            w_in=w_in[l].astype(BF16),
            sink=attn_sink[l].astype(F32),
            lb_lanes=jnp.repeat(lb, RET_HEAD_DIM).reshape(1, RET_WIDTH),
            lf_heads=jnp.broadcast_to(lf[:, None], (RET_HEADS, LANES)),
            lb_heads=jnp.broadcast_to(lb[:, None], (RET_HEADS, LANES)),
            ret_norm_g=ret_norm_g[l].reshape(1, RET_WIDTH),
            w_branch_attn=w_branch_attn[l].astype(BF16),
            w_branch_ret=w_branch_ret[l].astype(BF16),
            w_out=w_out[l].astype(BF16),
            norm_ffn_g=norm_ffn_g[l].reshape(1, D_MODEL),
            w_ffn_in=w_ffn_in[l].astype(BF16),
            conv_w=conv_w[l],
            conv_b=conv_b[l].reshape(1, D_FF),
            w_ffn_out=w_ffn_out[l].astype(BF16),
        ))
    fg = final_norm_g.reshape(1, D_MODEL)
    outs = []
    for x in (x_prompt, x_sample):
        s = x.shape[1]
        tabs = _rope_tables(s, HEAD_DIM // 2, LANES // HEAD_DIM) + _rope_tables(s, RET_HEAD_DIM // 2, 1)
        for l in range(depth):
            x = _layer(x, layers[l], tabs, fg, l == depth - 1)
        outs.append(x)
    return tuple(outs)
```

```python
import functools

import jax
import jax.numpy as jnp
from jax import lax
from jax.experimental import pallas as pl
from jax.experimental.pallas import tpu as pltpu

D_MODEL = 1024
HEAD_DIM = 64
ATT_Q_HEADS = 8
ATT_KV_HEADS = 2
ATT_GROUP = ATT_Q_HEADS // ATT_KV_HEADS
ATT_Q_W = ATT_Q_HEADS * HEAD_DIM
ATT_KV_W = ATT_KV_HEADS * HEAD_DIM
WINDOW = 128
BLOCK = 128
RET_HEADS = 4
RET_HEAD_DIM = 128
RET_WIDTH = RET_HEADS * RET_HEAD_DIM
CHUNK = 128
D_FF = 2816
CONV_WIDTH = 3
ROPE_THETA = 10000.0
EPS = 1e-6
NEG_INF = -1e30
LOG2E = 1.4426950408889634
IN_WIDTH = ATT_Q_W + 2 * ATT_KV_W + 4 * RET_WIDTH + 2 * D_MODEL

LANES = 128
BF16_ROWS = 16
VMEM_LIMIT = 56 * 1024 * 1024

INPROJ_TILE = 1024
MIXER_TILE = 1024
FFN_TILE = 1024
FF_CHUNK = 256
STATE_CHUNKS = 8
MERGE_COLS = 256

F32 = jnp.float32
BF16 = jnp.bfloat16

_C_QA = 0
_C_KA = _C_QA + ATT_Q_W
_C_VA = _C_KA + ATT_KV_W
_C_QR = _C_VA + ATT_KV_W
_C_KR = _C_QR + RET_WIDTH
_C_VR = _C_KR + RET_WIDTH
_C_GR = _C_VR + RET_WIDTH
_C_GATE = _C_GR + RET_WIDTH


def _rmsnorm(x, g):
    return (x * lax.rsqrt(jnp.mean(x * x, axis=-1, keepdims=True) + EPS)) * g


def _dot(a, b):
    return jnp.dot(a, b, preferred_element_type=F32)


def _dot_nt(a, b):
    return lax.dot_general(a, b, (((1,), (1,)), ((), ())), preferred_element_type=F32)


def _dot_tn(a, b):
    return lax.dot_general(a, b, (((0,), (0,)), ((), ())), preferred_element_type=F32)


def _const_spec(shape):
    nd = len(shape)
    return pl.BlockSpec(shape, lambda *_: (0,) * nd, pipeline_mode=pl.Buffered(1))


def _inproj_kernel(x_ref, g_ref, w_ref, ca_ref, sa_ref, cr_ref, sr_ref,
                   qa_ref, ka_ref, va_ref, qr_ref, kr_ref, vr_ref, gr_ref, gate_ref):
    x = x_ref[0]
    hb = _rmsnorm(x, g_ref[...]).astype(BF16)
    t = x.shape[0]
    lane = lax.broadcasted_iota(jnp.int32, (t, LANES), 1)
    first_half = (lane % HEAD_DIM) < (HEAD_DIM // 2)
    ca, sa, cr, sr = ca_ref[...], sa_ref[...], cr_ref[...], sr_ref[...]

    def mm(c0, n):
        return _dot(hb, w_ref[:, c0:c0 + n])

    def rope_a(y):
        rot = jnp.where(first_half, pltpu.roll(y, LANES - HEAD_DIM // 2, 1),
                        pltpu.roll(y, HEAD_DIM // 2, 1))
        return y * ca + rot * sa

    def rope_r(y):
        return y * cr + pltpu.roll(y, RET_HEAD_DIM // 2, 1) * sr

    def slab(y, s):
        return y[:, s * LANES:(s + 1) * LANES]

    low = lane < HEAD_DIM
    y = mm(_C_QA, ATT_Q_W)
    for s in range(ATT_Q_W // LANES):
        r = rope_a(slab(y, s)) * (HEAD_DIM ** -0.5 * LOG2E)
        r_sw = pltpu.roll(r, HEAD_DIM, 1)
        kv_low = (2 * s) // ATT_GROUP == 0
        h0 = jnp.where(low, r, 0.0) if kv_low else jnp.where(low, 0.0, r_sw)
        h1 = jnp.where(low, r_sw, 0.0) if kv_low else jnp.where(low, 0.0, r)
        qa_ref[0, :, (2 * s) * LANES:(2 * s + 1) * LANES] = h0.astype(BF16)
        qa_ref[0, :, (2 * s + 1) * LANES:(2 * s + 2) * LANES] = h1.astype(BF16)
    y = mm(_C_KA, 2 * ATT_KV_W)
    ka_ref[0] = rope_a(slab(y, 0)).astype(BF16)
    va_ref[0] = slab(y, 1).astype(BF16)
    y = mm(_C_QR, RET_WIDTH)
    for s in range(RET_HEADS):
        qr_ref[0, :, s * LANES:(s + 1) * LANES] = rope_r(slab(y, s)).astype(BF16)
    y = mm(_C_KR, RET_WIDTH)
    for s in range(RET_HEADS):
        kr_ref[0, :, s * LANES:(s + 1) * LANES] = (rope_r(slab(y, s)) * (RET_HEAD_DIM ** -0.5)).astype(BF16)
    vr_ref[0] = mm(_C_VR, RET_WIDTH).astype(BF16)
    gr_ref[0] = mm(_C_GR, RET_WIDTH).astype(BF16)
    for c in range(4):
        n = 2 * D_MODEL // 4
        gate_ref[0, :, c * n:(c + 1) * n] = mm(_C_GATE + c * n, n).astype(BF16)


def _inproj(x, g, w, ca, sa, cr, sr):
    b, s, d = x.shape
    t = min(INPROJ_TILE, s)
    tok = lambda width: pl.BlockSpec((1, t, width), lambda bi, i: (bi, i, 0))
    tab = pl.BlockSpec((t, LANES), lambda bi, i: (i, 0))
    widths = [ATT_Q_HEADS * LANES, ATT_KV_W, ATT_KV_W, RET_WIDTH, RET_WIDTH, RET_WIDTH, RET_WIDTH, 2 * D_MODEL]
    return pl.pallas_call(
        _inproj_kernel,
        grid=(b, s // t),
        in_specs=[tok(d), _const_spec((1, d)), _const_spec((d, IN_WIDTH)), tab, tab, tab, tab],
        out_specs=[tok(wd) for wd in widths],
        out_shape=[jax.ShapeDtypeStruct((b, s, wd), BF16) for wd in widths],
        compiler_params=pltpu.CompilerParams(
            dimension_semantics=("arbitrary", "arbitrary"), vmem_limit_bytes=VMEM_LIMIT),
        name="inproj",
    )(x, g, w, ca, sa, cr, sr)


def _state_kernel(k_ref, v_ref, lb_ref, rb_ref, st_ref):
    @pl.when(pl.program_id(1) == 0)
    def _():
        st_ref[...] = jnp.zeros_like(st_ref)

    row = lax.broadcasted_iota(jnp.int32, (CHUNK, RET_WIDTH), 0).astype(F32)
    lb = lb_ref[...]
    zeta = jnp.exp(lb * row)
    dec = jnp.exp(lb * float(CHUNK))
    for c in reversed(range(k_ref.shape[1] // CHUNK)):
        rows = slice(c * CHUNK, (c + 1) * CHUNK)
        for h in range(RET_HEADS):
            cols = slice(h * RET_HEAD_DIM, (h + 1) * RET_HEAD_DIM)
            st = st_ref[h]
            rb_ref[0, c, h] = st.astype(BF16)
            zv = (v_ref[0, rows, cols].astype(F32) * zeta[:, cols]).astype(BF16)
            st_ref[h] = st * dec[:, cols] + _dot_tn(k_ref[0, rows, cols], zv)


def _states(kr, vr, lb_l):
    b, s, _ = kr.shape
    nc = s // CHUNK
    per_step = min(STATE_CHUNKS, nc)
    ns = nc // per_step
    t = per_step * CHUNK
    bwd = pl.BlockSpec((1, t, RET_WIDTH), lambda bi, i: (bi, ns - 1 - i, 0))
    out_b = pl.BlockSpec((1, per_step, RET_HEADS, RET_HEAD_DIM, RET_HEAD_DIM),
                         lambda bi, i: (bi, ns - 1 - i, 0, 0, 0))
    return pl.pallas_call(
        _state_kernel,
        grid=(b, ns),
        in_specs=[bwd, bwd, _const_spec((1, RET_WIDTH))],
        out_specs=out_b,
        out_shape=jax.ShapeDtypeStruct((b, nc, RET_HEADS, RET_HEAD_DIM, RET_HEAD_DIM), BF16),
        scratch_shapes=[pltpu.VMEM((RET_HEADS, RET_HEAD_DIM, RET_HEAD_DIM), F32)],
        compiler_params=pltpu.CompilerParams(
            dimension_semantics=("arbitrary", "arbitrary"), vmem_limit_bytes=VMEM_LIMIT),
        name="ret_states",
    )(kr, vr, lb_l)


def _mixer_kernel(qa_ref, kp_ref, kc_ref, kn_ref, vp_ref, vc_ref, vn_ref,
                  qr_ref, kr_ref, vr_ref, gr_ref, gate_ref, rb_ref, x_ref,
                  sink_ref, lf_ref, lb_ref, rng_ref, wba_ref, wbr_ref, wo_ref,
                  o_ref,
                  k_s, vt_s, dm_s, xi_s, zeta_s, st_s, cap_s, s_buf, p_buf, ot_buf, in_buf, cr_buf,
                  att_s, ret_s, mrg_s):
    t = x_ref.shape[1]
    nblk = t // BLOCK
    tile = pl.program_id(1)

    k_s[0:BLOCK] = kp_ref[0]
    k_s[BLOCK:BLOCK + t] = kc_ref[0]
    k_s[BLOCK + t:] = kn_ref[0]
    for i in range(nblk + 2):
        src = vp_ref if i == 0 else (vn_ref if i == nblk + 1 else vc_ref)
        r0 = 0 if i in (0, nblk + 1) else (i - 1) * BLOCK
        vt_s[0:ATT_KV_W, i * BLOCK:(i + 1) * BLOCK] = src[0, r0:r0 + BLOCK, :].astype(F32).T.astype(BF16)
    ones_row = lax.broadcasted_iota(jnp.int32, (BF16_ROWS, t + 2 * BLOCK), 0) == 0
    vt_s[ATT_KV_W:, :] = jnp.where(ones_row, 1.0, 0.0).astype(BF16)

    ri = lax.broadcasted_iota(jnp.int32, (CHUNK, CHUNK), 0).astype(F32)
    ci = lax.broadcasted_iota(jnp.int32, (CHUNK, CHUNK), 1).astype(F32)
    diff = ri - ci
    for h in range(RET_HEADS):
        lf = lf_ref[h:h + 1, :]
        lb = lb_ref[h:h + 1, :]
        dm_s[h] = (jnp.where(diff >= 0, jnp.exp(lf * jnp.maximum(diff, 0.0)), 0.0)
                   + jnp.where(diff < 0, jnp.exp(lb * jnp.maximum(-diff, 0.0)), 0.0))
        xi_s[0, h] = jnp.exp(lf * (ri + 1.0))
        xi_s[1, h] = jnp.exp(lb * (float(CHUNK) - ri))
        zeta_s[h] = jnp.exp(lf * (float(CHUNK) - 1.0 - ri))

    @pl.when(tile == 0)
    def _():
        st_s[...] = jnp.zeros_like(st_s)

    kj = lax.broadcasted_iota(jnp.int32, (3 * BLOCK, BLOCK), 0)
    qi = lax.broadcasted_iota(jnp.int32, (3 * BLOCK, BLOCK), 1)
    band = jnp.abs(kj - BLOCK - qi) <= WINDOW
    first = tile == 0
    last = tile == pl.num_programs(1) - 1
    cap_s[0] = jnp.where(band, jnp.inf, NEG_INF)
    cap_s[1] = jnp.where(band & ((kj >= BLOCK) | jnp.logical_not(first)), jnp.inf, NEG_INF)
    cap_s[2] = jnp.where(band & ((kj < 2 * BLOCK) | jnp.logical_not(last)), jnp.inf, NEG_INF)
    head1 = lax.broadcasted_iota(jnp.int32, (1, 2 * BLOCK), 1) >= BLOCK
    low_o = lax.broadcasted_iota(jnp.int32, (BLOCK, LANES), 1) < HEAD_DIM

    n_slab = ATT_Q_HEADS // 2
    units = [(j, s) for j in range(nblk) for s in range(n_slab)]

    def att_scores(u):
        j, s = units[u]
        rows = slice(j * BLOCK, (j + 1) * BLOCK)
        q2 = jnp.concatenate([qa_ref[0, rows, (2 * s) * LANES:(2 * s + 1) * LANES],
                              qa_ref[0, rows, (2 * s + 1) * LANES:(2 * s + 2) * LANES]], axis=0)
        s_buf[u % 2] = _dot_nt(k_s[j * BLOCK:(j + 3) * BLOCK, :], q2)

    def att_softmax(u):
        j, s = units[u]
        cap = cap_s[1 if j == 0 else (2 if j == nblk - 1 else 0)]
        sc = s_buf[u % 2]
        sc = jnp.concatenate(
            [jnp.concatenate([jnp.minimum(sc[:BLOCK, cs], cap[:BLOCK]),
                              sc[BLOCK:2 * BLOCK, cs],
                              jnp.minimum(sc[2 * BLOCK:, cs], cap[2 * BLOCK:])], axis=0)
             for cs in (slice(0, BLOCK), slice(BLOCK, 2 * BLOCK))], axis=1)
        sink = jnp.where(head1, sink_ref[2 * s + 1], sink_ref[2 * s]) * LOG2E
        m = jnp.maximum(jnp.max(sc, axis=0, keepdims=True), sink)
        p_buf[u % 2] = jnp.exp2(sc - m).astype(BF16)
        return jnp.exp2(sink - m)

    def att_values(u, sink_term):
        j, s = units[u]
        res = _dot(vt_s[:, j * BLOCK:(j + 3) * BLOCK], p_buf[u % 2])
        den = res[ATT_KV_W:ATT_KV_W + 1, :] + sink_term
        ot_buf[u % 2] = res[:ATT_KV_W] * (1.0 / den)

    def att_store(u):
        j, s = units[u]
        rows = slice(j * BLOCK, (j + 1) * BLOCK)
        ot = ot_buf[u % 2]
        o0 = ot[:, :BLOCK].T
        o1 = ot[:, BLOCK:].T
        if (2 * s) // ATT_GROUP == 0:
            o = jnp.where(low_o, o0, pltpu.roll(o1, HEAD_DIM, 1))
        else:
            o = jnp.where(low_o, pltpu.roll(o0, HEAD_DIM, 1), o1)
        att_s[rows, s * LANES:(s + 1) * LANES] = o.astype(BF16)

    ret_units = [(j, h) for j in range(nblk) for h in range(RET_HEADS)]

    def ret_products(u):
        j, h = ret_units[u]
        rows = slice(j * BLOCK, (j + 1) * BLOCK)
        cols = slice(h * LANES, (h + 1) * LANES)
        q = qr_ref[0, rows, cols]
        k = kr_ref[0, rows, cols]
        in_buf[u % 2] = (_dot_nt(q, k) * dm_s[h]).astype(BF16)
        st = st_s[h]
        cr_buf[u % 2] = _dot(q, jnp.concatenate([st.astype(BF16), rb_ref[0, j, h]], axis=1))
        zv = (vr_ref[0, rows, cols].astype(F32) * zeta_s[h]).astype(BF16)
        st_s[h] = st * jnp.exp(lf_ref[h:h + 1, :] * float(CHUNK)) + _dot_tn(k, zv)

    def ret_finish(u):
        j, h = ret_units[u]
        rows = slice(j * BLOCK, (j + 1) * BLOCK)
        cols = slice(h * LANES, (h + 1) * LANES)
        cross = cr_buf[u % 2]
        o = _dot(in_buf[u % 2], vr_ref[0, rows, cols])
        o = o + cross[:, :LANES] * xi_s[0, h] + cross[:, LANES:] * xi_s[1, h]
        o = o * lax.rsqrt(jnp.mean(o * o, axis=-1, keepdims=True) + EPS)
        g = gr_ref[0, rows, cols].astype(F32)
        o = (o * rng_ref[:, cols]) * (g * jax.nn.sigmoid(g))
        ret_s[rows, cols] = o.astype(BF16)

    n_units = len(units)
    sink_terms = {}
    for k in range(n_units + 3):
        if k < n_units:
            att_scores(k)
        if 2 <= k < n_units + 2:
            att_values(k - 2, sink_terms.pop(k - 2))
        if k < len(ret_units):
            ret_products(k)
        if 1 <= k <= len(ret_units):
            ret_finish(k - 1)
        if k >= 3:
            att_store(k - 3)
        if 1 <= k <= n_units:
            sink_terms[k - 1] = att_softmax(k - 1)

    for c in range(D_MODEL // MERGE_COLS):
        cols = slice(c * MERGE_COLS, (c + 1) * MERGE_COLS)
        ba = _dot(att_s[...], wba_ref[:, cols])
        br = _dot(ret_s[...], wbr_ref[:, cols])
        g0 = jax.nn.sigmoid(gate_ref[0, :, c * MERGE_COLS:(c + 1) * MERGE_COLS].astype(F32))
        g1 = jax.nn.sigmoid(gate_ref[0, :, D_MODEL + c * MERGE_COLS:D_MODEL + (c + 1) * MERGE_COLS].astype(F32))
        mrg_s[:, cols] = (g0 * ba + g1 * br).astype(BF16)
    o_ref[0] = x_ref[0] + _dot(mrg_s[...], wo_ref[...])


def _mixer(qa, ka, va, qr, kr, vr, gr, gate, rb, x, sink, lf_h, lb_h, rng, wba, wbr, wo):
    b, s, d = x.shape
    t = MIXER_TILE
    nblk = t // BLOCK
    nb_total = s // BLOCK
    tok = lambda width: pl.BlockSpec((1, t, width), lambda bi, i: (bi, i, 0))
    prev = pl.BlockSpec((1, BLOCK, ATT_KV_W), lambda bi, i: (bi, jnp.maximum(i * nblk - 1, 0), 0))
    nxt = pl.BlockSpec((1, BLOCK, ATT_KV_W), lambda bi, i: (bi, jnp.minimum((i + 1) * nblk, nb_total - 1), 0))
    st = pl.BlockSpec((1, nblk, RET_HEADS, RET_HEAD_DIM, RET_HEAD_DIM), lambda bi, i: (bi, i, 0, 0, 0))
    smem = pl.BlockSpec(memory_space=pltpu.SMEM)
    return pl.pallas_call(
        _mixer_kernel,
        grid=(b, s // t),
        in_specs=[tok(ATT_Q_HEADS * LANES), prev, tok(ATT_KV_W), nxt, prev, tok(ATT_KV_W), nxt,
                  tok(RET_WIDTH), tok(RET_WIDTH), tok(RET_WIDTH), tok(RET_WIDTH), tok(2 * D_MODEL),
                  st, tok(d),
                  smem, _const_spec((RET_HEADS, LANES)), _const_spec((RET_HEADS, LANES)),
                  _const_spec((1, RET_WIDTH)),
                  _const_spec((ATT_Q_W, d)), _const_spec((RET_WIDTH, d)), _const_spec((d, d))],
        out_specs=tok(d),
        out_shape=jax.ShapeDtypeStruct((b, s, d), F32),
        scratch_shapes=[
            pltpu.VMEM((t + 2 * BLOCK, ATT_KV_W), BF16),
            pltpu.VMEM((ATT_KV_W + BF16_ROWS, t + 2 * BLOCK), BF16),
            pltpu.VMEM((RET_HEADS, CHUNK, CHUNK), F32),
            pltpu.VMEM((2, RET_HEADS, CHUNK, CHUNK), F32),
            pltpu.VMEM((RET_HEADS, CHUNK, CHUNK), F32),
            pltpu.VMEM((RET_HEADS, RET_HEAD_DIM, RET_HEAD_DIM), F32),
            pltpu.VMEM((3, 3 * BLOCK, BLOCK), F32),
            pltpu.VMEM((2, 3 * BLOCK, 2 * BLOCK), F32),
            pltpu.VMEM((2, 3 * BLOCK, 2 * BLOCK), BF16),
            pltpu.VMEM((2, ATT_KV_W, 2 * BLOCK), F32),
            pltpu.VMEM((2, CHUNK, CHUNK), BF16),
            pltpu.VMEM((2, CHUNK, 2 * RET_HEAD_DIM), F32),
            pltpu.VMEM((t, ATT_Q_W), BF16),
            pltpu.VMEM((t, RET_WIDTH), BF16),
            pltpu.VMEM((t, d), BF16),
        ],
        compiler_params=pltpu.CompilerParams(
            dimension_semantics=("arbitrary", "arbitrary"), vmem_limit_bytes=VMEM_LIMIT),
        name="mixer",
    )(qa, ka, ka, ka, va, va, va, qr, kr, vr, gr, gate, rb, x, sink, lf_h, lb_h, rng, wba, wbr, wo)


def _ffn_kernel(final_norm, xp_ref, x_ref, xn_ref, g_ref, wi_ref, cw_ref, cb_ref, wo_ref, fg_ref,
                o_ref, gu_s):
    t = x_ref.shape[1]
    i = pl.program_id(1)
    n = pl.num_programs(1)
    g = g_ref[...]
    x = x_ref[0]
    hp = jnp.where(i > 0, _rmsnorm(xp_ref[0], g), 0.0)
    hn = jnp.where(i < n - 1, _rmsnorm(xn_ref[0], g), 0.0)
    h = _rmsnorm(x, g).astype(BF16)
    h_ext = jnp.concatenate([hp.astype(BF16), h, hn.astype(BF16)], axis=0)

    def chunk_body(c, carry):
        c0 = pl.multiple_of(c * FF_CHUNK, FF_CHUNK)
        a_ext = _dot(h_ext, wi_ref[:, pl.ds(c0, FF_CHUNK)])
        u = _dot(h, wi_ref[:, pl.ds(D_FF + c0, FF_CHUNK)])
        w = cw_ref[:, pl.ds(c0, FF_CHUNK)]
        lo = BF16_ROWS
        a = (cb_ref[:, pl.ds(c0, FF_CHUNK)]
             + a_ext[lo - 1:lo - 1 + t] * w[0:1]
             + a_ext[lo:lo + t] * w[1:2]
             + a_ext[lo + 1:lo + 1 + t] * w[2:3])
        gelu = 0.5 * a * (1.0 + lax.erf(a * (2.0 ** -0.5)))
        gu_s[:, pl.ds(c0, FF_CHUNK)] = (gelu * u).astype(BF16)
        return carry

    lax.fori_loop(0, D_FF // FF_CHUNK, chunk_body, 0, unroll=True)
    y = x + _dot(gu_s[...], wo_ref[...])
    if final_norm:
        y = _rmsnorm(y, fg_ref[...])
    o_ref[0] = y


def _ffn(x, g, wi, cw, cb, wo, fg, final_norm):
    b, s, d = x.shape
    t = min(FFN_TILE, s)
    hb = t // BF16_ROWS
    nh = s // BF16_ROWS
    tok = pl.BlockSpec((1, t, d), lambda bi, i: (bi, i, 0))
    prev = pl.BlockSpec((1, BF16_ROWS, d), lambda bi, i: (bi, jnp.maximum(i * hb - 1, 0), 0))
    nxt = pl.BlockSpec((1, BF16_ROWS, d), lambda bi, i: (bi, jnp.minimum((i + 1) * hb, nh - 1), 0))
    return pl.pallas_call(
        functools.partial(_ffn_kernel, final_norm),
        grid=(b, s // t),
        in_specs=[prev, tok, nxt, _const_spec((1, d)), _const_spec((d, 2 * D_FF)),
                  _const_spec((CONV_WIDTH, D_FF)), _const_spec((1, D_FF)), _const_spec((D_FF, d)),
                  _const_spec((1, d))],
        out_specs=tok,
        out_shape=jax.ShapeDtypeStruct((b, s, d), F32),
        scratch_shapes=[pltpu.VMEM((t, D_FF), BF16)],
        compiler_params=pltpu.CompilerParams(
            dimension_semantics=("arbitrary", "arbitrary"), vmem_limit_bytes=VMEM_LIMIT),
        name="ffn",
    )(x, x, x, g, wi, cw, cb, wo, fg)


def _rope_tables(s, half, reps):
    freqs = ROPE_THETA ** (-jnp.arange(half, dtype=F32) / half)
    ang = jnp.arange(s, dtype=F32)[:, None] * freqs[None, :]
    cos = jnp.cos(ang)
    sin = jnp.sin(ang)
    cos_t = jnp.tile(jnp.concatenate([cos, cos], axis=-1), (1, reps))
    sin_t = jnp.tile(jnp.concatenate([-sin, sin], axis=-1), (1, reps))
    return cos_t, sin_t


def _layer(x, p, tabs, final_g, final_norm):
    ca, sa, cr, sr = tabs
    qa, ka, va, qr, kr, vr, gr, gate = _inproj(x, p["norm_mix_g"], p["w_in"], ca, sa, cr, sr)
    rb = _states(kr, vr, p["lb_lanes"])
    x = _mixer(qa, ka, va, qr, kr, vr, gr, gate, rb, x, p["sink"], p["lf_heads"], p["lb_heads"],
               p["ret_norm_g"], p["w_branch_attn"], p["w_branch_ret"], p["w_out"])
    return _ffn(x, p["norm_ffn_g"], p["w_ffn_in"], p["conv_w"], p["conv_b"], p["w_ffn_out"],
                final_g, final_norm)


def kernel(x_prompt, x_sample, norm_mix_g, w_in, attn_sink, ret_log_decay_f, ret_log_decay_b, ret_norm_g,
           w_branch_attn, w_branch_ret, w_out, norm_ffn_g, w_ffn_in, conv_w, conv_b, w_ffn_out, final_norm_g):
    depth = w_in.shape[0]
    layers = []
    for l in range(depth):
        lf = ret_log_decay_f[l].astype(F32)
        lb = ret_log_decay_b[l].astype(F32)
        layers.append(dict(
            norm_mix_g=norm_mix_g[l].reshape(1, D_MODEL),
            w_in=w_in[l].astype(BF16),
            sink=attn_sink[l].astype(F32),
            lb_lanes=jnp.repeat(lb, RET_HEAD_DIM).reshape(1, RET_WIDTH),
            lf_heads=jnp.broadcast_to(lf[:, None], (RET_HEADS, LANES)),
            lb_heads=jnp.broadcast_to(lb[:, None], (RET_HEADS, LANES)),
            ret_norm_g=ret_norm_g[l].reshape(1, RET_WIDTH),
            w_branch_attn=w_branch_attn[l].astype(BF16),
            w_branch_ret=w_branch_ret[l].astype(BF16),
            w_out=w_out[l].astype(BF16),
            norm_ffn_g=norm_ffn_g[l].reshape(1, D_MODEL),
            w_ffn_in=w_ffn_in[l].astype(BF16),
            conv_w=conv_w[l],
            conv_b=conv_b[l].reshape(1, D_FF),
            w_ffn_out=w_ffn_out[l].astype(BF16),
        ))
    fg = final_norm_g.reshape(1, D_MODEL)
    outs = []
    for x in (x_prompt, x_sample):
        s = x.shape[1]
        tabs = _rope_tables(s, HEAD_DIM // 2, LANES // HEAD_DIM) + _rope_tables(s, RET_HEAD_DIM // 2, 1)
        for l in range(depth):
            x = _layer(x, layers[l], tabs, fg, l == depth - 1)
        outs.append(x)
    return tuple(outs)
```

```python
import functools

import jax
import jax.numpy as jnp
from jax import lax
from jax.experimental import pallas as pl
from jax.experimental.pallas import tpu as pltpu

D_MODEL = 1024
HEAD_DIM = 64
ATT_Q_HEADS = 8
ATT_KV_HEADS = 2
ATT_GROUP = ATT_Q_HEADS // ATT_KV_HEADS
ATT_Q_W = ATT_Q_HEADS * HEAD_DIM
ATT_KV_W = ATT_KV_HEADS * HEAD_DIM
WINDOW = 128
BLOCK = 128
RET_HEADS = 4
RET_HEAD_DIM = 128
RET_WIDTH = RET_HEADS * RET_HEAD_DIM
CHUNK = 128
D_FF = 2816
CONV_WIDTH = 3
ROPE_THETA = 10000.0
EPS = 1e-6
NEG_INF = -1e30
LOG2E = 1.4426950408889634
IN_WIDTH = ATT_Q_W + 2 * ATT_KV_W + 4 * RET_WIDTH + 2 * D_MODEL

LANES = 128
BF16_ROWS = 16
VMEM_LIMIT = 56 * 1024 * 1024

INPROJ_TILE = 1024
MIXER_TILE = 1024
FFN_TILE = 1024
FF_CHUNK = 256
STATE_CHUNKS = 8
MERGE_COLS = 256

F32 = jnp.float32
BF16 = jnp.bfloat16

_C_QA = 0
_C_KA = _C_QA + ATT_Q_W
_C_VA = _C_KA + ATT_KV_W
_C_QR = _C_VA + ATT_KV_W
_C_KR = _C_QR + RET_WIDTH
_C_VR = _C_KR + RET_WIDTH
_C_GR = _C_VR + RET_WIDTH
_C_GATE = _C_GR + RET_WIDTH


def _rmsnorm(x, g):
    return (x * lax.rsqrt(jnp.mean(x * x, axis=-1, keepdims=True) + EPS)) * g


def _dot(a, b):
    return jnp.dot(a, b, preferred_element_type=F32)


def _dot_nt(a, b):
    return lax.dot_general(a, b, (((1,), (1,)), ((), ())), preferred_element_type=F32)


def _dot_tn(a, b):
    return lax.dot_general(a, b, (((0,), (0,)), ((), ())), preferred_element_type=F32)


def _const_spec(shape):
    nd = len(shape)
    return pl.BlockSpec(shape, lambda *_: (0,) * nd, pipeline_mode=pl.Buffered(1))


def _inproj_kernel(x_ref, g_ref, w_ref, ca_ref, sa_ref, cr_ref, sr_ref,
                   qa_ref, ka_ref, va_ref, qr_ref, kr_ref, vr_ref, gr_ref, gate_ref):
    x = x_ref[0]
    hb = _rmsnorm(x, g_ref[...]).astype(BF16)
    t = x.shape[0]
    lane = lax.broadcasted_iota(jnp.int32, (t, LANES), 1)
    first_half = (lane % HEAD_DIM) < (HEAD_DIM // 2)
    ca, sa, cr, sr = ca_ref[...], sa_ref[...], cr_ref[...], sr_ref[...]

    def mm(c0, n):
        return _dot(hb, w_ref[:, c0:c0 + n])

    def rope_a(y):
        rot = jnp.where(first_half, pltpu.roll(y, LANES - HEAD_DIM // 2, 1),
                        pltpu.roll(y, HEAD_DIM // 2, 1))
        return y * ca + rot * sa

    def rope_r(y):
        return y * cr + pltpu.roll(y, RET_HEAD_DIM // 2, 1) * sr

    def slab(y, s):
        return y[:, s * LANES:(s + 1) * LANES]

    low = lane < HEAD_DIM
    y = mm(_C_QA, ATT_Q_W)
    for s in range(ATT_Q_W // LANES):
        r = rope_a(slab(y, s)) * (HEAD_DIM ** -0.5 * LOG2E)
        r_sw = pltpu.roll(r, HEAD_DIM, 1)
        kv_low = (2 * s) // ATT_GROUP == 0
        h0 = jnp.where(low, r, 0.0) if kv_low else jnp.where(low, 0.0, r_sw)
        h1 = jnp.where(low, r_sw, 0.0) if kv_low else jnp.where(low, 0.0, r)
        qa_ref[0, :, (2 * s) * LANES:(2 * s + 1) * LANES] = h0.astype(BF16)
        qa_ref[0, :, (2 * s + 1) * LANES:(2 * s + 2) * LANES] = h1.astype(BF16)
    y = mm(_C_KA, 2 * ATT_KV_W)
    ka_ref[0] = rope_a(slab(y, 0)).astype(BF16)
    va_ref[0] = slab(y, 1).astype(BF16)
    y = mm(_C_QR, RET_WIDTH)
    for s in range(RET_HEADS):
        qr_ref[0, :, s * LANES:(s + 1) * LANES] = rope_r(slab(y, s)).astype(BF16)
    y = mm(_C_KR, RET_WIDTH)
    for s in range(RET_HEADS):
        kr_ref[0, :, s * LANES:(s + 1) * LANES] = (rope_r(slab(y, s)) * (RET_HEAD_DIM ** -0.5)).astype(BF16)
    vr_ref[0] = mm(_C_VR, RET_WIDTH).astype(BF16)
    gr_ref[0] = mm(_C_GR, RET_WIDTH).astype(BF16)
    for c in range(4):
        n = 2 * D_MODEL // 4
        gate_ref[0, :, c * n:(c + 1) * n] = mm(_C_GATE + c * n, n).astype(BF16)


def _inproj(x, g, w, ca, sa, cr, sr):
    b, s, d = x.shape
    t = min(INPROJ_TILE, s)
    tok = lambda width: pl.BlockSpec((1, t, width), lambda bi, i: (bi, i, 0))
    tab = pl.BlockSpec((t, LANES), lambda bi, i: (i, 0))
    widths = [ATT_Q_HEADS * LANES, ATT_KV_W, ATT_KV_W, RET_WIDTH, RET_WIDTH, RET_WIDTH, RET_WIDTH, 2 * D_MODEL]
    return pl.pallas_call(
        _inproj_kernel,
        grid=(b, s // t),
        in_specs=[tok(d), _const_spec((1, d)), _const_spec((d, IN_WIDTH)), tab, tab, tab, tab],
        out_specs=[tok(wd) for wd in widths],
        out_shape=[jax.ShapeDtypeStruct((b, s, wd), BF16) for wd in widths],
        compiler_params=pltpu.CompilerParams(
            dimension_semantics=("arbitrary", "arbitrary"), vmem_limit_bytes=VMEM_LIMIT),
        name="inproj",
    )(x, g, w, ca, sa, cr, sr)


def _state_kernel(k_ref, v_ref, lb_ref, rb_ref, st_ref):
    @pl.when(pl.program_id(1) == 0)
    def _():
        st_ref[...] = jnp.zeros_like(st_ref)

    row = lax.broadcasted_iota(jnp.int32, (CHUNK, RET_WIDTH), 0).astype(F32)
    lb = lb_ref[...]
    zeta = jnp.exp(lb * row)
    dec = jnp.exp(lb * float(CHUNK))
    for c in reversed(range(k_ref.shape[1] // CHUNK)):
        rows = slice(c * CHUNK, (c + 1) * CHUNK)
        for h in range(RET_HEADS):
            cols = slice(h * RET_HEAD_DIM, (h + 1) * RET_HEAD_DIM)
            st = st_ref[h]
            rb_ref[0, c, h] = st.astype(BF16)
            zv = (v_ref[0, rows, cols].astype(F32) * zeta[:, cols]).astype(BF16)
            st_ref[h] = st * dec[:, cols] + _dot_tn(k_ref[0, rows, cols], zv)


def _states(kr, vr, lb_l):
    b, s, _ = kr.shape
    nc = s // CHUNK
    per_step = min(STATE_CHUNKS, nc)
    ns = nc // per_step
    t = per_step * CHUNK
    bwd = pl.BlockSpec((1, t, RET_WIDTH), lambda bi, i: (bi, ns - 1 - i, 0))
    out_b = pl.BlockSpec((1, per_step, RET_HEADS, RET_HEAD_DIM, RET_HEAD_DIM),
                         lambda bi, i: (bi, ns - 1 - i, 0, 0, 0))
    return pl.pallas_call(
        _state_kernel,
        grid=(b, ns),
        in_specs=[bwd, bwd, _const_spec((1, RET_WIDTH))],
        out_specs=out_b,
        out_shape=jax.ShapeDtypeStruct((b, nc, RET_HEADS, RET_HEAD_DIM, RET_HEAD_DIM), BF16),
        scratch_shapes=[pltpu.VMEM((RET_HEADS, RET_HEAD_DIM, RET_HEAD_DIM), F32)],
        compiler_params=pltpu.CompilerParams(
            dimension_semantics=("arbitrary", "arbitrary"), vmem_limit_bytes=VMEM_LIMIT),
        name="ret_states",
    )(kr, vr, lb_l)


def _mixer_kernel(qa_ref, kp_ref, kc_ref, kn_ref, vp_ref, vc_ref, vn_ref,
                  qr_ref, kr_ref, vr_ref, gr_ref, gate_ref, rb_ref, x_ref,
                  sink_ref, lf_ref, lb_ref, rng_ref, wba_ref, wbr_ref, wo_ref,
                  o_ref,
                  k_s, vt_s, dm_s, xi_s, zeta_s, st_s, cap_s, s_buf, p_buf, ot_buf, in_buf, cr_buf,
                  att_s, ret_s, mrg_s):
    t = x_ref.shape[1]
    nblk = t // BLOCK
    tile = pl.program_id(1)

    k_s[0:BLOCK] = kp_ref[0]
    k_s[BLOCK:BLOCK + t] = kc_ref[0]
    k_s[BLOCK + t:] = kn_ref[0]
    for i in range(nblk + 2):
        src = vp_ref if i == 0 else (vn_ref if i == nblk + 1 else vc_ref)
        r0 = 0 if i in (0, nblk + 1) else (i - 1) * BLOCK
        vt_s[BF16_ROWS:, i * BLOCK:(i + 1) * BLOCK] = src[0, r0:r0 + BLOCK, :].astype(F32).T.astype(BF16)
    ones_row = lax.broadcasted_iota(jnp.int32, (BF16_ROWS, t + 2 * BLOCK), 0) == 0
    vt_s[0:BF16_ROWS, :] = jnp.where(ones_row, 1.0, 0.0).astype(BF16)

    ri = lax.broadcasted_iota(jnp.int32, (CHUNK, CHUNK), 0).astype(F32)
    ci = lax.broadcasted_iota(jnp.int32, (CHUNK, CHUNK), 1).astype(F32)
    diff = ri - ci
    for h in range(RET_HEADS):
        lf = lf_ref[h:h + 1, :]
        lb = lb_ref[h:h + 1, :]
        dm_s[h] = (jnp.where(diff >= 0, jnp.exp(lf * jnp.maximum(diff, 0.0)), 0.0)
                   + jnp.where(diff < 0, jnp.exp(lb * jnp.maximum(-diff, 0.0)), 0.0))
        xi_s[0, h] = jnp.exp(lf * (ri + 1.0))
        xi_s[1, h] = jnp.exp(lb * (float(CHUNK) - ri))
        zeta_s[h] = jnp.exp(lf * (float(CHUNK) - 1.0 - ri))

    @pl.when(tile == 0)
    def _():
        st_s[...] = jnp.zeros_like(st_s)

    kj = lax.broadcasted_iota(jnp.int32, (3 * BLOCK, BLOCK), 0)
    qi = lax.broadcasted_iota(jnp.int32, (3 * BLOCK, BLOCK), 1)
    band = jnp.abs(kj - BLOCK - qi) <= WINDOW
    first = tile == 0
    last = tile == pl.num_programs(1) - 1
    cap_s[0] = jnp.where(band, jnp.inf, NEG_INF)
    cap_s[1] = jnp.where(band & ((kj >= BLOCK) | jnp.logical_not(first)), jnp.inf, NEG_INF)
    cap_s[2] = jnp.where(band & ((kj < 2 * BLOCK) | jnp.logical_not(last)), jnp.inf, NEG_INF)
    head1 = lax.broadcasted_iota(jnp.int32, (1, 2 * BLOCK), 1) >= BLOCK
    low_o = lax.broadcasted_iota(jnp.int32, (BLOCK, LANES), 1) < HEAD_DIM

    n_slab = ATT_Q_HEADS // 2
    units = [(j, s) for j in range(nblk) for s in range(n_slab)]

    def att_scores(u):
        j, s = units[u]
        rows = slice(j * BLOCK, (j + 1) * BLOCK)
        q2 = jnp.concatenate([qa_ref[0, rows, (2 * s) * LANES:(2 * s + 1) * LANES],
                              qa_ref[0, rows, (2 * s + 1) * LANES:(2 * s + 2) * LANES]], axis=0)
        sc = _dot_nt(k_s[j * BLOCK:(j + 3) * BLOCK, :], q2)
        cap = cap_s[1 if j == 0 else (2 if j == nblk - 1 else 0)]
        for cs in (slice(0, BLOCK), slice(BLOCK, 2 * BLOCK)):
            s_buf[u % 2, 0:BLOCK, cs] = jnp.minimum(sc[:BLOCK, cs], cap[:BLOCK])
            s_buf[u % 2, BLOCK:2 * BLOCK, cs] = sc[BLOCK:2 * BLOCK, cs]
            s_buf[u % 2, 2 * BLOCK:, cs] = jnp.minimum(sc[2 * BLOCK:, cs], cap[2 * BLOCK:])

    def att_softmax(u):
        j, s = units[u]
        sink = jnp.where(head1, sink_ref[2 * s + 1], sink_ref[2 * s]) * LOG2E
        m = jnp.maximum(jnp.max(s_buf[u % 2], axis=0, keepdims=True), sink)
        p_buf[u % 2] = jnp.exp2(s_buf[u % 2] - m).astype(BF16)
        return jnp.exp2(sink - m)

    def att_values(u, sink_term):
        j, s = units[u]
        res = _dot(vt_s[:, j * BLOCK:(j + 3) * BLOCK], p_buf[u % 2])
        den = res[0:1, :] + sink_term
        ot_buf[u % 2] = res[BF16_ROWS:] * (1.0 / den)

    def att_store(u):
        j, s = units[u]
        rows = slice(j * BLOCK, (j + 1) * BLOCK)
        ot = ot_buf[u % 2]
        o0 = ot[:, :BLOCK].T
        o1 = ot[:, BLOCK:].T
        if (2 * s) // ATT_GROUP == 0:
            o = jnp.where(low_o, o0, pltpu.roll(o1, HEAD_DIM, 1))
        else:
            o = jnp.where(low_o, pltpu.roll(o0, HEAD_DIM, 1), o1)
        att_s[rows, s * LANES:(s + 1) * LANES] = o.astype(BF16)

    ret_units = [(j, h) for j in range(nblk) for h in range(RET_HEADS)]

    def ret_products(u):
        j, h = ret_units[u]
        rows = slice(j * BLOCK, (j + 1) * BLOCK)
        cols = slice(h * LANES, (h + 1) * LANES)
        q = qr_ref[0, rows, cols]
        k = kr_ref[0, rows, cols]
        in_buf[u % 2] = (_dot_nt(q, k) * dm_s[h]).astype(BF16)
        st = st_s[h]
        cr_buf[u % 2] = _dot(q, jnp.concatenate([st.astype(BF16), rb_ref[0, j, h]], axis=1))
        zv = (vr_ref[0, rows, cols].astype(F32) * zeta_s[h]).astype(BF16)
        st_s[h] = st * jnp.exp(lf_ref[h:h + 1, :] * float(CHUNK)) + _dot_tn(k, zv)

    def ret_finish(u):
        j, h = ret_units[u]
        rows = slice(j * BLOCK, (j + 1) * BLOCK)
        cols = slice(h * LANES, (h + 1) * LANES)
        cross = cr_buf[u % 2]
        o = _dot(in_buf[u % 2], vr_ref[0, rows, cols])
        o = o + cross[:, :LANES] * xi_s[0, h] + cross[:, LANES:] * xi_s[1, h]
        o = o * lax.rsqrt(jnp.mean(o * o, axis=-1, keepdims=True) + EPS)
        g = gr_ref[0, rows, cols].astype(F32)
        o = (o * rng_ref[:, cols]) * (g * jax.nn.sigmoid(g))
        ret_s[rows, cols] = o.astype(BF16)

    n_units = len(units)
    sink_terms = {}
    for k in range(n_units + 3):
        if k < n_units:
            att_scores(k)
        if 2 <= k < n_units + 2:
            att_values(k - 2, sink_terms.pop(k - 2))
        if k < len(ret_units):
            ret_products(k)
        if 1 <= k <= len(ret_units):
            ret_finish(k - 1)
        if k >= 3:
            att_store(k - 3)
        if 1 <= k <= n_units:
            sink_terms[k - 1] = att_softmax(k - 1)

    for c in range(D_MODEL // MERGE_COLS):
        cols = slice(c * MERGE_COLS, (c + 1) * MERGE_COLS)
        ba = _dot(att_s[...], wba_ref[:, cols])
        br = _dot(ret_s[...], wbr_ref[:, cols])
        g0 = jax.nn.sigmoid(gate_ref[0, :, c * MERGE_COLS:(c + 1) * MERGE_COLS].astype(F32))
        g1 = jax.nn.sigmoid(gate_ref[0, :, D_MODEL + c * MERGE_COLS:D_MODEL + (c + 1) * MERGE_COLS].astype(F32))
        mrg_s[:, cols] = (g0 * ba + g1 * br).astype(BF16)
    o_ref[0] = x_ref[0] + _dot(mrg_s[...], wo_ref[...])


def _mixer(qa, ka, va, qr, kr, vr, gr, gate, rb, x, sink, lf_h, lb_h, rng, wba, wbr, wo):
    b, s, d = x.shape
    t = MIXER_TILE
    nblk = t // BLOCK
    nb_total = s // BLOCK
    tok = lambda width: pl.BlockSpec((1, t, width), lambda bi, i: (bi, i, 0))
    prev = pl.BlockSpec((1, BLOCK, ATT_KV_W), lambda bi, i: (bi, jnp.maximum(i * nblk - 1, 0), 0))
    nxt = pl.BlockSpec((1, BLOCK, ATT_KV_W), lambda bi, i: (bi, jnp.minimum((i + 1) * nblk, nb_total - 1), 0))
    st = pl.BlockSpec((1, nblk, RET_HEADS, RET_HEAD_DIM, RET_HEAD_DIM), lambda bi, i: (bi, i, 0, 0, 0))
    smem = pl.BlockSpec(memory_space=pltpu.SMEM)
    return pl.pallas_call(
        _mixer_kernel,
        grid=(b, s // t),
        in_specs=[tok(ATT_Q_HEADS * LANES), prev, tok(ATT_KV_W), nxt, prev, tok(ATT_KV_W), nxt,
                  tok(RET_WIDTH), tok(RET_WIDTH), tok(RET_WIDTH), tok(RET_WIDTH), tok(2 * D_MODEL),
                  st, tok(d),
                  smem, _const_spec((RET_HEADS, LANES)), _const_spec((RET_HEADS, LANES)),
                  _const_spec((1, RET_WIDTH)),
                  _const_spec((ATT_Q_W, d)), _const_spec((RET_WIDTH, d)), _const_spec((d, d))],
        out_specs=tok(d),
        out_shape=jax.ShapeDtypeStruct((b, s, d), F32),
        scratch_shapes=[
            pltpu.VMEM((t + 2 * BLOCK, ATT_KV_W), BF16),
            pltpu.VMEM((ATT_KV_W + BF16_ROWS, t + 2 * BLOCK), BF16),
            pltpu.VMEM((RET_HEADS, CHUNK, CHUNK), F32),
            pltpu.VMEM((2, RET_HEADS, CHUNK, CHUNK), F32),
            pltpu.VMEM((RET_HEADS, CHUNK, CHUNK), F32),
            pltpu.VMEM((RET_HEADS, RET_HEAD_DIM, RET_HEAD_DIM), F32),
            pltpu.VMEM((3, 3 * BLOCK, BLOCK), F32),
            pltpu.VMEM((2, 3 * BLOCK, 2 * BLOCK), F32),
            pltpu.VMEM((2, 3 * BLOCK, 2 * BLOCK), BF16),
            pltpu.VMEM((2, ATT_KV_W, 2 * BLOCK), F32),
            pltpu.VMEM((2, CHUNK, CHUNK), BF16),
            pltpu.VMEM((2, CHUNK, 2 * RET_HEAD_DIM), F32),
            pltpu.VMEM((t, ATT_Q_W), BF16),
            pltpu.VMEM((t, RET_WIDTH), BF16),
            pltpu.VMEM((t, d), BF16),
        ],
        compiler_params=pltpu.CompilerParams(
            dimension_semantics=("arbitrary", "arbitrary"), vmem_limit_bytes=VMEM_LIMIT),
        name="mixer",
    )(qa, ka, ka, ka, va, va, va, qr, kr, vr, gr, gate, rb, x, sink, lf_h, lb_h, rng, wba, wbr, wo)


def _ffn_kernel(final_norm, xp_ref, x_ref, xn_ref, g_ref, wi_ref, cw_ref, cb_ref, wo_ref, fg_ref,
                o_ref, gu_s):
    t = x_ref.shape[1]
    i = pl.program_id(1)
    n = pl.num_programs(1)
    g = g_ref[...]
    x = x_ref[0]
    hp = jnp.where(i > 0, _rmsnorm(xp_ref[0], g), 0.0)
    hn = jnp.where(i < n - 1, _rmsnorm(xn_ref[0], g), 0.0)
    h = _rmsnorm(x, g).astype(BF16)
    h_ext = jnp.concatenate([hp.astype(BF16), h, hn.astype(BF16)], axis=0)

    def chunk_body(c, carry):
        c0 = pl.multiple_of(c * FF_CHUNK, FF_CHUNK)
        a_ext = _dot(h_ext, wi_ref[:, pl.ds(c0, FF_CHUNK)])
        u = _dot(h, wi_ref[:, pl.ds(D_FF + c0, FF_CHUNK)])
        w = cw_ref[:, pl.ds(c0, FF_CHUNK)]
        lo = BF16_ROWS
        a = (cb_ref[:, pl.ds(c0, FF_CHUNK)]
             + a_ext[lo - 1:lo - 1 + t] * w[0:1]
             + a_ext[lo:lo + t] * w[1:2]
             + a_ext[lo + 1:lo + 1 + t] * w[2:3])
        gelu = 0.5 * a * (1.0 + lax.erf(a * (2.0 ** -0.5)))
        gu_s[:, pl.ds(c0, FF_CHUNK)] = (gelu * u).astype(BF16)
        return carry

    lax.fori_loop(0, D_FF // FF_CHUNK, chunk_body, 0, unroll=True)
    y = x + _dot(gu_s[...], wo_ref[...])
    if final_norm:
        y = _rmsnorm(y, fg_ref[...])
    o_ref[0] = y


def _ffn(x, g, wi, cw, cb, wo, fg, final_norm):
    b, s, d = x.shape
    t = min(FFN_TILE, s)
    hb = t // BF16_ROWS
    nh = s // BF16_ROWS
    tok = pl.BlockSpec((1, t, d), lambda bi, i: (bi, i, 0))
    prev = pl.BlockSpec((1, BF16_ROWS, d), lambda bi, i: (bi, jnp.maximum(i * hb - 1, 0), 0))
    nxt = pl.BlockSpec((1, BF16_ROWS, d), lambda bi, i: (bi, jnp.minimum((i + 1) * hb, nh - 1), 0))
    return pl.pallas_call(
        functools.partial(_ffn_kernel, final_norm),
        grid=(b, s // t),
        in_specs=[prev, tok, nxt, _const_spec((1, d)), _const_spec((d, 2 * D_FF)),
                  _const_spec((CONV_WIDTH, D_FF)), _const_spec((1, D_FF)), _const_spec((D_FF, d)),
                  _const_spec((1, d))],
        out_specs=tok,
        out_shape=jax.ShapeDtypeStruct((b, s, d), F32),
        scratch_shapes=[pltpu.VMEM((t, D_FF), BF16)],
        compiler_params=pltpu.CompilerParams(
            dimension_semantics=("arbitrary", "arbitrary"), vmem_limit_bytes=VMEM_LIMIT),
        name="ffn",
    )(x, x, x, g, wi, cw, cb, wo, fg)


def _rope_tables(s, half, reps):
    freqs = ROPE_THETA ** (-jnp.arange(half, dtype=F32) / half)
    ang = jnp.arange(s, dtype=F32)[:, None] * freqs[None, :]
    cos = jnp.cos(ang)
    sin = jnp.sin(ang)
    cos_t = jnp.tile(jnp.concatenate([cos, cos], axis=-1), (1, reps))
    sin_t = jnp.tile(jnp.concatenate([-sin, sin], axis=-1), (1, reps))
    return cos_t, sin_t


def _layer(x, p, tabs, final_g, final_norm):
    ca, sa, cr, sr = tabs
    qa, ka, va, qr, kr, vr, gr, gate = _inproj(x, p["norm_mix_g"], p["w_in"], ca, sa, cr, sr)
    rb = _states(kr, vr, p["lb_lanes"])
    x = _mixer(qa, ka, va, qr, kr, vr, gr, gate, rb, x, p["sink"], p["lf_heads"], p["lb_heads"],
               p["ret_norm_g"], p["w_branch_attn"], p["w_branch_ret"], p["w_out"])
    return _ffn(x, p["norm_ffn_g"], p["w_ffn_in"], p["conv_w"], p["conv_b"], p["w_ffn_out"],
                final_g, final_norm)


def kernel(x_prompt, x_sample, norm_mix_g, w_in, attn_sink, ret_log_decay_f, ret_log_decay_b, ret_norm_g,
           w_branch_attn, w_branch_ret, w_out, norm_ffn_g, w_ffn_in, conv_w, conv_b, w_ffn_out, final_norm_g):
    depth = w_in.shape[0]
    layers = []
    for l in range(depth):
        lf = ret_log_decay_f[l].astype(F32)
        lb = ret_log_decay_b[l].astype(F32)
        layers.append(dict(
            norm_mix_g=norm_mix_g[l].reshape(1, D_MODEL),
            w_in=w_in[l].astype(BF16),
            sink=attn_sink[l].astype(F32),
            lb_lanes=jnp.repeat(lb, RET_HEAD_DIM).reshape(1, RET_WIDTH),
            lf_heads=jnp.broadcast_to(lf[:, None], (RET_HEADS, LANES)),
            lb_heads=jnp.broadcast_to(lb[:, None], (RET_HEADS, LANES)),
            ret_norm_g=ret_norm_g[l].reshape(1, RET_WIDTH),
            w_branch_attn=w_branch_attn[l].astype(BF16),
            w_branch_ret=w_branch_ret[l].astype(BF16),
            w_out=w_out[l].astype(BF16),
            norm_ffn_g=norm_ffn_g[l].reshape(1, D_MODEL),
            w_ffn_in=w_ffn_in[l].astype(BF16),
            conv_w=conv_w[l],
            conv_b=conv_b[l].reshape(1, D_FF),
            w_ffn_out=w_ffn_out[l].astype(BF16),
        ))
    fg = final_norm_g.reshape(1, D_MODEL)
    outs = []
    for x in (x_prompt, x_sample):
        s = x.shape[1]
        tabs = _rope_tables(s, HEAD_DIM // 2, LANES // HEAD_DIM) + _rope_tables(s, RET_HEAD_DIM // 2, 1)
        for l in range(depth):
            x = _layer(x, layers[l], tabs, fg, l == depth - 1)
        outs.append(x)
    return tuple(outs)
```

```python
import functools

import jax
import jax.numpy as jnp
from jax import lax
from jax.experimental import pallas as pl
from jax.experimental.pallas import tpu as pltpu

D_MODEL = 1024
HEAD_DIM = 64
ATT_Q_HEADS = 8
ATT_KV_HEADS = 2
ATT_GROUP = ATT_Q_HEADS // ATT_KV_HEADS
ATT_Q_W = ATT_Q_HEADS * HEAD_DIM
ATT_KV_W = ATT_KV_HEADS * HEAD_DIM
WINDOW = 128
BLOCK = 128
RET_HEADS = 4
RET_HEAD_DIM = 128
RET_WIDTH = RET_HEADS * RET_HEAD_DIM
CHUNK = 128
D_FF = 2816
CONV_WIDTH = 3
ROPE_THETA = 10000.0
EPS = 1e-6
NEG_INF = -1e30
LOG2E = 1.4426950408889634
IN_WIDTH = ATT_Q_W + 2 * ATT_KV_W + 4 * RET_WIDTH + 2 * D_MODEL

LANES = 128
BF16_ROWS = 16
VMEM_LIMIT = 56 * 1024 * 1024

INPROJ_TILE = 1024
MIXER_TILE = 1024
FFN_TILE = 1024
FF_CHUNK = 256
STATE_CHUNKS = 8
MERGE_COLS = 256

F32 = jnp.float32
BF16 = jnp.bfloat16

_C_QA = 0
_C_KA = _C_QA + ATT_Q_W
_C_VA = _C_KA + ATT_KV_W
_C_QR = _C_VA + ATT_KV_W
_C_KR = _C_QR + RET_WIDTH
_C_VR = _C_KR + RET_WIDTH
_C_GR = _C_VR + RET_WIDTH
_C_GATE = _C_GR + RET_WIDTH


def _rmsnorm(x, g):
    return (x * lax.rsqrt(jnp.mean(x * x, axis=-1, keepdims=True) + EPS)) * g


def _dot(a, b):
    return jnp.dot(a, b, preferred_element_type=F32)


def _dot_nt(a, b):
    return lax.dot_general(a, b, (((1,), (1,)), ((), ())), preferred_element_type=F32)


def _dot_tn(a, b):
    return lax.dot_general(a, b, (((0,), (0,)), ((), ())), preferred_element_type=F32)


def _const_spec(shape):
    nd = len(shape)
    return pl.BlockSpec(shape, lambda *_: (0,) * nd, pipeline_mode=pl.Buffered(1))


def _inproj_kernel(x_ref, g_ref, w_ref, ca_ref, sa_ref, cr_ref, sr_ref,
                   qa_ref, ka_ref, va_ref, qr_ref, kr_ref, vr_ref, gr_ref, gate_ref):
    x = x_ref[0]
    hb = _rmsnorm(x, g_ref[...]).astype(BF16)
    t = x.shape[0]
    lane = lax.broadcasted_iota(jnp.int32, (t, LANES), 1)
    first_half = (lane % HEAD_DIM) < (HEAD_DIM // 2)
    ca, sa, cr, sr = ca_ref[...], sa_ref[...], cr_ref[...], sr_ref[...]

    def mm(c0, n):
        return _dot(hb, w_ref[:, c0:c0 + n])

    def rope_a(y):
        rot = jnp.where(first_half, pltpu.roll(y, LANES - HEAD_DIM // 2, 1),
                        pltpu.roll(y, HEAD_DIM // 2, 1))
        return y * ca + rot * sa

    def rope_r(y):
        return y * cr + pltpu.roll(y, RET_HEAD_DIM // 2, 1) * sr

    def slab(y, s):
        return y[:, s * LANES:(s + 1) * LANES]

    low = lane < HEAD_DIM
    y = mm(_C_QA, ATT_Q_W)
    for s in range(ATT_Q_W // LANES):
        r = rope_a(slab(y, s)) * (HEAD_DIM ** -0.5 * LOG2E)
        r_sw = pltpu.roll(r, HEAD_DIM, 1)
        kv_low = (2 * s) // ATT_GROUP == 0
        h0 = jnp.where(low, r, 0.0) if kv_low else jnp.where(low, 0.0, r_sw)
        h1 = jnp.where(low, r_sw, 0.0) if kv_low else jnp.where(low, 0.0, r)
        qa_ref[0, :, (2 * s) * LANES:(2 * s + 1) * LANES] = h0.astype(BF16)
        qa_ref[0, :, (2 * s + 1) * LANES:(2 * s + 2) * LANES] = h1.astype(BF16)
    y = mm(_C_KA, 2 * ATT_KV_W)
    ka_ref[0] = rope_a(slab(y, 0)).astype(BF16)
    va_ref[0] = slab(y, 1).astype(BF16)
    y = mm(_C_QR, RET_WIDTH)
    for s in range(RET_HEADS):
        qr_ref[0, :, s * LANES:(s + 1) * LANES] = rope_r(slab(y, s)).astype(BF16)
    y = mm(_C_KR, RET_WIDTH)
    for s in range(RET_HEADS):
        kr_ref[0, :, s * LANES:(s + 1) * LANES] = (rope_r(slab(y, s)) * (RET_HEAD_DIM ** -0.5)).astype(BF16)
    vr_ref[0] = mm(_C_VR, RET_WIDTH).astype(BF16)
    gr_ref[0] = mm(_C_GR, RET_WIDTH).astype(BF16)
    for c in range(4):
        n = 2 * D_MODEL // 4
        gate_ref[0, :, c * n:(c + 1) * n] = mm(_C_GATE + c * n, n).astype(BF16)


def _inproj(x, g, w, ca, sa, cr, sr):
    b, s, d = x.shape
    t = min(INPROJ_TILE, s)
    tok = lambda width: pl.BlockSpec((1, t, width), lambda bi, i: (bi, i, 0))
    tab = pl.BlockSpec((t, LANES), lambda bi, i: (i, 0))
    widths = [ATT_Q_HEADS * LANES, ATT_KV_W, ATT_KV_W, RET_WIDTH, RET_WIDTH, RET_WIDTH, RET_WIDTH, 2 * D_MODEL]
    return pl.pallas_call(
        _inproj_kernel,
        grid=(b, s // t),
        in_specs=[tok(d), _const_spec((1, d)), _const_spec((d, IN_WIDTH)), tab, tab, tab, tab],
        out_specs=[tok(wd) for wd in widths],
        out_shape=[jax.ShapeDtypeStruct((b, s, wd), BF16) for wd in widths],
        compiler_params=pltpu.CompilerParams(
            dimension_semantics=("arbitrary", "arbitrary"), vmem_limit_bytes=VMEM_LIMIT),
        name="inproj",
    )(x, g, w, ca, sa, cr, sr)


def _state_kernel(k_ref, v_ref, lb_ref, rb_ref, st_ref):
    @pl.when(pl.program_id(1) == 0)
    def _():
        st_ref[...] = jnp.zeros_like(st_ref)

    row = lax.broadcasted_iota(jnp.int32, (CHUNK, RET_WIDTH), 0).astype(F32)
    lb = lb_ref[...]
    zeta = jnp.exp(lb * row)
    dec = jnp.exp(lb * float(CHUNK))
    for c in reversed(range(k_ref.shape[1] // CHUNK)):
        rows = slice(c * CHUNK, (c + 1) * CHUNK)
        for h in range(RET_HEADS):
            cols = slice(h * RET_HEAD_DIM, (h + 1) * RET_HEAD_DIM)
            st = st_ref[h]
            rb_ref[0, c, h] = st.astype(BF16)
            zv = (v_ref[0, rows, cols].astype(F32) * zeta[:, cols]).astype(BF16)
            st_ref[h] = st * dec[:, cols] + _dot_tn(k_ref[0, rows, cols], zv)


def _states(kr, vr, lb_l):
    b, s, _ = kr.shape
    nc = s // CHUNK
    per_step = min(STATE_CHUNKS, nc)
    ns = nc // per_step
    t = per_step * CHUNK
    bwd = pl.BlockSpec((1, t, RET_WIDTH), lambda bi, i: (bi, ns - 1 - i, 0))
    out_b = pl.BlockSpec((1, per_step, RET_HEADS, RET_HEAD_DIM, RET_HEAD_DIM),
                         lambda bi, i: (bi, ns - 1 - i, 0, 0, 0))
    return pl.pallas_call(
        _state_kernel,
        grid=(b, ns),
        in_specs=[bwd, bwd, _const_spec((1, RET_WIDTH))],
        out_specs=out_b,
        out_shape=jax.ShapeDtypeStruct((b, nc, RET_HEADS, RET_HEAD_DIM, RET_HEAD_DIM), BF16),
        scratch_shapes=[pltpu.VMEM((RET_HEADS, RET_HEAD_DIM, RET_HEAD_DIM), F32)],
        compiler_params=pltpu.CompilerParams(
            dimension_semantics=("arbitrary", "arbitrary"), vmem_limit_bytes=VMEM_LIMIT),
        name="ret_states",
    )(kr, vr, lb_l)


def _mixer_kernel(qa_ref, kp_ref, kc_ref, kn_ref, vp_ref, vc_ref, vn_ref,
                  qr_ref, kr_ref, vr_ref, gr_ref, gate_ref, rb_ref, x_ref,
                  sink_ref, lf_ref, lb_ref, rng_ref, wba_ref, wbr_ref, wo_ref,
                  o_ref,
                  k_s, vt_s, dm_s, xi_s, zeta_s, st_s, cap_s, s_buf, p_buf, ot_buf, in_buf, cr_buf,
                  att_s, ret_s, mrg_s):
    t = x_ref.shape[1]
    nblk = t // BLOCK
    tile = pl.program_id(1)

    k_s[0:BLOCK] = kp_ref[0]
    k_s[BLOCK:BLOCK + t] = kc_ref[0]
    k_s[BLOCK + t:] = kn_ref[0]
    for i in range(nblk + 2):
        src = vp_ref if i == 0 else (vn_ref if i == nblk + 1 else vc_ref)
        r0 = 0 if i in (0, nblk + 1) else (i - 1) * BLOCK
        vt_s[BF16_ROWS:, i * BLOCK:(i + 1) * BLOCK] = src[0, r0:r0 + BLOCK, :].astype(F32).T.astype(BF16)
    ones_row = lax.broadcasted_iota(jnp.int32, (BF16_ROWS, t + 2 * BLOCK), 0) == 0
    vt_s[0:BF16_ROWS, :] = jnp.where(ones_row, 1.0, 0.0).astype(BF16)

    ri = lax.broadcasted_iota(jnp.int32, (CHUNK, CHUNK), 0).astype(F32)
    ci = lax.broadcasted_iota(jnp.int32, (CHUNK, CHUNK), 1).astype(F32)
    diff = ri - ci
    for h in range(RET_HEADS):
        lf = lf_ref[h:h + 1, :]
        lb = lb_ref[h:h + 1, :]
        dm_s[h] = (jnp.where(diff >= 0, jnp.exp(lf * jnp.maximum(diff, 0.0)), 0.0)
                   + jnp.where(diff < 0, jnp.exp(lb * jnp.maximum(-diff, 0.0)), 0.0))
        xi_s[0, h] = jnp.exp(lf * (ri + 1.0))
        xi_s[1, h] = jnp.exp(lb * (float(CHUNK) - ri))
        zeta_s[h] = jnp.exp(lf * (float(CHUNK) - 1.0 - ri))

    @pl.when(tile == 0)
    def _():
        st_s[...] = jnp.zeros_like(st_s)

    kj = lax.broadcasted_iota(jnp.int32, (3 * BLOCK, BLOCK), 0)
    qi = lax.broadcasted_iota(jnp.int32, (3 * BLOCK, BLOCK), 1)
    band = jnp.abs(kj - BLOCK - qi) <= WINDOW
    first = tile == 0
    last = tile == pl.num_programs(1) - 1
    cap_s[0] = jnp.where(band, jnp.inf, NEG_INF)
    cap_s[1] = jnp.where(band & ((kj >= BLOCK) | jnp.logical_not(first)), jnp.inf, NEG_INF)
    cap_s[2] = jnp.where(band & ((kj < 2 * BLOCK) | jnp.logical_not(last)), jnp.inf, NEG_INF)
    head1 = lax.broadcasted_iota(jnp.int32, (1, 2 * BLOCK), 1) >= BLOCK

    n_slab = ATT_Q_HEADS // 2
    units = [(j, s) for j in range(nblk) for s in range(n_slab)]

    def att_scores(u):
        j, s = units[u]
        rows = slice(j * BLOCK, (j + 1) * BLOCK)
        q2 = jnp.concatenate([qa_ref[0, rows, (2 * s) * LANES:(2 * s + 1) * LANES],
                              qa_ref[0, rows, (2 * s + 1) * LANES:(2 * s + 2) * LANES]], axis=0)
        sc = _dot_nt(k_s[j * BLOCK:(j + 3) * BLOCK, :], q2)
        cap = cap_s[1 if j == 0 else (2 if j == nblk - 1 else 0)]
        for cs in (slice(0, BLOCK), slice(BLOCK, 2 * BLOCK)):
            s_buf[u % 2, 0:BLOCK, cs] = jnp.minimum(sc[:BLOCK, cs], cap[:BLOCK])
            s_buf[u % 2, BLOCK:2 * BLOCK, cs] = sc[BLOCK:2 * BLOCK, cs]
            s_buf[u % 2, 2 * BLOCK:, cs] = jnp.minimum(sc[2 * BLOCK:, cs], cap[2 * BLOCK:])

    def att_softmax(u):
        j, s = units[u]
        sink = jnp.where(head1, sink_ref[2 * s + 1], sink_ref[2 * s]) * LOG2E
        m = jnp.maximum(jnp.max(s_buf[u % 2], axis=0, keepdims=True), sink)
        p_buf[u % 2] = jnp.exp2(s_buf[u % 2] - m).astype(BF16)
        return jnp.exp2(sink - m)

    def att_values(u, sink_term):
        j, s = units[u]
        res = _dot(vt_s[:, j * BLOCK:(j + 3) * BLOCK], p_buf[u % 2])
        den = res[0:1, :] + sink_term
        v0 = BF16_ROWS + ((2 * s) // ATT_GROUP) * HEAD_DIM
        ot_buf[u % 2] = res[v0:v0 + HEAD_DIM] * (1.0 / den)

    def att_store(u):
        j, s = units[u]
        rows = slice(j * BLOCK, (j + 1) * BLOCK)
        ot = ot_buf[u % 2]
        o = jnp.concatenate([ot[:, :BLOCK], ot[:, BLOCK:]], axis=0).T
        att_s[rows, s * LANES:(s + 1) * LANES] = o.astype(BF16)

    ret_units = [(j, h) for j in range(nblk) for h in range(RET_HEADS)]

    def ret_products(u):
        j, h = ret_units[u]
        rows = slice(j * BLOCK, (j + 1) * BLOCK)
        cols = slice(h * LANES, (h + 1) * LANES)
        q = qr_ref[0, rows, cols]
        k = kr_ref[0, rows, cols]
        in_buf[u % 2] = (_dot_nt(q, k) * dm_s[h]).astype(BF16)
        st = st_s[h]
        cr_buf[u % 2] = _dot(q, jnp.concatenate([st.astype(BF16), rb_ref[0, j, h]], axis=1))
        zv = (vr_ref[0, rows, cols].astype(F32) * zeta_s[h]).astype(BF16)
        st_s[h] = st * jnp.exp(lf_ref[h:h + 1, :] * float(CHUNK)) + _dot_tn(k, zv)

    def ret_finish(u):
        j, h = ret_units[u]
        rows = slice(j * BLOCK, (j + 1) * BLOCK)
        cols = slice(h * LANES, (h + 1) * LANES)
        cross = cr_buf[u % 2]
        o = _dot(in_buf[u % 2], vr_ref[0, rows, cols])
        o = o + cross[:, :LANES] * xi_s[0, h] + cross[:, LANES:] * xi_s[1, h]
        o = o * lax.rsqrt(jnp.mean(o * o, axis=-1, keepdims=True) + EPS)
        g = gr_ref[0, rows, cols].astype(F32)
        o = (o * rng_ref[:, cols]) * (g * jax.nn.sigmoid(g))
        ret_s[rows, cols] = o.astype(BF16)

    n_units = len(units)
    sink_terms = {}
    for k in range(n_units + 3):
        if k < n_units:
            att_scores(k)
        if 2 <= k < n_units + 2:
            att_values(k - 2, sink_terms.pop(k - 2))
        if k < len(ret_units):
            ret_products(k)
        if 1 <= k <= len(ret_units):
            ret_finish(k - 1)
        if k >= 3:
            att_store(k - 3)
        if 1 <= k <= n_units:
            sink_terms[k - 1] = att_softmax(k - 1)

    for c in range(D_MODEL // MERGE_COLS):
        cols = slice(c * MERGE_COLS, (c + 1) * MERGE_COLS)
        ba = _dot(att_s[...], wba_ref[:, cols])
        br = _dot(ret_s[...], wbr_ref[:, cols])
        g0 = jax.nn.sigmoid(gate_ref[0, :, c * MERGE_COLS:(c + 1) * MERGE_COLS].astype(F32))
        g1 = jax.nn.sigmoid(gate_ref[0, :, D_MODEL + c * MERGE_COLS:D_MODEL + (c + 1) * MERGE_COLS].astype(F32))
        mrg_s[:, cols] = (g0 * ba + g1 * br).astype(BF16)
    o_ref[0] = x_ref[0] + _dot(mrg_s[...], wo_ref[...])


def _mixer(qa, ka, va, qr, kr, vr, gr, gate, rb, x, sink, lf_h, lb_h, rng, wba, wbr, wo):
    b, s, d = x.shape
    t = MIXER_TILE
    nblk = t // BLOCK
    nb_total = s // BLOCK
    tok = lambda width: pl.BlockSpec((1, t, width), lambda bi, i: (bi, i, 0))
    prev = pl.BlockSpec((1, BLOCK, ATT_KV_W), lambda bi, i: (bi, jnp.maximum(i * nblk - 1, 0), 0))
    nxt = pl.BlockSpec((1, BLOCK, ATT_KV_W), lambda bi, i: (bi, jnp.minimum((i + 1) * nblk, nb_total - 1), 0))
    st = pl.BlockSpec((1, nblk, RET_HEADS, RET_HEAD_DIM, RET_HEAD_DIM), lambda bi, i: (bi, i, 0, 0, 0))
    smem = pl.BlockSpec(memory_space=pltpu.SMEM)
    return pl.pallas_call(
        _mixer_kernel,
        grid=(b, s // t),
        in_specs=[tok(ATT_Q_HEADS * LANES), prev, tok(ATT_KV_W), nxt, prev, tok(ATT_KV_W), nxt,
                  tok(RET_WIDTH), tok(RET_WIDTH), tok(RET_WIDTH), tok(RET_WIDTH), tok(2 * D_MODEL),
                  st, tok(d),
                  smem, _const_spec((RET_HEADS, LANES)), _const_spec((RET_HEADS, LANES)),
                  _const_spec((1, RET_WIDTH)),
                  _const_spec((ATT_Q_W, d)), _const_spec((RET_WIDTH, d)), _const_spec((d, d))],
        out_specs=tok(d),
        out_shape=jax.ShapeDtypeStruct((b, s, d), F32),
        scratch_shapes=[
            pltpu.VMEM((t + 2 * BLOCK, ATT_KV_W), BF16),
            pltpu.VMEM((ATT_KV_W + BF16_ROWS, t + 2 * BLOCK), BF16),
            pltpu.VMEM((RET_HEADS, CHUNK, CHUNK), F32),
            pltpu.VMEM((2, RET_HEADS, CHUNK, CHUNK), F32),
            pltpu.VMEM((RET_HEADS, CHUNK, CHUNK), F32),
            pltpu.VMEM((RET_HEADS, RET_HEAD_DIM, RET_HEAD_DIM), F32),
            pltpu.VMEM((3, 3 * BLOCK, BLOCK), F32),
            pltpu.VMEM((2, 3 * BLOCK, 2 * BLOCK), F32),
            pltpu.VMEM((2, 3 * BLOCK, 2 * BLOCK), BF16),
            pltpu.VMEM((2, HEAD_DIM, 2 * BLOCK), F32),
            pltpu.VMEM((2, CHUNK, CHUNK), BF16),
            pltpu.VMEM((2, CHUNK, 2 * RET_HEAD_DIM), F32),
            pltpu.VMEM((t, ATT_Q_W), BF16),
            pltpu.VMEM((t, RET_WIDTH), BF16),
            pltpu.VMEM((t, d), BF16),
        ],
        compiler_params=pltpu.CompilerParams(
            dimension_semantics=("arbitrary", "arbitrary"), vmem_limit_bytes=VMEM_LIMIT),
        name="mixer",
    )(qa, ka, ka, ka, va, va, va, qr, kr, vr, gr, gate, rb, x, sink, lf_h, lb_h, rng, wba, wbr, wo)


def _ffn_kernel(final_norm, xp_ref, x_ref, xn_ref, g_ref, wi_ref, cw_ref, cb_ref, wo_ref, fg_ref,
                o_ref, gu_s):
    t = x_ref.shape[1]
    i = pl.program_id(1)
    n = pl.num_programs(1)
    g = g_ref[...]
    x = x_ref[0]
    hp = jnp.where(i > 0, _rmsnorm(xp_ref[0], g), 0.0)
    hn = jnp.where(i < n - 1, _rmsnorm(xn_ref[0], g), 0.0)
    h = _rmsnorm(x, g).astype(BF16)
    h_ext = jnp.concatenate([hp.astype(BF16), h, hn.astype(BF16)], axis=0)

    def chunk_body(c, carry):
        c0 = pl.multiple_of(c * FF_CHUNK, FF_CHUNK)
        a_ext = _dot(h_ext, wi_ref[:, pl.ds(c0, FF_CHUNK)])
        u = _dot(h, wi_ref[:, pl.ds(D_FF + c0, FF_CHUNK)])
        w = cw_ref[:, pl.ds(c0, FF_CHUNK)]
        lo = BF16_ROWS
        a = (cb_ref[:, pl.ds(c0, FF_CHUNK)]
             + a_ext[lo - 1:lo - 1 + t] * w[0:1]
             + a_ext[lo:lo + t] * w[1:2]
             + a_ext[lo + 1:lo + 1 + t] * w[2:3])
        gelu = 0.5 * a * (1.0 + lax.erf(a * (2.0 ** -0.5)))
        gu_s[:, pl.ds(c0, FF_CHUNK)] = (gelu * u).astype(BF16)
        return carry

    lax.fori_loop(0, D_FF // FF_CHUNK, chunk_body, 0, unroll=True)
    y = x + _dot(gu_s[...], wo_ref[...])
    if final_norm:
        y = _rmsnorm(y, fg_ref[...])
    o_ref[0] = y


def _ffn(x, g, wi, cw, cb, wo, fg, final_norm):
    b, s, d = x.shape
    t = min(FFN_TILE, s)
    hb = t // BF16_ROWS
    nh = s // BF16_ROWS
    tok = pl.BlockSpec((1, t, d), lambda bi, i: (bi, i, 0))
    prev = pl.BlockSpec((1, BF16_ROWS, d), lambda bi, i: (bi, jnp.maximum(i * hb - 1, 0), 0))
    nxt = pl.BlockSpec((1, BF16_ROWS, d), lambda bi, i: (bi, jnp.minimum((i + 1) * hb, nh - 1), 0))
    return pl.pallas_call(
        functools.partial(_ffn_kernel, final_norm),
        grid=(b, s // t),
        in_specs=[prev, tok, nxt, _const_spec((1, d)), _const_spec((d, 2 * D_FF)),
                  _const_spec((CONV_WIDTH, D_FF)), _const_spec((1, D_FF)), _const_spec((D_FF, d)),
                  _const_spec((1, d))],
        out_specs=tok,
        out_shape=jax.ShapeDtypeStruct((b, s, d), F32),
        scratch_shapes=[pltpu.VMEM((t, D_FF), BF16)],
        compiler_params=pltpu.CompilerParams(
            dimension_semantics=("arbitrary", "arbitrary"), vmem_limit_bytes=VMEM_LIMIT),
        name="ffn",
    )(x, x, x, g, wi, cw, cb, wo, fg)


def _rope_tables(s, half, reps):
    freqs = ROPE_THETA ** (-jnp.arange(half, dtype=F32) / half)
    ang = jnp.arange(s, dtype=F32)[:, None] * freqs[None, :]
    cos = jnp.cos(ang)
    sin = jnp.sin(ang)
    cos_t = jnp.tile(jnp.concatenate([cos, cos], axis=-1), (1, reps))
    sin_t = jnp.tile(jnp.concatenate([-sin, sin], axis=-1), (1, reps))
    return cos_t, sin_t


def _layer(x, p, tabs, final_g, final_norm):
    ca, sa, cr, sr = tabs
    qa, ka, va, qr, kr, vr, gr, gate = _inproj(x, p["norm_mix_g"], p["w_in"], ca, sa, cr, sr)
    rb = _states(kr, vr, p["lb_lanes"])
    x = _mixer(qa, ka, va, qr, kr, vr, gr, gate, rb, x, p["sink"], p["lf_heads"], p["lb_heads"],
               p["ret_norm_g"], p["w_branch_attn"], p["w_branch_ret"], p["w_out"])
    return _ffn(x, p["norm_ffn_g"], p["w_ffn_in"], p["conv_w"], p["conv_b"], p["w_ffn_out"],
                final_g, final_norm)


def kernel(x_prompt, x_sample, norm_mix_g, w_in, attn_sink, ret_log_decay_f, ret_log_decay_b, ret_norm_g,
           w_branch_attn, w_branch_ret, w_out, norm_ffn_g, w_ffn_in, conv_w, conv_b, w_ffn_out, final_norm_g):
    depth = w_in.shape[0]
    layers = []
    for l in range(depth):
        lf = ret_log_decay_f[l].astype(F32)
        lb = ret_log_decay_b[l].astype(F32)
        layers.append(dict(
            norm_mix_g=norm_mix_g[l].reshape(1, D_MODEL),
            w_in=w_in[l].astype(BF16),
            sink=attn_sink[l].astype(F32),
            lb_lanes=jnp.repeat(lb, RET_HEAD_DIM).reshape(1, RET_WIDTH),
            lf_heads=jnp.broadcast_to(lf[:, None], (RET_HEADS, LANES)),
            lb_heads=jnp.broadcast_to(lb[:, None], (RET_HEADS, LANES)),
            ret_norm_g=ret_norm_g[l].reshape(1, RET_WIDTH),
            w_branch_attn=w_branch_attn[l].astype(BF16),
            w_branch_ret=w_branch_ret[l].astype(BF16),
            w_out=w_out[l].astype(BF16),
            norm_ffn_g=norm_ffn_g[l].reshape(1, D_MODEL),
            w_ffn_in=w_ffn_in[l].astype(BF16),
            conv_w=conv_w[l],
            conv_b=conv_b[l].reshape(1, D_FF),
            w_ffn_out=w_ffn_out[l].astype(BF16),
        ))
    fg = final_norm_g.reshape(1, D_MODEL)
    outs = []
    for x in (x_prompt, x_sample):
        s = x.shape[1]
        tabs = _rope_tables(s, HEAD_DIM // 2, LANES // HEAD_DIM) + _rope_tables(s, RET_HEAD_DIM // 2, 1)
        for l in range(depth):
            x = _layer(x, layers[l], tabs, fg, l == depth - 1)
        outs.append(x)
    return tuple(outs)
```

```python
import functools

import jax
import jax.numpy as jnp
from jax import lax
from jax.experimental import pallas as pl
from jax.experimental.pallas import tpu as pltpu

D_MODEL = 1024
HEAD_DIM = 64
ATT_Q_HEADS = 8
ATT_KV_HEADS = 2
ATT_GROUP = ATT_Q_HEADS // ATT_KV_HEADS
ATT_Q_W = ATT_Q_HEADS * HEAD_DIM
ATT_KV_W = ATT_KV_HEADS * HEAD_DIM
WINDOW = 128
BLOCK = 128
RET_HEADS = 4
RET_HEAD_DIM = 128
RET_WIDTH = RET_HEADS * RET_HEAD_DIM
CHUNK = 128
D_FF = 2816
CONV_WIDTH = 3
ROPE_THETA = 10000.0
EPS = 1e-6
NEG_INF = -1e30
LOG2E = 1.4426950408889634
IN_WIDTH = ATT_Q_W + 2 * ATT_KV_W + 4 * RET_WIDTH + 2 * D_MODEL

LANES = 128
BF16_ROWS = 16
VMEM_LIMIT = 56 * 1024 * 1024

INPROJ_TILE = 1024
MIXER_TILE = 1024
FFN_TILE = 1024
FF_CHUNK = 256
MERGE_COLS = 256

F32 = jnp.float32
BF16 = jnp.bfloat16

_C_QA = 0
_C_KA = _C_QA + ATT_Q_W
_C_VA = _C_KA + ATT_KV_W
_C_QR = _C_VA + ATT_KV_W
_C_KR = _C_QR + RET_WIDTH
_C_VR = _C_KR + RET_WIDTH
_C_GR = _C_VR + RET_WIDTH
_C_GATE = _C_GR + RET_WIDTH
_M_VR = _C_KR
_M_GR = _M_VR + RET_WIDTH
_M_GATE = _M_GR + RET_WIDTH
MAIN_WIDTH = IN_WIDTH - RET_WIDTH


def _rmsnorm(x, g):
    return (x * lax.rsqrt(jnp.mean(x * x, axis=-1, keepdims=True) + EPS)) * g


def _dot(a, b):
    return jnp.dot(a, b, preferred_element_type=F32)


def _dot_nt(a, b):
    return lax.dot_general(a, b, (((1,), (1,)), ((), ())), preferred_element_type=F32)


def _const_spec(shape):
    nd = len(shape)
    return pl.BlockSpec(shape, lambda *_: (0,) * nd, pipeline_mode=pl.Buffered(1))


def _inproj_kernel(x_ref, g_ref, w_ref, wkt_ref, ca_ref, sa_ref, cr_ref, sr_ref, crt_ref, srt_ref, lb_ref,
                   qa_ref, ka_ref, va_ref, qr_ref, krt_ref, vr_ref, gr_ref, gate_ref, rb_ref, st_ref):
    x = x_ref[0]
    hb = _rmsnorm(x, g_ref[...]).astype(BF16)
    t = x.shape[0]
    lane = lax.broadcasted_iota(jnp.int32, (t, LANES), 1)
    first_half = (lane % HEAD_DIM) < (HEAD_DIM // 2)
    ca, sa, cr, sr = ca_ref[...], sa_ref[...], cr_ref[...], sr_ref[...]

    def mm(c0, n):
        return _dot(hb, w_ref[:, c0:c0 + n])

    def rope_a(y):
        rot = jnp.where(first_half, pltpu.roll(y, LANES - HEAD_DIM // 2, 1),
                        pltpu.roll(y, HEAD_DIM // 2, 1))
        return y * ca + rot * sa

    def rope_r(y):
        return y * cr + pltpu.roll(y, RET_HEAD_DIM // 2, 1) * sr

    def slab(y, s):
        return y[:, s * LANES:(s + 1) * LANES]

    low = lane < HEAD_DIM
    y = mm(_C_QA, ATT_Q_W)
    for s in range(ATT_Q_W // LANES):
        r = rope_a(slab(y, s)) * (HEAD_DIM ** -0.5 * LOG2E)
        r_sw = pltpu.roll(r, HEAD_DIM, 1)
        kv_low = (2 * s) // ATT_GROUP == 0
        h0 = jnp.where(low, r, 0.0) if kv_low else jnp.where(low, 0.0, r_sw)
        h1 = jnp.where(low, r_sw, 0.0) if kv_low else jnp.where(low, 0.0, r)
        qa_ref[0, :, (2 * s) * LANES:(2 * s + 1) * LANES] = h0.astype(BF16)
        qa_ref[0, :, (2 * s + 1) * LANES:(2 * s + 2) * LANES] = h1.astype(BF16)
    y = mm(_C_KA, 2 * ATT_KV_W)
    ka_ref[0] = rope_a(slab(y, 0)).astype(BF16)
    va_ref[0] = slab(y, 1).astype(BF16)
    y = mm(_C_QR, RET_WIDTH)
    for s in range(RET_HEADS):
        qr_ref[0, :, s * LANES:(s + 1) * LANES] = rope_r(slab(y, s)).astype(BF16)

    yt = _dot_nt(wkt_ref[...], hb)
    crt, srt = crt_ref[...], srt_ref[...]
    half = RET_HEAD_DIM // 2
    for h in range(RET_HEADS):
        x1 = yt[h * RET_HEAD_DIM:h * RET_HEAD_DIM + half]
        x2 = yt[h * RET_HEAD_DIM + half:(h + 1) * RET_HEAD_DIM]
        krt_ref[0, h * RET_HEAD_DIM:h * RET_HEAD_DIM + half, :] = (
            (x1 * crt - x2 * srt) * (RET_HEAD_DIM ** -0.5)).astype(BF16)
        krt_ref[0, h * RET_HEAD_DIM + half:(h + 1) * RET_HEAD_DIM, :] = (
            (x2 * crt + x1 * srt) * (RET_HEAD_DIM ** -0.5)).astype(BF16)

    vr_ref[0] = mm(_M_VR, RET_WIDTH).astype(BF16)
    gr_ref[0] = mm(_M_GR, RET_WIDTH).astype(BF16)
    for c in range(4):
        n = 2 * D_MODEL // 4
        gate_ref[0, :, c * n:(c + 1) * n] = mm(_M_GATE + c * n, n).astype(BF16)

    @pl.when(pl.program_id(1) == 0)
    def _():
        st_ref[...] = jnp.zeros_like(st_ref)

    row = lax.broadcasted_iota(jnp.int32, (CHUNK, RET_WIDTH), 0).astype(F32)
    lb = lb_ref[...]
    zeta = jnp.exp(lb * row)
    dec = jnp.exp(lb * float(CHUNK))
    for c in reversed(range(t // CHUNK)):
        toks = slice(c * CHUNK, (c + 1) * CHUNK)
        for h in range(RET_HEADS):
            cols = slice(h * RET_HEAD_DIM, (h + 1) * RET_HEAD_DIM)
            st = st_ref[h]
            rb_ref[0, c, h] = st.astype(BF16)
            zv = (vr_ref[0, toks, cols].astype(F32) * zeta[:, cols]).astype(BF16)
            st_ref[h] = st * dec[:, cols] + _dot(krt_ref[0, cols, toks], zv)


def _inproj(x, g, w, wkt, ca, sa, cr, sr, crt, srt, lb_l):
    b, s, d = x.shape
    t = min(INPROJ_TILE, s)
    n = s // t
    nck = t // CHUNK
    tok = lambda width: pl.BlockSpec((1, t, width), lambda bi, i: (bi, n - 1 - i, 0))
    tab = pl.BlockSpec((t, LANES), lambda bi, i: (n - 1 - i, 0))
    tab_t = pl.BlockSpec((RET_HEAD_DIM // 2, t), lambda bi, i: (0, n - 1 - i))
    widths = [ATT_Q_HEADS * LANES, ATT_KV_W, ATT_KV_W, RET_WIDTH, None, RET_WIDTH, RET_WIDTH, 2 * D_MODEL]
    out_specs = [tok(wd) if wd else pl.BlockSpec((1, RET_WIDTH, t), lambda bi, i: (bi, 0, n - 1 - i))
                 for wd in widths]
    out_shape = [jax.ShapeDtypeStruct((b, s, wd) if wd else (b, RET_WIDTH, s), BF16) for wd in widths]
    out_specs.append(pl.BlockSpec((1, nck, RET_HEADS, RET_HEAD_DIM, RET_HEAD_DIM),
                                  lambda bi, i: (bi, n - 1 - i, 0, 0, 0)))
    out_shape.append(jax.ShapeDtypeStruct((b, s // CHUNK, RET_HEADS, RET_HEAD_DIM, RET_HEAD_DIM), BF16))
    return pl.pallas_call(
        _inproj_kernel,
        grid=(b, n),
        in_specs=[tok(d), _const_spec((1, d)), _const_spec((d, MAIN_WIDTH)), _const_spec((RET_WIDTH, d)),
                  tab, tab, tab, tab, tab_t, tab_t, _const_spec((1, RET_WIDTH))],
        out_specs=out_specs,
        out_shape=out_shape,
        scratch_shapes=[pltpu.VMEM((RET_HEADS, RET_HEAD_DIM, RET_HEAD_DIM), F32)],
        compiler_params=pltpu.CompilerParams(
            dimension_semantics=("arbitrary", "arbitrary"), vmem_limit_bytes=VMEM_LIMIT),
        name="inproj",
    )(x, g, w, wkt, ca, sa, cr, sr, crt, srt, lb_l)


def _mixer_kernel(qa_ref, kp_ref, kc_ref, kn_ref, vp_ref, vc_ref, vn_ref,
                  qr_ref, krt_ref, vr_ref, gr_ref, gate_ref, rb_ref, x_ref,
                  sink_ref, lf_ref, lb_ref, rng_ref, wba_ref, wbr_ref, wo_ref,
                  o_ref,
                  k_s, vt_s, dm_s, xi_s, zeta_s, st_s, cap_s, s_buf, p_buf, ot_buf, in_buf, cr_buf,
                  att_s, ret_s, mrg_s):
    t = x_ref.shape[1]
    nblk = t // BLOCK
    tile = pl.program_id(1)

    k_s[0:BLOCK] = kp_ref[0]
    k_s[BLOCK:BLOCK + t] = kc_ref[0]
    k_s[BLOCK + t:] = kn_ref[0]
    for i in range(nblk + 2):
        src = vp_ref if i == 0 else (vn_ref if i == nblk + 1 else vc_ref)
        r0 = 0 if i in (0, nblk + 1) else (i - 1) * BLOCK
        vt_s[BF16_ROWS:, i * BLOCK:(i + 1) * BLOCK] = src[0, r0:r0 + BLOCK, :].astype(F32).T.astype(BF16)
    ones_row = lax.broadcasted_iota(jnp.int32, (BF16_ROWS, t + 2 * BLOCK), 0) == 0
    vt_s[0:BF16_ROWS, :] = jnp.where(ones_row, 1.0, 0.0).astype(BF16)

    ri = lax.broadcasted_iota(jnp.int32, (CHUNK, CHUNK), 0).astype(F32)
    ci = lax.broadcasted_iota(jnp.int32, (CHUNK, CHUNK), 1).astype(F32)
    diff = ri - ci
    for h in range(RET_HEADS):
        lf = lf_ref[h:h + 1, :]
        lb = lb_ref[h:h + 1, :]
        dm_s[h] = (jnp.where(diff >= 0, jnp.exp(lf * jnp.maximum(diff, 0.0)), 0.0)
                   + jnp.where(diff < 0, jnp.exp(lb * jnp.maximum(-diff, 0.0)), 0.0))
        xi_s[0, h] = jnp.exp(lf * (ri + 1.0))
        xi_s[1, h] = jnp.exp(lb * (float(CHUNK) - ri))
        zeta_s[h] = jnp.exp(lf * (float(CHUNK) - 1.0 - ri))

    @pl.when(tile == 0)
    def _():
        st_s[...] = jnp.zeros_like(st_s)

    kj = lax.broadcasted_iota(jnp.int32, (3 * BLOCK, BLOCK), 0)
    qi = lax.broadcasted_iota(jnp.int32, (3 * BLOCK, BLOCK), 1)
    band = jnp.abs(kj - BLOCK - qi) <= WINDOW
    first = tile == 0
    last = tile == pl.num_programs(1) - 1
    cap_s[0] = jnp.where(band, jnp.inf, NEG_INF)
    cap_s[1] = jnp.where(band & ((kj >= BLOCK) | jnp.logical_not(first)), jnp.inf, NEG_INF)
    cap_s[2] = jnp.where(band & ((kj < 2 * BLOCK) | jnp.logical_not(last)), jnp.inf, NEG_INF)
    head1 = lax.broadcasted_iota(jnp.int32, (1, 2 * BLOCK), 1) >= BLOCK

    n_slab = ATT_Q_HEADS // 2
    units = [(j, s) for j in range(nblk) for s in range(n_slab)]

    def att_scores(u):
        j, s = units[u]
        rows = slice(j * BLOCK, (j + 1) * BLOCK)
        q2 = jnp.concatenate([qa_ref[0, rows, (2 * s) * LANES:(2 * s + 1) * LANES],
                              qa_ref[0, rows, (2 * s + 1) * LANES:(2 * s + 2) * LANES]], axis=0)
        sc = _dot_nt(k_s[j * BLOCK:(j + 3) * BLOCK, :], q2)
        cap = cap_s[1 if j == 0 else (2 if j == nblk - 1 else 0)]
        for cs in (slice(0, BLOCK), slice(BLOCK, 2 * BLOCK)):
            s_buf[u % 2, 0:BLOCK, cs] = jnp.minimum(sc[:BLOCK, cs], cap[:BLOCK])
            s_buf[u % 2, BLOCK:2 * BLOCK, cs] = sc[BLOCK:2 * BLOCK, cs]
            s_buf[u % 2, 2 * BLOCK:, cs] = jnp.minimum(sc[2 * BLOCK:, cs], cap[2 * BLOCK:])

    def att_softmax(u):
        j, s = units[u]
        sink = jnp.where(head1, sink_ref[2 * s + 1], sink_ref[2 * s]) * LOG2E
        m = jnp.maximum(jnp.max(s_buf[u % 2], axis=0, keepdims=True), sink)
        p_buf[u % 2] = jnp.exp2(s_buf[u % 2] - m).astype(BF16)
        return jnp.exp2(sink - m)

    def att_values(u, sink_term):
        j, s = units[u]
        res = _dot(vt_s[:, j * BLOCK:(j + 3) * BLOCK], p_buf[u % 2])
        den = res[0:1, :] + sink_term
        v0 = BF16_ROWS + ((2 * s) // ATT_GROUP) * HEAD_DIM
        ot_buf[u % 2] = res[v0:v0 + HEAD_DIM] * (1.0 / den)

    def att_store(u):
        j, s = units[u]
        rows = slice(j * BLOCK, (j + 1) * BLOCK)
        ot = ot_buf[u % 2]
        o = jnp.concatenate([ot[:, :BLOCK], ot[:, BLOCK:]], axis=0).T
        att_s[rows, s * LANES:(s + 1) * LANES] = o.astype(BF16)

    ret_units = [(j, h) for j in range(nblk) for h in range(RET_HEADS)]

    def ret_products(u):
        j, h = ret_units[u]
        rows = slice(j * BLOCK, (j + 1) * BLOCK)
        cols = slice(h * LANES, (h + 1) * LANES)
        q = qr_ref[0, rows, cols]
        kt = krt_ref[0, cols, rows]
        in_buf[u % 2] = (_dot(q, kt) * dm_s[h]).astype(BF16)
        st = st_s[h]
        cr_buf[u % 2] = _dot(q, jnp.concatenate([st.astype(BF16), rb_ref[0, j, h]], axis=1))
        zv = (vr_ref[0, rows, cols].astype(F32) * zeta_s[h]).astype(BF16)
        st_s[h] = st * jnp.exp(lf_ref[h:h + 1, :] * float(CHUNK)) + _dot(kt, zv)

    def ret_finish(u):
        j, h = ret_units[u]
        rows = slice(j * BLOCK, (j + 1) * BLOCK)
        cols = slice(h * LANES, (h + 1) * LANES)
        cross = cr_buf[u % 2]
        o = _dot(in_buf[u % 2], vr_ref[0, rows, cols])
        o = o + cross[:, :LANES] * xi_s[0, h] + cross[:, LANES:] * xi_s[1, h]
        o = o * lax.rsqrt(jnp.mean(o * o, axis=-1, keepdims=True) + EPS)
        g = gr_ref[0, rows, cols].astype(F32)
        o = (o * rng_ref[:, cols]) * (g * jax.nn.sigmoid(g))
        ret_s[rows, cols] = o.astype(BF16)

    n_units = len(units)
    sink_terms = {}
    for k in range(n_units + 3):
        if k < n_units:
            att_scores(k)
        if 2 <= k < n_units + 2:
            att_values(k - 2, sink_terms.pop(k - 2))
        if k < len(ret_units):
            ret_products(k)
        if 1 <= k <= len(ret_units):
            ret_finish(k - 1)
        if k >= 3:
            att_store(k - 3)
        if 1 <= k <= n_units:
            sink_terms[k - 1] = att_softmax(k - 1)

    for c in range(D_MODEL // MERGE_COLS):
        cols = slice(c * MERGE_COLS, (c + 1) * MERGE_COLS)
        ba = _dot(att_s[...], wba_ref[:, cols])
        br = _dot(ret_s[...], wbr_ref[:, cols])
        g0 = jax.nn.sigmoid(gate_ref[0, :, c * MERGE_COLS:(c + 1) * MERGE_COLS].astype(F32))
        g1 = jax.nn.sigmoid(gate_ref[0, :, D_MODEL + c * MERGE_COLS:D_MODEL + (c + 1) * MERGE_COLS].astype(F32))
        mrg_s[:, cols] = (g0 * ba + g1 * br).astype(BF16)
    o_ref[0] = x_ref[0] + _dot(mrg_s[...], wo_ref[...])


def _mixer(qa, ka, va, qr, krt, vr, gr, gate, rb, x, sink, lf_h, lb_h, rng, wba, wbr, wo):
    b, s, d = x.shape
    t = min(MIXER_TILE, s)
    nblk = t // BLOCK
    nb_total = s // BLOCK
    tok = lambda width: pl.BlockSpec((1, t, width), lambda bi, i: (bi, i, 0))
    prev = pl.BlockSpec((1, BLOCK, ATT_KV_W), lambda bi, i: (bi, jnp.maximum(i * nblk - 1, 0), 0))
    nxt = pl.BlockSpec((1, BLOCK, ATT_KV_W), lambda bi, i: (bi, jnp.minimum((i + 1) * nblk, nb_total - 1), 0))
    st = pl.BlockSpec((1, nblk, RET_HEADS, RET_HEAD_DIM, RET_HEAD_DIM), lambda bi, i: (bi, i, 0, 0, 0))
    smem = pl.BlockSpec(memory_space=pltpu.SMEM)
    return pl.pallas_call(
        _mixer_kernel,
        grid=(b, s // t),
        in_specs=[tok(ATT_Q_HEADS * LANES), prev, tok(ATT_KV_W), nxt, prev, tok(ATT_KV_W), nxt,
                  tok(RET_WIDTH), pl.BlockSpec((1, RET_WIDTH, t), lambda bi, i: (bi, 0, i)),
                  tok(RET_WIDTH), tok(RET_WIDTH), tok(2 * D_MODEL),
                  st, tok(d),
                  smem, _const_spec((RET_HEADS, LANES)), _const_spec((RET_HEADS, LANES)),
                  _const_spec((1, RET_WIDTH)),
                  _const_spec((ATT_Q_W, d)), _const_spec((RET_WIDTH, d)), _const_spec((d, d))],
        out_specs=tok(d),
        out_shape=jax.ShapeDtypeStruct((b, s, d), F32),
        scratch_shapes=[
            pltpu.VMEM((t + 2 * BLOCK, ATT_KV_W), BF16),
            pltpu.VMEM((ATT_KV_W + BF16_ROWS, t + 2 * BLOCK), BF16),
            pltpu.VMEM((RET_HEADS, CHUNK, CHUNK), F32),
            pltpu.VMEM((2, RET_HEADS, CHUNK, CHUNK), F32),
            pltpu.VMEM((RET_HEADS, CHUNK, CHUNK), F32),
            pltpu.VMEM((RET_HEADS, RET_HEAD_DIM, RET_HEAD_DIM), F32),
            pltpu.VMEM((3, 3 * BLOCK, BLOCK), F32),
            pltpu.VMEM((2, 3 * BLOCK, 2 * BLOCK), F32),
            pltpu.VMEM((2, 3 * BLOCK, 2 * BLOCK), BF16),
            pltpu.VMEM((2, HEAD_DIM, 2 * BLOCK), F32),
            pltpu.VMEM((2, CHUNK, CHUNK), BF16),
            pltpu.VMEM((2, CHUNK, 2 * RET_HEAD_DIM), F32),
            pltpu.VMEM((t, ATT_Q_W), BF16),
            pltpu.VMEM((t, RET_WIDTH), BF16),
            pltpu.VMEM((t, d), BF16),
        ],
        compiler_params=pltpu.CompilerParams(
            dimension_semantics=("arbitrary", "arbitrary"), vmem_limit_bytes=VMEM_LIMIT),
        name="mixer",
    )(qa, ka, ka, ka, va, va, va, qr, krt, vr, gr, gate, rb, x, sink, lf_h, lb_h, rng, wba, wbr, wo)


def _ffn_kernel(final_norm, xp_ref, x_ref, xn_ref, g_ref, wi_ref, cw_ref, cb_ref, wo_ref, fg_ref,
                o_ref, gu_s):
    t = x_ref.shape[1]
    i = pl.program_id(1)
    n = pl.num_programs(1)
    g = g_ref[...]
    x = x_ref[0]
    hp = jnp.where(i > 0, _rmsnorm(xp_ref[0], g), 0.0)
    hn = jnp.where(i < n - 1, _rmsnorm(xn_ref[0], g), 0.0)
    h = _rmsnorm(x, g).astype(BF16)
    h_ext = jnp.concatenate([hp.astype(BF16), h, hn.astype(BF16)], axis=0)

    def chunk_body(c, carry):
        c0 = pl.multiple_of(c * FF_CHUNK, FF_CHUNK)
        a_ext = _dot(h_ext, wi_ref[:, pl.ds(c0, FF_CHUNK)])
        u = _dot(h, wi_ref[:, pl.ds(D_FF + c0, FF_CHUNK)])
        w = cw_ref[:, pl.ds(c0, FF_CHUNK)]
        lo = BF16_ROWS
        a = (cb_ref[:, pl.ds(c0, FF_CHUNK)]
             + a_ext[lo - 1:lo - 1 + t] * w[0:1]
             + a_ext[lo:lo + t] * w[1:2]
             + a_ext[lo + 1:lo + 1 + t] * w[2:3])
        gelu = 0.5 * a * (1.0 + lax.erf(a * (2.0 ** -0.5)))
        gu_s[:, pl.ds(c0, FF_CHUNK)] = (gelu * u).astype(BF16)
        return carry

    lax.fori_loop(0, D_FF // FF_CHUNK, chunk_body, 0, unroll=True)
    y = x + _dot(gu_s[...], wo_ref[...])
    if final_norm:
        y = _rmsnorm(y, fg_ref[...])
    o_ref[0] = y


def _ffn(x, g, wi, cw, cb, wo, fg, final_norm):
    b, s, d = x.shape
    t = min(FFN_TILE, s)
    hb = t // BF16_ROWS
    nh = s // BF16_ROWS
    tok = pl.BlockSpec((1, t, d), lambda bi, i: (bi, i, 0))
    prev = pl.BlockSpec((1, BF16_ROWS, d), lambda bi, i: (bi, jnp.maximum(i * hb - 1, 0), 0))
    nxt = pl.BlockSpec((1, BF16_ROWS, d), lambda bi, i: (bi, jnp.minimum((i + 1) * hb, nh - 1), 0))
    return pl.pallas_call(
        functools.partial(_ffn_kernel, final_norm),
        grid=(b, s // t),
        in_specs=[prev, tok, nxt, _const_spec((1, d)), _const_spec((d, 2 * D_FF)),
                  _const_spec((CONV_WIDTH, D_FF)), _const_spec((1, D_FF)), _const_spec((D_FF, d)),
                  _const_spec((1, d))],
        out_specs=tok,
        out_shape=jax.ShapeDtypeStruct((b, s, d), F32),
        scratch_shapes=[pltpu.VMEM((t, D_FF), BF16)],
        compiler_params=pltpu.CompilerParams(
            dimension_semantics=("arbitrary", "arbitrary"), vmem_limit_bytes=VMEM_LIMIT),
        name="ffn",
    )(x, x, x, g, wi, cw, cb, wo, fg)


def _rope_tables(s, half, reps, transposed=False):
    freqs = ROPE_THETA ** (-jnp.arange(half, dtype=F32) / half)
    ang = jnp.arange(s, dtype=F32)[:, None] * freqs[None, :]
    cos = jnp.cos(ang)
    sin = jnp.sin(ang)
    if transposed:
        return cos.T, sin.T
    cos_t = jnp.tile(jnp.concatenate([cos, cos], axis=-1), (1, reps))
    sin_t = jnp.tile(jnp.concatenate([-sin, sin], axis=-1), (1, reps))
    return cos_t, sin_t


def _layer(x, p, tabs, final_g, final_norm):
    ca, sa, cr, sr, crt, srt = tabs
    qa, ka, va, qr, krt, vr, gr, gate, rb = _inproj(x, p["norm_mix_g"], p["w_main"], p["w_kr_t"],
                                                    ca, sa, cr, sr, crt, srt, p["lb_lanes"])
    x = _mixer(qa, ka, va, qr, krt, vr, gr, gate, rb, x, p["sink"], p["lf_heads"], p["lb_heads"],
               p["ret_norm_g"], p["w_branch_attn"], p["w_branch_ret"], p["w_out"])
    return _ffn(x, p["norm_ffn_g"], p["w_ffn_in"], p["conv_w"], p["conv_b"], p["w_ffn_out"],
                final_g, final_norm)


def kernel(x_prompt, x_sample, norm_mix_g, w_in, attn_sink, ret_log_decay_f, ret_log_decay_b, ret_norm_g,
           w_branch_attn, w_branch_ret, w_out, norm_ffn_g, w_ffn_in, conv_w, conv_b, w_ffn_out, final_norm_g):
    depth = w_in.shape[0]
    layers = []
    for l in range(depth):
        lf = ret_log_decay_f[l].astype(F32)
        lb = ret_log_decay_b[l].astype(F32)
        layers.append(dict(
            norm_mix_g=norm_mix_g[l].reshape(1, D_MODEL),
            w_main=jnp.concatenate([w_in[l, :, :_C_KR], w_in[l, :, _C_VR:]], axis=1).astype(BF16),
            w_kr_t=w_in[l, :, _C_KR:_C_VR].T.astype(BF16),
            sink=attn_sink[l].astype(F32),
            lb_lanes=jnp.repeat(lb, RET_HEAD_DIM).reshape(1, RET_WIDTH),
            lf_heads=jnp.broadcast_to(lf[:, None], (RET_HEADS, LANES)),
            lb_heads=jnp.broadcast_to(lb[:, None], (RET_HEADS, LANES)),
            ret_norm_g=ret_norm_g[l].reshape(1, RET_WIDTH),
            w_branch_attn=w_branch_attn[l].astype(BF16),
            w_branch_ret=w_branch_ret[l].astype(BF16),
            w_out=w_out[l].astype(BF16),
            norm_ffn_g=norm_ffn_g[l].reshape(1, D_MODEL),
            w_ffn_in=w_ffn_in[l].astype(BF16),
            conv_w=conv_w[l],
            conv_b=conv_b[l].reshape(1, D_FF),
            w_ffn_out=w_ffn_out[l].astype(BF16),
        ))
    fg = final_norm_g.reshape(1, D_MODEL)
    outs = []
    for x in (x_prompt, x_sample):
        s = x.shape[1]
        tabs = (_rope_tables(s, HEAD_DIM // 2, LANES // HEAD_DIM) + _rope_tables(s, RET_HEAD_DIM // 2, 1)
                + _rope_tables(s, RET_HEAD_DIM // 2, 1, transposed=True))
        for l in range(depth):
            x = _layer(x, layers[l], tabs, fg, l == depth - 1)
        outs.append(x)
    return tuple(outs)
```

```python
import functools

import jax
import jax.numpy as jnp
from jax import lax
from jax.experimental import pallas as pl
from jax.experimental.pallas import tpu as pltpu

D_MODEL = 1024
HEAD_DIM = 64
ATT_Q_HEADS = 8
ATT_KV_HEADS = 2
ATT_GROUP = ATT_Q_HEADS // ATT_KV_HEADS
ATT_Q_W = ATT_Q_HEADS * HEAD_DIM
ATT_KV_W = ATT_KV_HEADS * HEAD_DIM
WINDOW = 128
BLOCK = 128
RET_HEADS = 4
RET_HEAD_DIM = 128
RET_WIDTH = RET_HEADS * RET_HEAD_DIM
CHUNK = 128
D_FF = 2816
CONV_WIDTH = 3
ROPE_THETA = 10000.0
EPS = 1e-6
NEG_INF = -1e30
LOG2E = 1.4426950408889634
IN_WIDTH = ATT_Q_W + 2 * ATT_KV_W + 4 * RET_WIDTH + 2 * D_MODEL

LANES = 128
BF16_ROWS = 16
VMEM_LIMIT = 56 * 1024 * 1024

INPROJ_TILE = 1024
MIXER_TILE = 1024
FFN_TILE = 1024
FF_CHUNK = 256
MERGE_COLS = 256

F32 = jnp.float32
BF16 = jnp.bfloat16

_C_QA = 0
_C_KA = _C_QA + ATT_Q_W
_C_VA = _C_KA + ATT_KV_W
_C_QR = _C_VA + ATT_KV_W
_C_KR = _C_QR + RET_WIDTH
_C_VR = _C_KR + RET_WIDTH
_C_GR = _C_VR + RET_WIDTH
_C_GATE = _C_GR + RET_WIDTH


def _rmsnorm(x, g):
    return (x * lax.rsqrt(jnp.mean(x * x, axis=-1, keepdims=True) + EPS)) * g


def _dot(a, b):
    return jnp.dot(a, b, preferred_element_type=F32)


def _dot_nt(a, b):
    return lax.dot_general(a, b, (((1,), (1,)), ((), ())), preferred_element_type=F32)


def _const_spec(shape):
    nd = len(shape)
    return pl.BlockSpec(shape, lambda *_: (0,) * nd, pipeline_mode=pl.Buffered(1))


def _inproj_kernel(x_ref, g_ref, w_ref, wkt_ref, ca_ref, sa_ref, cr_ref, sr_ref, crt_ref, srt_ref, lb_ref,
                   qa_ref, ka_ref, va_ref, qr_ref, krt_ref, vr_ref, gr_ref, gate_ref, rb_ref, st_ref):
    x = x_ref[0]
    hb = _rmsnorm(x, g_ref[...]).astype(BF16)
    t = x.shape[0]
    lane = lax.broadcasted_iota(jnp.int32, (t, LANES), 1)
    first_half = (lane % HEAD_DIM) < (HEAD_DIM // 2)
    ca, sa, cr, sr = ca_ref[...], sa_ref[...], cr_ref[...], sr_ref[...]

    def mm(c0, n):
        return _dot(hb, w_ref[:, c0:c0 + n])

    def rope_a(y):
        rot = jnp.where(first_half, pltpu.roll(y, LANES - HEAD_DIM // 2, 1),
                        pltpu.roll(y, HEAD_DIM // 2, 1))
        return y * ca + rot * sa

    def rope_r(y):
        return y * cr + pltpu.roll(y, RET_HEAD_DIM // 2, 1) * sr

    def slab(y, s):
        return y[:, s * LANES:(s + 1) * LANES]

    low = lane < HEAD_DIM
    y = mm(_C_QA, ATT_Q_W)
    for s in range(ATT_Q_W // LANES):
        r = rope_a(slab(y, s)) * (HEAD_DIM ** -0.5 * LOG2E)
        r_sw = pltpu.roll(r, HEAD_DIM, 1)
        kv_low = (2 * s) // ATT_GROUP == 0
        h0 = jnp.where(low, r, 0.0) if kv_low else jnp.where(low, 0.0, r_sw)
        h1 = jnp.where(low, r_sw, 0.0) if kv_low else jnp.where(low, 0.0, r)
        qa_ref[0, :, (2 * s) * LANES:(2 * s + 1) * LANES] = h0.astype(BF16)
        qa_ref[0, :, (2 * s + 1) * LANES:(2 * s + 2) * LANES] = h1.astype(BF16)
    y = mm(_C_KA, 2 * ATT_KV_W)
    ka_ref[0] = rope_a(slab(y, 0)).astype(BF16)
    va_ref[0] = slab(y, 1).astype(BF16)
    y = mm(_C_QR, RET_WIDTH)
    for s in range(RET_HEADS):
        qr_ref[0, :, s * LANES:(s + 1) * LANES] = rope_r(slab(y, s)).astype(BF16)

    yt = _dot_nt(wkt_ref[...], hb)
    crt, srt = crt_ref[...], srt_ref[...]
    half = RET_HEAD_DIM // 2
    for h in range(RET_HEADS):
        x1 = yt[h * RET_HEAD_DIM:h * RET_HEAD_DIM + half]
        x2 = yt[h * RET_HEAD_DIM + half:(h + 1) * RET_HEAD_DIM]
        krt_ref[0, h * RET_HEAD_DIM:h * RET_HEAD_DIM + half, :] = (
            (x1 * crt - x2 * srt) * (RET_HEAD_DIM ** -0.5)).astype(BF16)
        krt_ref[0, h * RET_HEAD_DIM + half:(h + 1) * RET_HEAD_DIM, :] = (
            (x2 * crt + x1 * srt) * (RET_HEAD_DIM ** -0.5)).astype(BF16)

    vr_ref[0] = mm(_C_VR, RET_WIDTH).astype(BF16)
    gr_ref[0] = mm(_C_GR, RET_WIDTH).astype(BF16)
    for c in range(4):
        n = 2 * D_MODEL // 4
        gate_ref[0, :, c * n:(c + 1) * n] = mm(_C_GATE + c * n, n).astype(BF16)

    @pl.when(pl.program_id(1) == 0)
    def _():
        st_ref[...] = jnp.zeros_like(st_ref)

    row = lax.broadcasted_iota(jnp.int32, (CHUNK, RET_WIDTH), 0).astype(F32)
    lb = lb_ref[...]
    zeta = jnp.exp(lb * row)
    dec = jnp.exp(lb * float(CHUNK))
    for c in reversed(range(t // CHUNK)):
        toks = slice(c * CHUNK, (c + 1) * CHUNK)
        for h in range(RET_HEADS):
            cols = slice(h * RET_HEAD_DIM, (h + 1) * RET_HEAD_DIM)
            st = st_ref[h]
            rb_ref[0, c, h] = st.astype(BF16)
            zv = (vr_ref[0, toks, cols].astype(F32) * zeta[:, cols]).astype(BF16)
            st_ref[h] = st * dec[:, cols] + _dot(krt_ref[0, cols, toks], zv)


def _inproj(x, g, w, wkt, ca, sa, cr, sr, crt, srt, lb_l):
    b, s, d = x.shape
    t = min(INPROJ_TILE, s)
    n = s // t
    nck = t // CHUNK
    tok = lambda width: pl.BlockSpec((1, t, width), lambda bi, i: (bi, n - 1 - i, 0))
    tab = pl.BlockSpec((t, LANES), lambda bi, i: (n - 1 - i, 0))
    tab_t = pl.BlockSpec((RET_HEAD_DIM // 2, t), lambda bi, i: (0, n - 1 - i))
    widths = [ATT_Q_HEADS * LANES, ATT_KV_W, ATT_KV_W, RET_WIDTH, None, RET_WIDTH, RET_WIDTH, 2 * D_MODEL]
    out_specs = [tok(wd) if wd else pl.BlockSpec((1, RET_WIDTH, t), lambda bi, i: (bi, 0, n - 1 - i))
                 for wd in widths]
    out_shape = [jax.ShapeDtypeStruct((b, s, wd) if wd else (b, RET_WIDTH, s), BF16) for wd in widths]
    out_specs.append(pl.BlockSpec((1, nck, RET_HEADS, RET_HEAD_DIM, RET_HEAD_DIM),
                                  lambda bi, i: (bi, n - 1 - i, 0, 0, 0)))
    out_shape.append(jax.ShapeDtypeStruct((b, s // CHUNK, RET_HEADS, RET_HEAD_DIM, RET_HEAD_DIM), BF16))
    return pl.pallas_call(
        _inproj_kernel,
        grid=(b, n),
        in_specs=[tok(d), _const_spec((1, d)), _const_spec((d, IN_WIDTH)), _const_spec((RET_WIDTH, d)),
                  tab, tab, tab, tab, tab_t, tab_t, _const_spec((1, RET_WIDTH))],
        out_specs=out_specs,
        out_shape=out_shape,
        scratch_shapes=[pltpu.VMEM((RET_HEADS, RET_HEAD_DIM, RET_HEAD_DIM), F32)],
        compiler_params=pltpu.CompilerParams(
            dimension_semantics=("arbitrary", "arbitrary"), vmem_limit_bytes=VMEM_LIMIT),
        name="inproj",
    )(x, g, w, wkt, ca, sa, cr, sr, crt, srt, lb_l)


def _mixer_kernel(qa_ref, kp_ref, kc_ref, kn_ref, vp_ref, vc_ref, vn_ref,
                  qr_ref, krt_ref, vr_ref, gr_ref, gate_ref, rb_ref, x_ref,
                  sink_ref, lf_ref, lb_ref, rng_ref, wba_ref, wbr_ref, wo_ref,
                  o_ref,
                  k_s, vt_s, dm_s, xi_s, zeta_s, st_s, cap_s, s_buf, p_buf, ot_buf, in_buf, cr_buf,
                  att_s, ret_s, mrg_s):
    t = x_ref.shape[1]
    nblk = t // BLOCK
    tile = pl.program_id(1)

    k_s[0:BLOCK] = kp_ref[0]
    k_s[BLOCK:BLOCK + t] = kc_ref[0]
    k_s[BLOCK + t:] = kn_ref[0]
    for i in range(nblk + 2):
        src = vp_ref if i == 0 else (vn_ref if i == nblk + 1 else vc_ref)
        r0 = 0 if i in (0, nblk + 1) else (i - 1) * BLOCK
        vt_s[BF16_ROWS:, i * BLOCK:(i + 1) * BLOCK] = src[0, r0:r0 + BLOCK, :].astype(F32).T.astype(BF16)
    ones_row = lax.broadcasted_iota(jnp.int32, (BF16_ROWS, t + 2 * BLOCK), 0) == 0
    vt_s[0:BF16_ROWS, :] = jnp.where(ones_row, 1.0, 0.0).astype(BF16)

    ri = lax.broadcasted_iota(jnp.int32, (CHUNK, CHUNK), 0).astype(F32)
    ci = lax.broadcasted_iota(jnp.int32, (CHUNK, CHUNK), 1).astype(F32)
    diff = ri - ci
    for h in range(RET_HEADS):
        lf = lf_ref[h:h + 1, :]
        lb = lb_ref[h:h + 1, :]
        dm_s[h] = (jnp.where(diff >= 0, jnp.exp(lf * jnp.maximum(diff, 0.0)), 0.0)
                   + jnp.where(diff < 0, jnp.exp(lb * jnp.maximum(-diff, 0.0)), 0.0))
        xi_s[0, h] = jnp.exp(lf * (ri + 1.0))
        xi_s[1, h] = jnp.exp(lb * (float(CHUNK) - ri))
        zeta_s[h] = jnp.exp(lf * (float(CHUNK) - 1.0 - ri))

    @pl.when(tile == 0)
    def _():
        st_s[...] = jnp.zeros_like(st_s)

    kj = lax.broadcasted_iota(jnp.int32, (3 * BLOCK, BLOCK), 0)
    qi = lax.broadcasted_iota(jnp.int32, (3 * BLOCK, BLOCK), 1)
    band = jnp.abs(kj - BLOCK - qi) <= WINDOW
    first = tile == 0
    last = tile == pl.num_programs(1) - 1
    cap_s[0] = jnp.where(band, jnp.inf, NEG_INF)
    cap_s[1] = jnp.where(band & ((kj >= BLOCK) | jnp.logical_not(first)), jnp.inf, NEG_INF)
    cap_s[2] = jnp.where(band & ((kj < 2 * BLOCK) | jnp.logical_not(last)), jnp.inf, NEG_INF)
    head1 = lax.broadcasted_iota(jnp.int32, (1, 2 * BLOCK), 1) >= BLOCK

    n_slab = ATT_Q_HEADS // 2
    units = [(j, s) for j in range(nblk) for s in range(n_slab)]

    def att_scores(u):
        j, s = units[u]
        rows = slice(j * BLOCK, (j + 1) * BLOCK)
        q2 = jnp.concatenate([qa_ref[0, rows, (2 * s) * LANES:(2 * s + 1) * LANES],
                              qa_ref[0, rows, (2 * s + 1) * LANES:(2 * s + 2) * LANES]], axis=0)
        sc = _dot_nt(k_s[j * BLOCK:(j + 3) * BLOCK, :], q2)
        cap = cap_s[1 if j == 0 else (2 if j == nblk - 1 else 0)]
        for cs in (slice(0, BLOCK), slice(BLOCK, 2 * BLOCK)):
            s_buf[u % 2, 0:BLOCK, cs] = jnp.minimum(sc[:BLOCK, cs], cap[:BLOCK])
            s_buf[u % 2, BLOCK:2 * BLOCK, cs] = sc[BLOCK:2 * BLOCK, cs]
            s_buf[u % 2, 2 * BLOCK:, cs] = jnp.minimum(sc[2 * BLOCK:, cs], cap[2 * BLOCK:])

    def att_softmax(u):
        j, s = units[u]
        sink = jnp.where(head1, sink_ref[2 * s + 1], sink_ref[2 * s]) * LOG2E
        m = jnp.maximum(jnp.max(s_buf[u % 2], axis=0, keepdims=True), sink)
        p_buf[u % 2] = jnp.exp2(s_buf[u % 2] - m).astype(BF16)
        return jnp.exp2(sink - m)

    def att_values(u, sink_term):
        j, s = units[u]
        res = _dot(vt_s[:, j * BLOCK:(j + 3) * BLOCK], p_buf[u % 2])
        den = res[0:1, :] + sink_term
        v0 = BF16_ROWS + ((2 * s) // ATT_GROUP) * HEAD_DIM
        ot_buf[u % 2] = res[v0:v0 + HEAD_DIM] * (1.0 / den)

    def att_store(u):
        j, s = units[u]
        rows = slice(j * BLOCK, (j + 1) * BLOCK)
        ot = ot_buf[u % 2]
        o = jnp.concatenate([ot[:, :BLOCK], ot[:, BLOCK:]], axis=0).T
        att_s[rows, s * LANES:(s + 1) * LANES] = o.astype(BF16)

    ret_units = [(j, h) for j in range(nblk) for h in range(RET_HEADS)]

    def ret_products(u):
        j, h = ret_units[u]
        rows = slice(j * BLOCK, (j + 1) * BLOCK)
        cols = slice(h * LANES, (h + 1) * LANES)
        q = qr_ref[0, rows, cols]
        kt = krt_ref[0, cols, rows]
        in_buf[u % 2] = (_dot(q, kt) * dm_s[h]).astype(BF16)
        st = st_s[h]
        cr_buf[u % 2] = _dot(q, jnp.concatenate([st.astype(BF16), rb_ref[0, j, h]], axis=1))
        zv = (vr_ref[0, rows, cols].astype(F32) * zeta_s[h]).astype(BF16)
        st_s[h] = st * jnp.exp(lf_ref[h:h + 1, :] * float(CHUNK)) + _dot(kt, zv)

    def ret_finish(u):
        j, h = ret_units[u]
        rows = slice(j * BLOCK, (j + 1) * BLOCK)
        cols = slice(h * LANES, (h + 1) * LANES)
        cross = cr_buf[u % 2]
        o = _dot(in_buf[u % 2], vr_ref[0, rows, cols])
        o = o + cross[:, :LANES] * xi_s[0, h] + cross[:, LANES:] * xi_s[1, h]
        o = o * lax.rsqrt(jnp.mean(o * o, axis=-1, keepdims=True) + EPS)
        g = gr_ref[0, rows, cols].astype(F32)
        o = (o * rng_ref[:, cols]) * (g * jax.nn.sigmoid(g))
        ret_s[rows, cols] = o.astype(BF16)

    n_units = len(units)
    sink_terms = {}
    for k in range(n_units + 3):
        if k < n_units:
            att_scores(k)
        if 2 <= k < n_units + 2:
            att_values(k - 2, sink_terms.pop(k - 2))
        if k < len(ret_units):
            ret_products(k)
        if 1 <= k <= len(ret_units):
            ret_finish(k - 1)
        if k >= 3:
            att_store(k - 3)
        if 1 <= k <= n_units:
            sink_terms[k - 1] = att_softmax(k - 1)

    for c in range(D_MODEL // MERGE_COLS):
        cols = slice(c * MERGE_COLS, (c + 1) * MERGE_COLS)
        ba = _dot(att_s[...], wba_ref[:, cols])
        br = _dot(ret_s[...], wbr_ref[:, cols])
        g0 = jax.nn.sigmoid(gate_ref[0, :, c * MERGE_COLS:(c + 1) * MERGE_COLS].astype(F32))
        g1 = jax.nn.sigmoid(gate_ref[0, :, D_MODEL + c * MERGE_COLS:D_MODEL + (c + 1) * MERGE_COLS].astype(F32))
        mrg_s[:, cols] = (g0 * ba + g1 * br).astype(BF16)
    o_ref[0] = x_ref[0] + _dot(mrg_s[...], wo_ref[...])


def _mixer(qa, ka, va, qr, krt, vr, gr, gate, rb, x, sink, lf_h, lb_h, rng, wba, wbr, wo):
    b, s, d = x.shape
    t = min(MIXER_TILE, s)
    nblk = t // BLOCK
    nb_total = s // BLOCK
    tok = lambda width: pl.BlockSpec((1, t, width), lambda bi, i: (bi, i, 0))
    prev = pl.BlockSpec((1, BLOCK, ATT_KV_W), lambda bi, i: (bi, jnp.maximum(i * nblk - 1, 0), 0))
    nxt = pl.BlockSpec((1, BLOCK, ATT_KV_W), lambda bi, i: (bi, jnp.minimum((i + 1) * nblk, nb_total - 1), 0))
    st = pl.BlockSpec((1, nblk, RET_HEADS, RET_HEAD_DIM, RET_HEAD_DIM), lambda bi, i: (bi, i, 0, 0, 0))
    smem = pl.BlockSpec(memory_space=pltpu.SMEM)
    return pl.pallas_call(
        _mixer_kernel,
        grid=(b, s // t),
        in_specs=[tok(ATT_Q_HEADS * LANES), prev, tok(ATT_KV_W), nxt, prev, tok(ATT_KV_W), nxt,
                  tok(RET_WIDTH), pl.BlockSpec((1, RET_WIDTH, t), lambda bi, i: (bi, 0, i)),
                  tok(RET_WIDTH), tok(RET_WIDTH), tok(2 * D_MODEL),
                  st, tok(d),
                  smem, _const_spec((RET_HEADS, LANES)), _const_spec((RET_HEADS, LANES)),
                  _const_spec((1, RET_WIDTH)),
                  _const_spec((ATT_Q_W, d)), _const_spec((RET_WIDTH, d)), _const_spec((d, d))],
        out_specs=tok(d),
        out_shape=jax.ShapeDtypeStruct((b, s, d), F32),
        scratch_shapes=[
            pltpu.VMEM((t + 2 * BLOCK, ATT_KV_W), BF16),
            pltpu.VMEM((ATT_KV_W + BF16_ROWS, t + 2 * BLOCK), BF16),
            pltpu.VMEM((RET_HEADS, CHUNK, CHUNK), F32),
            pltpu.VMEM((2, RET_HEADS, CHUNK, CHUNK), F32),
            pltpu.VMEM((RET_HEADS, CHUNK, CHUNK), F32),
            pltpu.VMEM((RET_HEADS, RET_HEAD_DIM, RET_HEAD_DIM), F32),
            pltpu.VMEM((3, 3 * BLOCK, BLOCK), F32),
            pltpu.VMEM((2, 3 * BLOCK, 2 * BLOCK), F32),
            pltpu.VMEM((2, 3 * BLOCK, 2 * BLOCK), BF16),
            pltpu.VMEM((2, HEAD_DIM, 2 * BLOCK), F32),
            pltpu.VMEM((2, CHUNK, CHUNK), BF16),
            pltpu.VMEM((2, CHUNK, 2 * RET_HEAD_DIM), F32),
            pltpu.VMEM((t, ATT_Q_W), BF16),
            pltpu.VMEM((t, RET_WIDTH), BF16),
            pltpu.VMEM((t, d), BF16),
        ],
        compiler_params=pltpu.CompilerParams(
            dimension_semantics=("arbitrary", "arbitrary"), vmem_limit_bytes=VMEM_LIMIT),
        name="mixer",
    )(qa, ka, ka, ka, va, va, va, qr, krt, vr, gr, gate, rb, x, sink, lf_h, lb_h, rng, wba, wbr, wo)


def _ffn_kernel(final_norm, xp_ref, x_ref, xn_ref, g_ref, wi_ref, cw_ref, cb_ref, wo_ref, fg_ref,
                o_ref, gu_s):
    t = x_ref.shape[1]
    i = pl.program_id(1)
    n = pl.num_programs(1)
    g = g_ref[...]
    x = x_ref[0]
    hp = jnp.where(i > 0, _rmsnorm(xp_ref[0], g), 0.0)
    hn = jnp.where(i < n - 1, _rmsnorm(xn_ref[0], g), 0.0)
    h = _rmsnorm(x, g).astype(BF16)
    h_ext = jnp.concatenate([hp.astype(BF16), h, hn.astype(BF16)], axis=0)

    def chunk_body(c, carry):
        c0 = pl.multiple_of(c * FF_CHUNK, FF_CHUNK)
        a_ext = _dot(h_ext, wi_ref[:, pl.ds(c0, FF_CHUNK)])
        u = _dot(h, wi_ref[:, pl.ds(D_FF + c0, FF_CHUNK)])
        w = cw_ref[:, pl.ds(c0, FF_CHUNK)]
        lo = BF16_ROWS
        a = (cb_ref[:, pl.ds(c0, FF_CHUNK)]
             + a_ext[lo - 1:lo - 1 + t] * w[0:1]
             + a_ext[lo:lo + t] * w[1:2]
             + a_ext[lo + 1:lo + 1 + t] * w[2:3])
        gelu = 0.5 * a * (1.0 + lax.erf(a * (2.0 ** -0.5)))
        gu_s[:, pl.ds(c0, FF_CHUNK)] = (gelu * u).astype(BF16)
        return carry

    lax.fori_loop(0, D_FF // FF_CHUNK, chunk_body, 0, unroll=True)
    y = x + _dot(gu_s[...], wo_ref[...])
    if final_norm:
        y = _rmsnorm(y, fg_ref[...])
    o_ref[0] = y


def _ffn(x, g, wi, cw, cb, wo, fg, final_norm):
    b, s, d = x.shape
    t = min(FFN_TILE, s)
    hb = t // BF16_ROWS
    nh = s // BF16_ROWS
    tok = pl.BlockSpec((1, t, d), lambda bi, i: (bi, i, 0))
    prev = pl.BlockSpec((1, BF16_ROWS, d), lambda bi, i: (bi, jnp.maximum(i * hb - 1, 0), 0))
    nxt = pl.BlockSpec((1, BF16_ROWS, d), lambda bi, i: (bi, jnp.minimum((i + 1) * hb, nh - 1), 0))
    return pl.pallas_call(
        functools.partial(_ffn_kernel, final_norm),
        grid=(b, s // t),
        in_specs=[prev, tok, nxt, _const_spec((1, d)), _const_spec((d, 2 * D_FF)),
                  _const_spec((CONV_WIDTH, D_FF)), _const_spec((1, D_FF)), _const_spec((D_FF, d)),
                  _const_spec((1, d))],
        out_specs=tok,
        out_shape=jax.ShapeDtypeStruct((b, s, d), F32),
        scratch_shapes=[pltpu.VMEM((t, D_FF), BF16)],
        compiler_params=pltpu.CompilerParams(
            dimension_semantics=("arbitrary", "arbitrary"), vmem_limit_bytes=VMEM_LIMIT),
        name="ffn",
    )(x, x, x, g, wi, cw, cb, wo, fg)


def _rope_tables(s, half, reps, transposed=False):
    freqs = ROPE_THETA ** (-jnp.arange(half, dtype=F32) / half)
    ang = jnp.arange(s, dtype=F32)[:, None] * freqs[None, :]
    cos = jnp.cos(ang)
    sin = jnp.sin(ang)
    if transposed:
        return cos.T, sin.T
    cos_t = jnp.tile(jnp.concatenate([cos, cos], axis=-1), (1, reps))
    sin_t = jnp.tile(jnp.concatenate([-sin, sin], axis=-1), (1, reps))
    return cos_t, sin_t


def _layer(x, p, tabs, final_g, final_norm):
    ca, sa, cr, sr, crt, srt = tabs
    qa, ka, va, qr, krt, vr, gr, gate, rb = _inproj(x, p["norm_mix_g"], p["w_in"], p["w_kr_t"],
                                                    ca, sa, cr, sr, crt, srt, p["lb_lanes"])
    x = _mixer(qa, ka, va, qr, krt, vr, gr, gate, rb, x, p["sink"], p["lf_heads"], p["lb_heads"],
               p["ret_norm_g"], p["w_branch_attn"], p["w_branch_ret"], p["w_out"])
    return _ffn(x, p["norm_ffn_g"], p["w_ffn_in"], p["conv_w"], p["conv_b"], p["w_ffn_out"],
                final_g, final_norm)


def kernel(x_prompt, x_sample, norm_mix_g, w_in, attn_sink, ret_log_decay_f, ret_log_decay_b, ret_norm_g,
           w_branch_attn, w_branch_ret, w_out, norm_ffn_g, w_ffn_in, conv_w, conv_b, w_ffn_out, final_norm_g):
    depth = w_in.shape[0]
    layers = []
    for l in range(depth):
        lf = ret_log_decay_f[l].astype(F32)
        lb = ret_log_decay_b[l].astype(F32)
        layers.append(dict(
            norm_mix_g=norm_mix_g[l].reshape(1, D_MODEL),
            w_in=w_in[l].astype(BF16),
            w_kr_t=w_in[l, :, _C_KR:_C_VR].T.astype(BF16),
            sink=attn_sink[l].astype(F32),
            lb_lanes=jnp.repeat(lb, RET_HEAD_DIM).reshape(1, RET_WIDTH),
            lf_heads=jnp.broadcast_to(lf[:, None], (RET_HEADS, LANES)),
            lb_heads=jnp.broadcast_to(lb[:, None], (RET_HEADS, LANES)),
            ret_norm_g=ret_norm_g[l].reshape(1, RET_WIDTH),
            w_branch_attn=w_branch_attn[l].astype(BF16),
            w_branch_ret=w_branch_ret[l].astype(BF16),
            w_out=w_out[l].astype(BF16),
            norm_ffn_g=norm_ffn_g[l].reshape(1, D_MODEL),
            w_ffn_in=w_ffn_in[l].astype(BF16),
            conv_w=conv_w[l],
            conv_b=conv_b[l].reshape(1, D_FF),
            w_ffn_out=w_ffn_out[l].astype(BF16),
        ))
    fg = final_norm_g.reshape(1, D_MODEL)
    outs = []
    for x in (x_prompt, x_sample):
        s = x.shape[1]
        tabs = (_rope_tables(s, HEAD_DIM // 2, LANES // HEAD_DIM) + _rope_tables(s, RET_HEAD_DIM // 2, 1)
                + _rope_tables(s, RET_HEAD_DIM // 2, 1, transposed=True))
        for l in range(depth):
            x = _layer(x, layers[l], tabs, fg, l == depth - 1)
        outs.append(x)
    return tuple(outs)
```

```python
import functools

import jax
import jax.numpy as jnp
from jax import lax
from jax.experimental import pallas as pl
from jax.experimental.pallas import tpu as pltpu

D_MODEL = 1024
HEAD_DIM = 64
ATT_Q_HEADS = 8
ATT_KV_HEADS = 2
ATT_GROUP = ATT_Q_HEADS // ATT_KV_HEADS
ATT_Q_W = ATT_Q_HEADS * HEAD_DIM
ATT_KV_W = ATT_KV_HEADS * HEAD_DIM
WINDOW = 128
BLOCK = 128
RET_HEADS = 4
RET_HEAD_DIM = 128
RET_WIDTH = RET_HEADS * RET_HEAD_DIM
CHUNK = 128
D_FF = 2816
CONV_WIDTH = 3
ROPE_THETA = 10000.0
EPS = 1e-6
NEG_INF = -1e30
LOG2E = 1.4426950408889634
IN_WIDTH = ATT_Q_W + 2 * ATT_KV_W + 4 * RET_WIDTH + 2 * D_MODEL

LANES = 128
BF16_ROWS = 16
VMEM_LIMIT = 56 * 1024 * 1024

INPROJ_TILE = 1024
MIXER_TILE = 1024
FFN_TILE = 1024
FF_CHUNK = 256
STATE_CHUNKS = 8
MERGE_COLS = 256

F32 = jnp.float32
BF16 = jnp.bfloat16

_C_QA = 0
_C_KA = _C_QA + ATT_Q_W
_C_VA = _C_KA + ATT_KV_W
_C_QR = _C_VA + ATT_KV_W
_C_KR = _C_QR + RET_WIDTH
_C_VR = _C_KR + RET_WIDTH
_C_GR = _C_VR + RET_WIDTH
_C_GATE = _C_GR + RET_WIDTH


def _rmsnorm(x, g):
    return (x * lax.rsqrt(jnp.mean(x * x, axis=-1, keepdims=True) + EPS)) * g


def _dot(a, b):
    return jnp.dot(a, b, preferred_element_type=F32)


def _dot_nt(a, b):
    return lax.dot_general(a, b, (((1,), (1,)), ((), ())), preferred_element_type=F32)


def _dot_tn(a, b):
    return lax.dot_general(a, b, (((0,), (0,)), ((), ())), preferred_element_type=F32)


def _const_spec(shape):
    nd = len(shape)
    return pl.BlockSpec(shape, lambda *_: (0,) * nd, pipeline_mode=pl.Buffered(1))


def _inproj_kernel(x_ref, g_ref, w_ref, ca_ref, sa_ref, cr_ref, sr_ref,
                   qa_ref, ka_ref, va_ref, qr_ref, kr_ref, vr_ref, gr_ref, gate_ref):
    x = x_ref[0]
    hb = _rmsnorm(x, g_ref[...]).astype(BF16)
    t = x.shape[0]
    lane = lax.broadcasted_iota(jnp.int32, (t, LANES), 1)
    first_half = (lane % HEAD_DIM) < (HEAD_DIM // 2)
    ca, sa, cr, sr = ca_ref[...], sa_ref[...], cr_ref[...], sr_ref[...]

    def mm(c0, n):
        return _dot(hb, w_ref[:, c0:c0 + n])

    def rope_a(y):
        rot = jnp.where(first_half, pltpu.roll(y, LANES - HEAD_DIM // 2, 1),
                        pltpu.roll(y, HEAD_DIM // 2, 1))
        return y * ca + rot * sa

    def rope_r(y):
        return y * cr + pltpu.roll(y, RET_HEAD_DIM // 2, 1) * sr

    def slab(y, s):
        return y[:, s * LANES:(s + 1) * LANES]

    low = lane < HEAD_DIM
    y = mm(_C_QA, ATT_Q_W)
    for s in range(ATT_Q_W // LANES):
        r = rope_a(slab(y, s)) * (HEAD_DIM ** -0.5 * LOG2E)
        r_sw = pltpu.roll(r, HEAD_DIM, 1)
        kv_low = (2 * s) // ATT_GROUP == 0
        h0 = jnp.where(low, r, 0.0) if kv_low else jnp.where(low, 0.0, r_sw)
        h1 = jnp.where(low, r_sw, 0.0) if kv_low else jnp.where(low, 0.0, r)
        qa_ref[0, :, (2 * s) * LANES:(2 * s + 1) * LANES] = h0.astype(BF16)
        qa_ref[0, :, (2 * s + 1) * LANES:(2 * s + 2) * LANES] = h1.astype(BF16)
    y = mm(_C_KA, 2 * ATT_KV_W)
    ka_ref[0] = rope_a(slab(y, 0)).astype(BF16)
    va_ref[0] = slab(y, 1).astype(BF16)
    y = mm(_C_QR, RET_WIDTH)
    for s in range(RET_HEADS):
        qr_ref[0, :, s * LANES:(s + 1) * LANES] = rope_r(slab(y, s)).astype(BF16)
    y = mm(_C_KR, RET_WIDTH)
    for s in range(RET_HEADS):
        kr_ref[0, :, s * LANES:(s + 1) * LANES] = (rope_r(slab(y, s)) * (RET_HEAD_DIM ** -0.5)).astype(BF16)
    vr_ref[0] = mm(_C_VR, RET_WIDTH).astype(BF16)
    gr_ref[0] = mm(_C_GR, RET_WIDTH).astype(BF16)
    for c in range(4):
        n = 2 * D_MODEL // 4
        gate_ref[0, :, c * n:(c + 1) * n] = mm(_C_GATE + c * n, n).astype(BF16)


def _inproj(x, g, w, ca, sa, cr, sr):
    b, s, d = x.shape
    t = min(INPROJ_TILE, s)
    tok = lambda width: pl.BlockSpec((1, t, width), lambda bi, i: (bi, i, 0))
    tab = pl.BlockSpec((t, LANES), lambda bi, i: (i, 0))
    widths = [ATT_Q_HEADS * LANES, ATT_KV_W, ATT_KV_W, RET_WIDTH, RET_WIDTH, RET_WIDTH, RET_WIDTH, 2 * D_MODEL]
    return pl.pallas_call(
        _inproj_kernel,
        grid=(b, s // t),
        in_specs=[tok(d), _const_spec((1, d)), _const_spec((d, IN_WIDTH)), tab, tab, tab, tab],
        out_specs=[tok(wd) for wd in widths],
        out_shape=[jax.ShapeDtypeStruct((b, s, wd), BF16) for wd in widths],
        compiler_params=pltpu.CompilerParams(
            dimension_semantics=("arbitrary", "arbitrary"), vmem_limit_bytes=VMEM_LIMIT),
        name="inproj",
    )(x, g, w, ca, sa, cr, sr)


def _state_kernel(k_ref, v_ref, lb_ref, rb_ref, st_ref):
    @pl.when(pl.program_id(1) == 0)
    def _():
        st_ref[...] = jnp.zeros_like(st_ref)

    row = lax.broadcasted_iota(jnp.int32, (CHUNK, RET_WIDTH), 0).astype(F32)
    lb = lb_ref[...]
    zeta = jnp.exp(lb * row)
    dec = jnp.exp(lb * float(CHUNK))
    for c in reversed(range(k_ref.shape[1] // CHUNK)):
        rows = slice(c * CHUNK, (c + 1) * CHUNK)
        for h in range(RET_HEADS):
            cols = slice(h * RET_HEAD_DIM, (h + 1) * RET_HEAD_DIM)
            st = st_ref[h]
            rb_ref[0, c, h] = st.astype(BF16)
            zv = (v_ref[0, rows, cols].astype(F32) * zeta[:, cols]).astype(BF16)
            st_ref[h] = st * dec[:, cols] + _dot_tn(k_ref[0, rows, cols], zv)


def _states(kr, vr, lb_l):
    b, s, _ = kr.shape
    nc = s // CHUNK
    per_step = min(STATE_CHUNKS, nc)
    ns = nc // per_step
    t = per_step * CHUNK
    bwd = pl.BlockSpec((1, t, RET_WIDTH), lambda bi, i: (bi, ns - 1 - i, 0))
    out_b = pl.BlockSpec((1, per_step, RET_HEADS, RET_HEAD_DIM, RET_HEAD_DIM),
                         lambda bi, i: (bi, ns - 1 - i, 0, 0, 0))
    return pl.pallas_call(
        _state_kernel,
        grid=(b, ns),
        in_specs=[bwd, bwd, _const_spec((1, RET_WIDTH))],
        out_specs=out_b,
        out_shape=jax.ShapeDtypeStruct((b, nc, RET_HEADS, RET_HEAD_DIM, RET_HEAD_DIM), BF16),
        scratch_shapes=[pltpu.VMEM((RET_HEADS, RET_HEAD_DIM, RET_HEAD_DIM), F32)],
        compiler_params=pltpu.CompilerParams(
            dimension_semantics=("arbitrary", "arbitrary"), vmem_limit_bytes=VMEM_LIMIT),
        name="ret_states",
    )(kr, vr, lb_l)


def _mixer_kernel(qa_ref, kp_ref, kc_ref, kn_ref, vp_ref, vc_ref, vn_ref,
                  qr_ref, kr_ref, vr_ref, gr_ref, gate_ref, rb_ref, x_ref,
                  sink_ref, lf_ref, lb_ref, rng_ref, wba_ref, wbr_ref, wo_ref,
                  o_ref,
                  k_s, vt_s, dm_s, xi_s, zeta_s, st_s, cap_s, s_buf, p_buf, ot_buf, in_buf, cr_buf,
                  att_s, ret_s, mrg_s):
    t = x_ref.shape[1]
    nblk = t // BLOCK
    tile = pl.program_id(1)

    k_s[0:BLOCK] = kp_ref[0]
    k_s[BLOCK:BLOCK + t] = kc_ref[0]
    k_s[BLOCK + t:] = kn_ref[0]
    for i in range(nblk + 2):
        src = vp_ref if i == 0 else (vn_ref if i == nblk + 1 else vc_ref)
        r0 = 0 if i in (0, nblk + 1) else (i - 1) * BLOCK
        vt_s[BF16_ROWS:, i * BLOCK:(i + 1) * BLOCK] = src[0, r0:r0 + BLOCK, :].astype(F32).T.astype(BF16)
    ones_row = lax.broadcasted_iota(jnp.int32, (BF16_ROWS, t + 2 * BLOCK), 0) == 0
    vt_s[0:BF16_ROWS, :] = jnp.where(ones_row, 1.0, 0.0).astype(BF16)

    ri = lax.broadcasted_iota(jnp.int32, (CHUNK, CHUNK), 0).astype(F32)
    ci = lax.broadcasted_iota(jnp.int32, (CHUNK, CHUNK), 1).astype(F32)
    diff = ri - ci
    for h in range(RET_HEADS):
        lf = lf_ref[h:h + 1, :]
        lb = lb_ref[h:h + 1, :]
        dm_s[h] = (jnp.where(diff >= 0, jnp.exp(lf * jnp.maximum(diff, 0.0)), 0.0)
                   + jnp.where(diff < 0, jnp.exp(lb * jnp.maximum(-diff, 0.0)), 0.0))
        xi_s[0, h] = jnp.exp(lf * (ri + 1.0))
        xi_s[1, h] = jnp.exp(lb * (float(CHUNK) - ri))
        zeta_s[h] = jnp.exp(lf * (float(CHUNK) - 1.0 - ri))

    @pl.when(tile == 0)
    def _():
        st_s[...] = jnp.zeros_like(st_s)

    kj = lax.broadcasted_iota(jnp.int32, (3 * BLOCK, BLOCK), 0)
    qi = lax.broadcasted_iota(jnp.int32, (3 * BLOCK, BLOCK), 1)
    band = jnp.abs(kj - BLOCK - qi) <= WINDOW
    first = tile == 0
    last = tile == pl.num_programs(1) - 1
    cap_s[0] = jnp.where(band, jnp.inf, NEG_INF)
    cap_s[1] = jnp.where(band & ((kj >= BLOCK) | jnp.logical_not(first)), jnp.inf, NEG_INF)
    cap_s[2] = jnp.where(band & ((kj < 2 * BLOCK) | jnp.logical_not(last)), jnp.inf, NEG_INF)
    head1 = lax.broadcasted_iota(jnp.int32, (1, 2 * BLOCK), 1) >= BLOCK

    n_slab = ATT_Q_HEADS // 2
    units = [(j, s) for j in range(nblk) for s in range(n_slab)]

    def att_scores(u):
        j, s = units[u]
        rows = slice(j * BLOCK, (j + 1) * BLOCK)
        q2 = jnp.concatenate([qa_ref[0, rows, (2 * s) * LANES:(2 * s + 1) * LANES],
                              qa_ref[0, rows, (2 * s + 1) * LANES:(2 * s + 2) * LANES]], axis=0)
        sc = _dot_nt(k_s[j * BLOCK:(j + 3) * BLOCK, :], q2)
        cap = cap_s[1 if j == 0 else (2 if j == nblk - 1 else 0)]
        for cs in (slice(0, BLOCK), slice(BLOCK, 2 * BLOCK)):
            s_buf[u % 2, 0:BLOCK, cs] = jnp.minimum(sc[:BLOCK, cs], cap[:BLOCK])
            s_buf[u % 2, BLOCK:2 * BLOCK, cs] = sc[BLOCK:2 * BLOCK, cs]
            s_buf[u % 2, 2 * BLOCK:, cs] = jnp.minimum(sc[2 * BLOCK:, cs], cap[2 * BLOCK:])

    def att_softmax(u):
        j, s = units[u]
        sink = jnp.where(head1, sink_ref[2 * s + 1], sink_ref[2 * s]) * LOG2E
        m = jnp.maximum(jnp.max(s_buf[u % 2], axis=0, keepdims=True), sink)
        p_buf[u % 2] = jnp.exp2(s_buf[u % 2] - m).astype(BF16)
        return jnp.exp2(sink - m)

    def att_values(u, sink_term):
        j, s = units[u]
        res = _dot(vt_s[:, j * BLOCK:(j + 3) * BLOCK], p_buf[u % 2])
        den = res[0:1, :] + sink_term
        v0 = BF16_ROWS + ((2 * s) // ATT_GROUP) * HEAD_DIM
        ot_buf[u % 2] = res[v0:v0 + HEAD_DIM] * (1.0 / den)

    def att_store(u):
        j, s = units[u]
        rows = slice(j * BLOCK, (j + 1) * BLOCK)
        ot = ot_buf[u % 2]
        o = jnp.concatenate([ot[:, :BLOCK], ot[:, BLOCK:]], axis=0).T
        att_s[rows, s * LANES:(s + 1) * LANES] = o.astype(BF16)

    ret_units = [(j, h) for j in range(nblk) for h in range(RET_HEADS)]

    def ret_products(u):
        j, h = ret_units[u]
        rows = slice(j * BLOCK, (j + 1) * BLOCK)
        cols = slice(h * LANES, (h + 1) * LANES)
        q = qr_ref[0, rows, cols]
        k = kr_ref[0, rows, cols]
        in_buf[u % 2] = (_dot_nt(q, k) * dm_s[h]).astype(BF16)
        st = st_s[h]
        cr_buf[u % 2] = _dot(q, jnp.concatenate([st.astype(BF16), rb_ref[0, j, h]], axis=1))
        zv = (vr_ref[0, rows, cols].astype(F32) * zeta_s[h]).astype(BF16)
        st_s[h] = st * jnp.exp(lf_ref[h:h + 1, :] * float(CHUNK)) + _dot_tn(k, zv)

    def ret_finish(u):
        j, h = ret_units[u]
        rows = slice(j * BLOCK, (j + 1) * BLOCK)
        cols = slice(h * LANES, (h + 1) * LANES)
        cross = cr_buf[u % 2]
        o = _dot(in_buf[u % 2], vr_ref[0, rows, cols])
        o = o + cross[:, :LANES] * xi_s[0, h] + cross[:, LANES:] * xi_s[1, h]
        o = o * lax.rsqrt(jnp.mean(o * o, axis=-1, keepdims=True) + EPS)
        g = gr_ref[0, rows, cols].astype(F32)
        o = (o * rng_ref[:, cols]) * (g * jax.nn.sigmoid(g))
        ret_s[rows, cols] = o.astype(BF16)

    n_units = len(units)
    sink_terms = {}
    for k in range(n_units + 3):
        if k < n_units:
            att_scores(k)
        if 2 <= k < n_units + 2:
            att_values(k - 2, sink_terms.pop(k - 2))
        if k < len(ret_units):
            ret_products(k)
        if 1 <= k <= len(ret_units):
            ret_finish(k - 1)
        if k >= 3:
            att_store(k - 3)
        if 1 <= k <= n_units:
            sink_terms[k - 1] = att_softmax(k - 1)

    for c in range(D_MODEL // MERGE_COLS):
        cols = slice(c * MERGE_COLS, (c + 1) * MERGE_COLS)
        ba = _dot(att_s[...], wba_ref[:, cols])
        br = _dot(ret_s[...], wbr_ref[:, cols])
        g0 = jax.nn.sigmoid(gate_ref[0, :, c * MERGE_COLS:(c + 1) * MERGE_COLS].astype(F32))
        g1 = jax.nn.sigmoid(gate_ref[0, :, D_MODEL + c * MERGE_COLS:D_MODEL + (c + 1) * MERGE_COLS].astype(F32))
        mrg_s[:, cols] = (g0 * ba + g1 * br).astype(BF16)
    o_ref[0] = x_ref[0] + _dot(mrg_s[...], wo_ref[...])


def _mixer(qa, ka, va, qr, kr, vr, gr, gate, rb, x, sink, lf_h, lb_h, rng, wba, wbr, wo):
    b, s, d = x.shape
    t = MIXER_TILE
    nblk = t // BLOCK
    nb_total = s // BLOCK
    tok = lambda width: pl.BlockSpec((1, t, width), lambda bi, i: (bi, i, 0))
    prev = pl.BlockSpec((1, BLOCK, ATT_KV_W), lambda bi, i: (bi, jnp.maximum(i * nblk - 1, 0), 0))
    nxt = pl.BlockSpec((1, BLOCK, ATT_KV_W), lambda bi, i: (bi, jnp.minimum((i + 1) * nblk, nb_total - 1), 0))
    st = pl.BlockSpec((1, nblk, RET_HEADS, RET_HEAD_DIM, RET_HEAD_DIM), lambda bi, i: (bi, i, 0, 0, 0))
    smem = pl.BlockSpec(memory_space=pltpu.SMEM)
    return pl.pallas_call(
        _mixer_kernel,
        grid=(b, s // t),
        in_specs=[tok(ATT_Q_HEADS * LANES), prev, tok(ATT_KV_W), nxt, prev, tok(ATT_KV_W), nxt,
                  tok(RET_WIDTH), tok(RET_WIDTH), tok(RET_WIDTH), tok(RET_WIDTH), tok(2 * D_MODEL),
                  st, tok(d),
                  smem, _const_spec((RET_HEADS, LANES)), _const_spec((RET_HEADS, LANES)),
                  _const_spec((1, RET_WIDTH)),
                  _const_spec((ATT_Q_W, d)), _const_spec((RET_WIDTH, d)), _const_spec((d, d))],
        out_specs=tok(d),
        out_shape=jax.ShapeDtypeStruct((b, s, d), F32),
        scratch_shapes=[
            pltpu.VMEM((t + 2 * BLOCK, ATT_KV_W), BF16),
            pltpu.VMEM((ATT_KV_W + BF16_ROWS, t + 2 * BLOCK), BF16),
            pltpu.VMEM((RET_HEADS, CHUNK, CHUNK), F32),
            pltpu.VMEM((2, RET_HEADS, CHUNK, CHUNK), F32),
            pltpu.VMEM((RET_HEADS, CHUNK, CHUNK), F32),
            pltpu.VMEM((RET_HEADS, RET_HEAD_DIM, RET_HEAD_DIM), F32),
            pltpu.VMEM((3, 3 * BLOCK, BLOCK), F32),
            pltpu.VMEM((2, 3 * BLOCK, 2 * BLOCK), F32),
            pltpu.VMEM((2, 3 * BLOCK, 2 * BLOCK), BF16),
            pltpu.VMEM((2, HEAD_DIM, 2 * BLOCK), F32),
            pltpu.VMEM((2, CHUNK, CHUNK), BF16),
            pltpu.VMEM((2, CHUNK, 2 * RET_HEAD_DIM), F32),
            pltpu.VMEM((t, ATT_Q_W), BF16),
            pltpu.VMEM((t, RET_WIDTH), BF16),
            pltpu.VMEM((t, d), BF16),
        ],
        compiler_params=pltpu.CompilerParams(
            dimension_semantics=("arbitrary", "arbitrary"), vmem_limit_bytes=VMEM_LIMIT),
        name="mixer",
    )(qa, ka, ka, ka, va, va, va, qr, kr, vr, gr, gate, rb, x, sink, lf_h, lb_h, rng, wba, wbr, wo)


def _ffn_kernel(final_norm, xp_ref, x_ref, xn_ref, g_ref, wi_ref, cw_ref, cb_ref, wo_ref, fg_ref,
                o_ref, gu_s):
    t = x_ref.shape[1]
    i = pl.program_id(1)
    n = pl.num_programs(1)
    g = g_ref[...]
    x = x_ref[0]
    hp = jnp.where(i > 0, _rmsnorm(xp_ref[0], g), 0.0)
    hn = jnp.where(i < n - 1, _rmsnorm(xn_ref[0], g), 0.0)
    h = _rmsnorm(x, g).astype(BF16)
    h_ext = jnp.concatenate([hp.astype(BF16), h, hn.astype(BF16)], axis=0)

    lo = BF16_ROWS
    for c in range(D_FF // FF_CHUNK):
        cols = slice(c * FF_CHUNK, (c + 1) * FF_CHUNK)
        a_ext = _dot(h_ext, wi_ref[:, cols])
        u = _dot(h, wi_ref[:, D_FF + c * FF_CHUNK:D_FF + (c + 1) * FF_CHUNK])
        w = cw_ref[:, cols]
        a = (cb_ref[:, cols]
             + pltpu.roll(a_ext, 1, 0)[lo:lo + t] * w[0:1]
             + a_ext[lo:lo + t] * w[1:2]
             + pltpu.roll(a_ext, t + 2 * lo - 1, 0)[lo:lo + t] * w[2:3])
        gelu = 0.5 * a * (1.0 + lax.erf(a * (2.0 ** -0.5)))
        gu_s[:, cols] = (gelu * u).astype(BF16)
    y = x + _dot(gu_s[...], wo_ref[...])
    if final_norm:
        y = _rmsnorm(y, fg_ref[...])
    o_ref[0] = y


def _ffn(x, g, wi, cw, cb, wo, fg, final_norm):
    b, s, d = x.shape
    t = min(FFN_TILE, s)
    hb = t // BF16_ROWS
    nh = s // BF16_ROWS
    tok = pl.BlockSpec((1, t, d), lambda bi, i: (bi, i, 0))
    prev = pl.BlockSpec((1, BF16_ROWS, d), lambda bi, i: (bi, jnp.maximum(i * hb - 1, 0), 0))
    nxt = pl.BlockSpec((1, BF16_ROWS, d), lambda bi, i: (bi, jnp.minimum((i + 1) * hb, nh - 1), 0))
    return pl.pallas_call(
        functools.partial(_ffn_kernel, final_norm),
        grid=(b, s // t),
        in_specs=[prev, tok, nxt, _const_spec((1, d)), _const_spec((d, 2 * D_FF)),
                  _const_spec((CONV_WIDTH, D_FF)), _const_spec((1, D_FF)), _const_spec((D_FF, d)),
                  _const_spec((1, d))],
        out_specs=tok,
        out_shape=jax.ShapeDtypeStruct((b, s, d), F32),
        scratch_shapes=[pltpu.VMEM((t, D_FF), BF16)],
        compiler_params=pltpu.CompilerParams(
            dimension_semantics=("arbitrary", "arbitrary"), vmem_limit_bytes=VMEM_LIMIT),
        name="ffn",
    )(x, x, x, g, wi, cw, cb, wo, fg)


def _rope_tables(s, half, reps):
    freqs = ROPE_THETA ** (-jnp.arange(half, dtype=F32) / half)
    ang = jnp.arange(s, dtype=F32)[:, None] * freqs[None, :]
    cos = jnp.cos(ang)
    sin = jnp.sin(ang)
    cos_t = jnp.tile(jnp.concatenate([cos, cos], axis=-1), (1, reps))
    sin_t = jnp.tile(jnp.concatenate([-sin, sin], axis=-1), (1, reps))
    return cos_t, sin_t


def _layer(x, p, tabs, final_g, final_norm):
    ca, sa, cr, sr = tabs
    qa, ka, va, qr, kr, vr, gr, gate = _inproj(x, p["norm_mix_g"], p["w_in"], ca, sa, cr, sr)
    rb = _states(kr, vr, p["lb_lanes"])
    x = _mixer(qa, ka, va, qr, kr, vr, gr, gate, rb, x, p["sink"], p["lf_heads"], p["lb_heads"],
               p["ret_norm_g"], p["w_branch_attn"], p["w_branch_ret"], p["w_out"])
    return _ffn(x, p["norm_ffn_g"], p["w_ffn_in"], p["conv_w"], p["conv_b"], p["w_ffn_out"],
                final_g, final_norm)


def kernel(x_prompt, x_sample, norm_mix_g, w_in, attn_sink, ret_log_decay_f, ret_log_decay_b, ret_norm_g,
           w_branch_attn, w_branch_ret, w_out, norm_ffn_g, w_ffn_in, conv_w, conv_b, w_ffn_out, final_norm_g):
    depth = w_in.shape[0]
    layers = []
    for l in range(depth):
        lf = ret_log_decay_f[l].astype(F32)
        lb = ret_log_decay_b[l].astype(F32)
        layers.append(dict(
            norm_mix_g=norm_mix_g[l].reshape(1, D_MODEL),
            w_in=w_in[l].astype(BF16),
            sink=attn_sink[l].astype(F32),
            lb_lanes=jnp.repeat(lb, RET_HEAD_DIM).reshape(1, RET_WIDTH),
            lf_heads=jnp.broadcast_to(lf[:, None], (RET_HEADS, LANES)),
            lb_heads=jnp.broadcast_to(lb[:, None], (RET_HEADS, LANES)),
            ret_norm_g=ret_norm_g[l].reshape(1, RET_WIDTH),
            w_branch_attn=w_branch_attn[l].astype(BF16),
            w_branch_ret=w_branch_ret[l].astype(BF16),
            w_out=w_out[l].astype(BF16),
            norm_ffn_g=norm_ffn_g[l].reshape(1, D_MODEL),
            w_ffn_in=w_ffn_in[l].astype(BF16),
            conv_w=conv_w[l],
            conv_b=conv_b[l].reshape(1, D_FF),
            w_ffn_out=w_ffn_out[l].astype(BF16),
        ))
    fg = final_norm_g.reshape(1, D_MODEL)
    outs = []
    for x in (x_prompt, x_sample):
        s = x.shape[1]
        tabs = _rope_tables(s, HEAD_DIM // 2, LANES // HEAD_DIM) + _rope_tables(s, RET_HEAD_DIM // 2, 1)
        for l in range(depth):
            x = _layer(x, layers[l], tabs, fg, l == depth - 1)
        outs.append(x)
    return tuple(outs)
```

```python
import functools

import jax
import jax.numpy as jnp
from jax import lax
from jax.experimental import pallas as pl
from jax.experimental.pallas import tpu as pltpu

D_MODEL = 1024
HEAD_DIM = 64
ATT_Q_HEADS = 8
ATT_KV_HEADS = 2
ATT_GROUP = ATT_Q_HEADS // ATT_KV_HEADS
ATT_Q_W = ATT_Q_HEADS * HEAD_DIM
ATT_KV_W = ATT_KV_HEADS * HEAD_DIM
WINDOW = 128
BLOCK = 128
RET_HEADS = 4
RET_HEAD_DIM = 128
RET_WIDTH = RET_HEADS * RET_HEAD_DIM
CHUNK = 128
D_FF = 2816
CONV_WIDTH = 3
ROPE_THETA = 10000.0
EPS = 1e-6
NEG_INF = -1e30
LOG2E = 1.4426950408889634
IN_WIDTH = ATT_Q_W + 2 * ATT_KV_W + 4 * RET_WIDTH + 2 * D_MODEL

LANES = 128
BF16_ROWS = 16
VMEM_LIMIT = 56 * 1024 * 1024

INPROJ_TILE = 1024
MIXER_TILE = 1024
FFN_TILE = 1024
FF_CHUNK = 256
STATE_CHUNKS = 8
MERGE_COLS = 256
OUT_ROWS = 256

F32 = jnp.float32
BF16 = jnp.bfloat16

_C_QA = 0
_C_KA = _C_QA + ATT_Q_W
_C_VA = _C_KA + ATT_KV_W
_C_QR = _C_VA + ATT_KV_W
_C_KR = _C_QR + RET_WIDTH
_C_VR = _C_KR + RET_WIDTH
_C_GR = _C_VR + RET_WIDTH
_C_GATE = _C_GR + RET_WIDTH


def _rmsnorm(x, g):
    return (x * lax.rsqrt(jnp.mean(x * x, axis=-1, keepdims=True) + EPS)) * g


def _dot(a, b):
    return jnp.dot(a, b, preferred_element_type=F32)


def _dot_nt(a, b):
    return lax.dot_general(a, b, (((1,), (1,)), ((), ())), preferred_element_type=F32)


def _dot_tn(a, b):
    return lax.dot_general(a, b, (((0,), (0,)), ((), ())), preferred_element_type=F32)


def _const_spec(shape):
    nd = len(shape)
    return pl.BlockSpec(shape, lambda *_: (0,) * nd, pipeline_mode=pl.Buffered(1))


def _inproj_kernel(x_ref, g_ref, w_ref, ca_ref, sa_ref, cr_ref, sr_ref,
                   qa_ref, ka_ref, va_ref, qr_ref, kr_ref, vr_ref, gr_ref, gate_ref):
    x = x_ref[0]
    hb = _rmsnorm(x, g_ref[...]).astype(BF16)
    t = x.shape[0]
    lane = lax.broadcasted_iota(jnp.int32, (t, LANES), 1)
    first_half = (lane % HEAD_DIM) < (HEAD_DIM // 2)
    ca, sa, cr, sr = ca_ref[...], sa_ref[...], cr_ref[...], sr_ref[...]

    def mm(c0, n):
        return _dot(hb, w_ref[:, c0:c0 + n])

    def rope_a(y):
        rot = jnp.where(first_half, pltpu.roll(y, LANES - HEAD_DIM // 2, 1),
                        pltpu.roll(y, HEAD_DIM // 2, 1))
        return y * ca + rot * sa

    def rope_r(y):
        return y * cr + pltpu.roll(y, RET_HEAD_DIM // 2, 1) * sr

    def slab(y, s):
        return y[:, s * LANES:(s + 1) * LANES]

    low = lane < HEAD_DIM
    y = mm(_C_QA, ATT_Q_W)
    for s in range(ATT_Q_W // LANES):
        r = rope_a(slab(y, s)) * (HEAD_DIM ** -0.5 * LOG2E)
        r_sw = pltpu.roll(r, HEAD_DIM, 1)
        kv_low = (2 * s) // ATT_GROUP == 0
        h0 = jnp.where(low, r, 0.0) if kv_low else jnp.where(low, 0.0, r_sw)
        h1 = jnp.where(low, r_sw, 0.0) if kv_low else jnp.where(low, 0.0, r)
        qa_ref[0, :, (2 * s) * LANES:(2 * s + 1) * LANES] = h0.astype(BF16)
        qa_ref[0, :, (2 * s + 1) * LANES:(2 * s + 2) * LANES] = h1.astype(BF16)
    y = mm(_C_KA, 2 * ATT_KV_W)
    ka_ref[0] = rope_a(slab(y, 0)).astype(BF16)
    va_ref[0] = slab(y, 1).astype(BF16)
    y = mm(_C_QR, RET_WIDTH)
    for s in range(RET_HEADS):
        qr_ref[0, :, s * LANES:(s + 1) * LANES] = rope_r(slab(y, s)).astype(BF16)
    y = mm(_C_KR, RET_WIDTH)
    for s in range(RET_HEADS):
        kr_ref[0, :, s * LANES:(s + 1) * LANES] = (rope_r(slab(y, s)) * (RET_HEAD_DIM ** -0.5)).astype(BF16)
    vr_ref[0] = mm(_C_VR, RET_WIDTH).astype(BF16)
    gr_ref[0] = mm(_C_GR, RET_WIDTH).astype(BF16)
    for c in range(4):
        n = 2 * D_MODEL // 4
        gate_ref[0, :, c * n:(c + 1) * n] = mm(_C_GATE + c * n, n).astype(BF16)


def _inproj(x, g, w, ca, sa, cr, sr):
    b, s, d = x.shape
    t = min(INPROJ_TILE, s)
    tok = lambda width: pl.BlockSpec((1, t, width), lambda bi, i: (bi, i, 0))
    tab = pl.BlockSpec((t, LANES), lambda bi, i: (i, 0))
    widths = [ATT_Q_HEADS * LANES, ATT_KV_W, ATT_KV_W, RET_WIDTH, RET_WIDTH, RET_WIDTH, RET_WIDTH, 2 * D_MODEL]
    return pl.pallas_call(
        _inproj_kernel,
        grid=(b, s // t),
        in_specs=[tok(d), _const_spec((1, d)), _const_spec((d, IN_WIDTH)), tab, tab, tab, tab],
        out_specs=[tok(wd) for wd in widths],
        out_shape=[jax.ShapeDtypeStruct((b, s, wd), BF16) for wd in widths],
        compiler_params=pltpu.CompilerParams(
            dimension_semantics=("arbitrary", "arbitrary"), vmem_limit_bytes=VMEM_LIMIT),
        name="inproj",
    )(x, g, w, ca, sa, cr, sr)


def _state_kernel(k_ref, v_ref, lb_ref, rb_ref, st_ref):
    @pl.when(pl.program_id(1) == 0)
    def _():
        st_ref[...] = jnp.zeros_like(st_ref)

    row = lax.broadcasted_iota(jnp.int32, (CHUNK, RET_WIDTH), 0).astype(F32)
    lb = lb_ref[...]
    zeta = jnp.exp(lb * row)
    dec = jnp.exp(lb * float(CHUNK))
    for c in reversed(range(k_ref.shape[1] // CHUNK)):
        rows = slice(c * CHUNK, (c + 1) * CHUNK)
        for h in range(RET_HEADS):
            cols = slice(h * RET_HEAD_DIM, (h + 1) * RET_HEAD_DIM)
            st = st_ref[h]
            rb_ref[0, c, h] = st.astype(BF16)
            zv = (v_ref[0, rows, cols].astype(F32) * zeta[:, cols]).astype(BF16)
            st_ref[h] = st * dec[:, cols] + _dot_tn(k_ref[0, rows, cols], zv)


def _states(kr, vr, lb_l):
    b, s, _ = kr.shape
    nc = s // CHUNK
    per_step = min(STATE_CHUNKS, nc)
    ns = nc // per_step
    t = per_step * CHUNK
    bwd = pl.BlockSpec((1, t, RET_WIDTH), lambda bi, i: (bi, ns - 1 - i, 0))
    out_b = pl.BlockSpec((1, per_step, RET_HEADS, RET_HEAD_DIM, RET_HEAD_DIM),
                         lambda bi, i: (bi, ns - 1 - i, 0, 0, 0))
    return pl.pallas_call(
        _state_kernel,
        grid=(b, ns),
        in_specs=[bwd, bwd, _const_spec((1, RET_WIDTH))],
        out_specs=out_b,
        out_shape=jax.ShapeDtypeStruct((b, nc, RET_HEADS, RET_HEAD_DIM, RET_HEAD_DIM), BF16),
        scratch_shapes=[pltpu.VMEM((RET_HEADS, RET_HEAD_DIM, RET_HEAD_DIM), F32)],
        compiler_params=pltpu.CompilerParams(
            dimension_semantics=("arbitrary", "arbitrary"), vmem_limit_bytes=VMEM_LIMIT),
        name="ret_states",
    )(kr, vr, lb_l)


def _mixer_kernel(qa_ref, kp_ref, kc_ref, kn_ref, vp_ref, vc_ref, vn_ref,
                  qr_ref, kr_ref, vr_ref, gr_ref, gate_ref, rb_ref, x_ref,
                  sink_ref, lf_ref, lb_ref, rng_ref, wba_ref, wbr_ref, wo_ref,
                  o_ref,
                  k_s, vt_s, dm_s, xi_s, zeta_s, st_s, cap_s, s_buf, p_buf, ot_buf, in_buf, cr_buf,
                  att_s, ret_s, mrg_s):
    t = x_ref.shape[1]
    nblk = t // BLOCK
    tile = pl.program_id(1)

    k_s[0:BLOCK] = kp_ref[0]
    k_s[BLOCK:BLOCK + t] = kc_ref[0]
    k_s[BLOCK + t:] = kn_ref[0]
    for i in range(nblk + 2):
        src = vp_ref if i == 0 else (vn_ref if i == nblk + 1 else vc_ref)
        r0 = 0 if i in (0, nblk + 1) else (i - 1) * BLOCK
        vt_s[BF16_ROWS:, i * BLOCK:(i + 1) * BLOCK] = src[0, r0:r0 + BLOCK, :].astype(F32).T.astype(BF16)

    @pl.when((pl.program_id(0) == 0) & (tile == 0))
    def _():
        ones_row = lax.broadcasted_iota(jnp.int32, (BF16_ROWS, t + 2 * BLOCK), 0) == 0
        vt_s[0:BF16_ROWS, :] = jnp.where(ones_row, 1.0, 0.0).astype(BF16)
        ri = lax.broadcasted_iota(jnp.int32, (CHUNK, CHUNK), 0).astype(F32)
        ci = lax.broadcasted_iota(jnp.int32, (CHUNK, CHUNK), 1).astype(F32)
        diff = ri - ci
        for h in range(RET_HEADS):
            lf = lf_ref[h:h + 1, :]
            lb = lb_ref[h:h + 1, :]
            dm_s[h] = (jnp.where(diff >= 0, jnp.exp(lf * jnp.maximum(diff, 0.0)), 0.0)
                       + jnp.where(diff < 0, jnp.exp(lb * jnp.maximum(-diff, 0.0)), 0.0))
            xi_s[0, h] = jnp.exp(lf * (ri + 1.0))
            xi_s[1, h] = jnp.exp(lb * (float(CHUNK) - ri))
            zeta_s[h] = jnp.exp(lf * (float(CHUNK) - 1.0 - ri))
        kj = lax.broadcasted_iota(jnp.int32, (3 * BLOCK, BLOCK), 0)
        qi = lax.broadcasted_iota(jnp.int32, (3 * BLOCK, BLOCK), 1)
        band = jnp.abs(kj - BLOCK - qi) <= WINDOW
        cap_s[0] = jnp.where(band, jnp.inf, NEG_INF)
        cap_s[1] = jnp.where(band & (kj >= BLOCK), jnp.inf, NEG_INF)
        cap_s[2] = jnp.where(band & (kj < 2 * BLOCK), jnp.inf, NEG_INF)

    @pl.when(tile == 0)
    def _():
        st_s[...] = jnp.zeros_like(st_s)

    first_cap = jnp.where(tile == 0, 1, 0)
    last_cap = jnp.where(tile == pl.num_programs(1) - 1, 2, 0)
    head1 = lax.broadcasted_iota(jnp.int32, (1, 2 * BLOCK), 1) >= BLOCK

    n_slab = ATT_Q_HEADS // 2
    units = [(j, s) for j in range(nblk) for s in range(n_slab)]

    def att_scores(u):
        j, s = units[u]
        rows = slice(j * BLOCK, (j + 1) * BLOCK)
        q2 = jnp.concatenate([qa_ref[0, rows, (2 * s) * LANES:(2 * s + 1) * LANES],
                              qa_ref[0, rows, (2 * s + 1) * LANES:(2 * s + 2) * LANES]], axis=0)
        sc = _dot_nt(k_s[j * BLOCK:(j + 3) * BLOCK, :], q2)
        cap = cap_s[first_cap if j == 0 else (last_cap if j == nblk - 1 else 0)]
        for cs in (slice(0, BLOCK), slice(BLOCK, 2 * BLOCK)):
            s_buf[u % 2, 0:BLOCK, cs] = jnp.minimum(sc[:BLOCK, cs], cap[:BLOCK])
            s_buf[u % 2, BLOCK:2 * BLOCK, cs] = sc[BLOCK:2 * BLOCK, cs]
            s_buf[u % 2, 2 * BLOCK:, cs] = jnp.minimum(sc[2 * BLOCK:, cs], cap[2 * BLOCK:])

    def att_softmax(u):
        j, s = units[u]
        sink = jnp.where(head1, sink_ref[2 * s + 1], sink_ref[2 * s]) * LOG2E
        m = jnp.maximum(jnp.max(s_buf[u % 2], axis=0, keepdims=True), sink)
        p_buf[u % 2] = jnp.exp2(s_buf[u % 2] - m).astype(BF16)
        return jnp.exp2(sink - m)

    def att_values(u, sink_term):
        j, s = units[u]
        res = _dot(vt_s[:, j * BLOCK:(j + 3) * BLOCK], p_buf[u % 2])
        den = res[0:1, :] + sink_term
        v0 = BF16_ROWS + ((2 * s) // ATT_GROUP) * HEAD_DIM
        ot_buf[u % 2] = res[v0:v0 + HEAD_DIM] * (1.0 / den)

    def att_store(u):
        j, s = units[u]
        rows = slice(j * BLOCK, (j + 1) * BLOCK)
        ot = ot_buf[u % 2]
        o = jnp.concatenate([ot[:, :BLOCK], ot[:, BLOCK:]], axis=0).T
        att_s[rows, s * LANES:(s + 1) * LANES] = o.astype(BF16)

    ret_units = [(j, h) for j in range(nblk) for h in range(RET_HEADS)]

    def ret_products(u):
        j, h = ret_units[u]
        rows = slice(j * BLOCK, (j + 1) * BLOCK)
        cols = slice(h * LANES, (h + 1) * LANES)
        q = qr_ref[0, rows, cols]
        k = kr_ref[0, rows, cols]
        in_buf[u % 2] = (_dot_nt(q, k) * dm_s[h]).astype(BF16)
        st = st_s[h]
        cr_buf[u % 2] = _dot(q, jnp.concatenate([st.astype(BF16), rb_ref[0, j, h]], axis=1))
        zv = (vr_ref[0, rows, cols].astype(F32) * zeta_s[h]).astype(BF16)
        st_s[h] = st * jnp.exp(lf_ref[h:h + 1, :] * float(CHUNK)) + _dot_tn(k, zv)

    def ret_finish(u):
        j, h = ret_units[u]
        rows = slice(j * BLOCK, (j + 1) * BLOCK)
        cols = slice(h * LANES, (h + 1) * LANES)
        cross = cr_buf[u % 2]
        o = _dot(in_buf[u % 2], vr_ref[0, rows, cols])
        o = o + cross[:, :LANES] * xi_s[0, h] + cross[:, LANES:] * xi_s[1, h]
        o = o * lax.rsqrt(jnp.mean(o * o, axis=-1, keepdims=True) + EPS)
        g = gr_ref[0, rows, cols].astype(F32)
        o = (o * rng_ref[:, cols]) * (g * jax.nn.sigmoid(g))
        ret_s[rows, cols] = o.astype(BF16)

    n_units = len(units)
    sink_terms = {}
    for k in range(n_units + 3):
        if k < n_units:
            att_scores(k)
        if 2 <= k < n_units + 2:
            att_values(k - 2, sink_terms.pop(k - 2))
        if k < len(ret_units):
            ret_products(k)
        if 1 <= k <= len(ret_units):
            ret_finish(k - 1)
        if k >= 3:
            att_store(k - 3)
        if 1 <= k <= n_units:
            sink_terms[k - 1] = att_softmax(k - 1)

    for c in range(D_MODEL // MERGE_COLS):
        cols = slice(c * MERGE_COLS, (c + 1) * MERGE_COLS)
        ba = _dot(att_s[...], wba_ref[:, cols])
        br = _dot(ret_s[...], wbr_ref[:, cols])
        g0 = jax.nn.sigmoid(gate_ref[0, :, c * MERGE_COLS:(c + 1) * MERGE_COLS].astype(F32))
        g1 = jax.nn.sigmoid(gate_ref[0, :, D_MODEL + c * MERGE_COLS:D_MODEL + (c + 1) * MERGE_COLS].astype(F32))
        mrg_s[:, cols] = (g0 * ba + g1 * br).astype(BF16)
    for r in range(t // OUT_ROWS):
        rows = slice(r * OUT_ROWS, (r + 1) * OUT_ROWS)
        o_ref[0, rows, :] = x_ref[0, rows, :] + _dot(mrg_s[rows, :], wo_ref[...])


def _mixer(qa, ka, va, qr, kr, vr, gr, gate, rb, x, sink, lf_h, lb_h, rng, wba, wbr, wo):
    b, s, d = x.shape
    t = min(MIXER_TILE, s)
    nblk = t // BLOCK
    nb_total = s // BLOCK
    tok = lambda width: pl.BlockSpec((1, t, width), lambda bi, i: (bi, i, 0))
    prev = pl.BlockSpec((1, BLOCK, ATT_KV_W), lambda bi, i: (bi, jnp.maximum(i * nblk - 1, 0), 0))
    nxt = pl.BlockSpec((1, BLOCK, ATT_KV_W), lambda bi, i: (bi, jnp.minimum((i + 1) * nblk, nb_total - 1), 0))
    st = pl.BlockSpec((1, nblk, RET_HEADS, RET_HEAD_DIM, RET_HEAD_DIM), lambda bi, i: (bi, i, 0, 0, 0))
    smem = pl.BlockSpec(memory_space=pltpu.SMEM)
    return pl.pallas_call(
        _mixer_kernel,
        grid=(b, s // t),
        in_specs=[tok(ATT_Q_HEADS * LANES), prev, tok(ATT_KV_W), nxt, prev, tok(ATT_KV_W), nxt,
                  tok(RET_WIDTH), tok(RET_WIDTH), tok(RET_WIDTH), tok(RET_WIDTH), tok(2 * D_MODEL),
                  st, tok(d),
                  smem, _const_spec((RET_HEADS, LANES)), _const_spec((RET_HEADS, LANES)),
                  _const_spec((1, RET_WIDTH)),
                  _const_spec((ATT_Q_W, d)), _const_spec((RET_WIDTH, d)), _const_spec((d, d))],
        out_specs=tok(d),
        out_shape=jax.ShapeDtypeStruct((b, s, d), F32),
        scratch_shapes=[
            pltpu.VMEM((t + 2 * BLOCK, ATT_KV_W), BF16),
            pltpu.VMEM((ATT_KV_W + BF16_ROWS, t + 2 * BLOCK), BF16),
            pltpu.VMEM((RET_HEADS, CHUNK, CHUNK), F32),
            pltpu.VMEM((2, RET_HEADS, CHUNK, CHUNK), F32),
            pltpu.VMEM((RET_HEADS, CHUNK, CHUNK), F32),
            pltpu.VMEM((RET_HEADS, RET_HEAD_DIM, RET_HEAD_DIM), F32),
            pltpu.VMEM((3, 3 * BLOCK, BLOCK), F32),
            pltpu.VMEM((2, 3 * BLOCK, 2 * BLOCK), F32),
            pltpu.VMEM((2, 3 * BLOCK, 2 * BLOCK), BF16),
            pltpu.VMEM((2, HEAD_DIM, 2 * BLOCK), F32),
            pltpu.VMEM((2, CHUNK, CHUNK), BF16),
            pltpu.VMEM((2, CHUNK, 2 * RET_HEAD_DIM), F32),
            pltpu.VMEM((t, ATT_Q_W), BF16),
            pltpu.VMEM((t, RET_WIDTH), BF16),
            pltpu.VMEM((t, d), BF16),
        ],
        compiler_params=pltpu.CompilerParams(
            dimension_semantics=("arbitrary", "arbitrary"), vmem_limit_bytes=VMEM_LIMIT),
        name="mixer",
    )(qa, ka, ka, ka, va, va, va, qr, kr, vr, gr, gate, rb, x, sink, lf_h, lb_h, rng, wba, wbr, wo)


def _ffn_kernel(final_norm, xp_ref, x_ref, xn_ref, g_ref, wi_ref, cw_ref, cb_ref, wo_ref, fg_ref,
                o_ref, gu_s):
    t = x_ref.shape[1]
    i = pl.program_id(1)
    n = pl.num_programs(1)
    g = g_ref[...]
    x = x_ref[0]
    hp = jnp.where(i > 0, _rmsnorm(xp_ref[0], g), 0.0)
    hn = jnp.where(i < n - 1, _rmsnorm(xn_ref[0], g), 0.0)
    h = _rmsnorm(x, g).astype(BF16)
    h_ext = jnp.concatenate([hp.astype(BF16), h, hn.astype(BF16)], axis=0)

    lo = BF16_ROWS
    for c in range(D_FF // FF_CHUNK):
        cols = slice(c * FF_CHUNK, (c + 1) * FF_CHUNK)
        a_ext = _dot(h_ext, wi_ref[:, cols])
        u = _dot(h, wi_ref[:, D_FF + c * FF_CHUNK:D_FF + (c + 1) * FF_CHUNK])
        w = cw_ref[:, cols]
        a = (cb_ref[:, cols]
             + pltpu.roll(a_ext, 1, 0)[lo:lo + t] * w[0:1]
             + a_ext[lo:lo + t] * w[1:2]
             + pltpu.roll(a_ext, t + 2 * lo - 1, 0)[lo:lo + t] * w[2:3])
        gelu = 0.5 * a * (1.0 + lax.erf(a * (2.0 ** -0.5)))
        gu_s[:, cols] = (gelu * u).astype(BF16)
    for r in range(t // OUT_ROWS):
        rows = slice(r * OUT_ROWS, (r + 1) * OUT_ROWS)
        y = x_ref[0, rows, :] + _dot(gu_s[rows, :], wo_ref[...])
        if final_norm:
            y = _rmsnorm(y, fg_ref[...])
        o_ref[0, rows, :] = y


def _ffn(x, g, wi, cw, cb, wo, fg, final_norm):
    b, s, d = x.shape
    t = min(FFN_TILE, s)
    hb = t // BF16_ROWS
    nh = s // BF16_ROWS
    tok = pl.BlockSpec((1, t, d), lambda bi, i: (bi, i, 0))
    prev = pl.BlockSpec((1, BF16_ROWS, d), lambda bi, i: (bi, jnp.maximum(i * hb - 1, 0), 0))
    nxt = pl.BlockSpec((1, BF16_ROWS, d), lambda bi, i: (bi, jnp.minimum((i + 1) * hb, nh - 1), 0))
    return pl.pallas_call(
        functools.partial(_ffn_kernel, final_norm),
        grid=(b, s // t),
        in_specs=[prev, tok, nxt, _const_spec((1, d)), _const_spec((d, 2 * D_FF)),
                  _const_spec((CONV_WIDTH, D_FF)), _const_spec((1, D_FF)), _const_spec((D_FF, d)),
                  _const_spec((1, d))],
        out_specs=tok,
        out_shape=jax.ShapeDtypeStruct((b, s, d), F32),
        scratch_shapes=[pltpu.VMEM((t, D_FF), BF16)],
        compiler_params=pltpu.CompilerParams(
            dimension_semantics=("arbitrary", "arbitrary"), vmem_limit_bytes=VMEM_LIMIT),
        name="ffn",
    )(x, x, x, g, wi, cw, cb, wo, fg)


def _rope_tables(s, half, reps):
    freqs = ROPE_THETA ** (-jnp.arange(half, dtype=F32) / half)
    ang = jnp.arange(s, dtype=F32)[:, None] * freqs[None, :]
    cos = jnp.cos(ang)
    sin = jnp.sin(ang)
    cos_t = jnp.tile(jnp.concatenate([cos, cos], axis=-1), (1, reps))
    sin_t = jnp.tile(jnp.concatenate([-sin, sin], axis=-1), (1, reps))
    return cos_t, sin_t


def _layer(x, p, tabs, final_g, final_norm):
    ca, sa, cr, sr = tabs
    qa, ka, va, qr, kr, vr, gr, gate = _inproj(x, p["norm_mix_g"], p["w_in"], ca, sa, cr, sr)
    rb = _states(kr, vr, p["lb_lanes"])
    x = _mixer(qa, ka, va, qr, kr, vr, gr, gate, rb, x, p["sink"], p["lf_heads"], p["lb_heads"],
               p["ret_norm_g"], p["w_branch_attn"], p["w_branch_ret"], p["w_out"])
    return _ffn(x, p["norm_ffn_g"], p["w_ffn_in"], p["conv_w"], p["conv_b"], p["w_ffn_out"],
                final_g, final_norm)


def kernel(x_prompt, x_sample, norm_mix_g, w_in, attn_sink, ret_log_decay_f, ret_log_decay_b, ret_norm_g,
           w_branch_attn, w_branch_ret, w_out, norm_ffn_g, w_ffn_in, conv_w, conv_b, w_ffn_out, final_norm_g):
    depth = w_in.shape[0]
    layers = []
    for l in range(depth):
        lf = ret_log_decay_f[l].astype(F32)
        lb = ret_log_decay_b[l].astype(F32)
        layers.append(dict(
            norm_mix_g=norm_mix_g[l].reshape(1, D_MODEL),
            w_in=w_in[l].astype(BF16),
            sink=attn_sink[l].astype(F32),
            lb_lanes=jnp.repeat(lb, RET_HEAD_DIM).reshape(1, RET_WIDTH),
            lf_heads=jnp.broadcast_to(lf[:, None], (RET_HEADS, LANES)),
            lb_heads=jnp.broadcast_to(lb[:, None], (RET_HEADS, LANES)),
            ret_norm_g=ret_norm_g[l].reshape(1, RET_WIDTH),
            w_branch_attn=w_branch_attn[l].astype(BF16),
            w_branch_ret=w_branch_ret[l].astype(BF16),
            w_out=w_out[l].astype(BF16),
            norm_ffn_g=norm_ffn_g[l].reshape(1, D_MODEL),
            w_ffn_in=w_ffn_in[l].astype(BF16),
            conv_w=conv_w[l],
            conv_b=conv_b[l].reshape(1, D_FF),
            w_ffn_out=w_ffn_out[l].astype(BF16),
        ))
    fg = final_norm_g.reshape(1, D_MODEL)
    outs = []
    for x in (x_prompt, x_sample):
        s = x.shape[1]
        tabs = _rope_tables(s, HEAD_DIM // 2, LANES // HEAD_DIM) + _rope_tables(s, RET_HEAD_DIM // 2, 1)
        for l in range(depth):
            x = _layer(x, layers[l], tabs, fg, l == depth - 1)
        outs.append(x)
    return tuple(outs)
```

```python
import functools

import jax
import jax.numpy as jnp
import numpy as np
from jax import lax
from jax.experimental import pallas as pl
from jax.experimental.pallas import tpu as pltpu

D_MODEL = 1024
HEAD_DIM = 64
ATT_Q_HEADS = 8
ATT_KV_HEADS = 2
ATT_GROUP = ATT_Q_HEADS // ATT_KV_HEADS
ATT_Q_W = ATT_Q_HEADS * HEAD_DIM
ATT_KV_W = ATT_KV_HEADS * HEAD_DIM
WINDOW = 128
BLOCK = 128
RET_HEADS = 4
RET_HEAD_DIM = 128
RET_WIDTH = RET_HEADS * RET_HEAD_DIM
CHUNK = 128
D_FF = 2816
CONV_WIDTH = 3
ROPE_THETA = 10000.0
EPS = 1e-6
NEG_INF = -1e30
LOG2E = 1.4426950408889634
IN_WIDTH = ATT_Q_W + 2 * ATT_KV_W + 4 * RET_WIDTH + 2 * D_MODEL

LANES = 128
BF16_ROWS = 16
VMEM_LIMIT = 56 * 1024 * 1024

INPROJ_TILE = 1024
MIXER_TILE = 1024
FFN_TILE = 1024
FF_CHUNK = 256
STATE_CHUNKS = 8
MERGE_COLS = 256
OUT_ROWS = 256

F32 = jnp.float32
BF16 = jnp.bfloat16

_C_QA = 0
_C_KA = _C_QA + ATT_Q_W
_C_VA = _C_KA + ATT_KV_W
_C_QR = _C_VA + ATT_KV_W
_C_KR = _C_QR + RET_WIDTH
_C_VR = _C_KR + RET_WIDTH
_C_GR = _C_VR + RET_WIDTH
_C_GATE = _C_GR + RET_WIDTH


def _rmsnorm(x, g):
    return (x * lax.rsqrt(jnp.mean(x * x, axis=-1, keepdims=True) + EPS)) * g


def _dot(a, b):
    return jnp.dot(a, b, preferred_element_type=F32)


def _dot_nt(a, b):
    return lax.dot_general(a, b, (((1,), (1,)), ((), ())), preferred_element_type=F32)


def _dot_tn(a, b):
    return lax.dot_general(a, b, (((0,), (0,)), ((), ())), preferred_element_type=F32)


def _const_spec(shape):
    nd = len(shape)
    return pl.BlockSpec(shape, lambda *_: (0,) * nd, pipeline_mode=pl.Buffered(1))


def _inproj_kernel(x_ref, g_ref, w_ref, ca_ref, sa_ref, cr_ref, sr_ref,
                   qa_ref, ka_ref, va_ref, qr_ref, kr_ref, vr_ref, gr_ref, gate_ref):
    x = x_ref[0]
    hb = _rmsnorm(x, g_ref[...]).astype(BF16)
    t = x.shape[0]
    lane = lax.broadcasted_iota(jnp.int32, (t, LANES), 1)
    first_half = (lane % HEAD_DIM) < (HEAD_DIM // 2)
    ca, sa, cr, sr = ca_ref[...], sa_ref[...], cr_ref[...], sr_ref[...]

    def mm(c0, n):
        return _dot(hb, w_ref[:, c0:c0 + n])

    def rope_a(y):
        rot = jnp.where(first_half, pltpu.roll(y, LANES - HEAD_DIM // 2, 1),
                        pltpu.roll(y, HEAD_DIM // 2, 1))
        return y * ca + rot * sa

    def rope_r(y):
        return y * cr + pltpu.roll(y, RET_HEAD_DIM // 2, 1) * sr

    def slab(y, s):
        return y[:, s * LANES:(s + 1) * LANES]

    low = lane < HEAD_DIM
    y = mm(_C_QA, ATT_Q_W)
    for s in range(ATT_Q_W // LANES):
        r = rope_a(slab(y, s)) * (HEAD_DIM ** -0.5 * LOG2E)
        r_sw = pltpu.roll(r, HEAD_DIM, 1)
        kv_low = (2 * s) // ATT_GROUP == 0
        h0 = jnp.where(low, r, 0.0) if kv_low else jnp.where(low, 0.0, r_sw)
        h1 = jnp.where(low, r_sw, 0.0) if kv_low else jnp.where(low, 0.0, r)
        qa_ref[0, :, (2 * s) * LANES:(2 * s + 1) * LANES] = h0.astype(BF16)
        qa_ref[0, :, (2 * s + 1) * LANES:(2 * s + 2) * LANES] = h1.astype(BF16)
    y = mm(_C_KA, 2 * ATT_KV_W)
    ka_ref[0] = rope_a(slab(y, 0)).astype(BF16)
    va_ref[0] = slab(y, 1).astype(BF16)
    y = mm(_C_QR, RET_WIDTH)
    for s in range(RET_HEADS):
        qr_ref[0, :, s * LANES:(s + 1) * LANES] = rope_r(slab(y, s)).astype(BF16)
    y = mm(_C_KR, RET_WIDTH)
    for s in range(RET_HEADS):
        kr_ref[0, :, s * LANES:(s + 1) * LANES] = (rope_r(slab(y, s)) * (RET_HEAD_DIM ** -0.5)).astype(BF16)
    vr_ref[0] = mm(_C_VR, RET_WIDTH).astype(BF16)
    gr_ref[0] = mm(_C_GR, RET_WIDTH).astype(BF16)
    for c in range(4):
        n = 2 * D_MODEL // 4
        gate_ref[0, :, c * n:(c + 1) * n] = mm(_C_GATE + c * n, n).astype(BF16)


def _inproj(x, g, w, ca, sa, cr, sr):
    b, s, d = x.shape
    t = min(INPROJ_TILE, s)
    tok = lambda width: pl.BlockSpec((1, t, width), lambda bi, i: (bi, i, 0))
    tab = pl.BlockSpec((t, LANES), lambda bi, i: (i, 0))
    widths = [ATT_Q_HEADS * LANES, ATT_KV_W, ATT_KV_W, RET_WIDTH, RET_WIDTH, RET_WIDTH, RET_WIDTH, 2 * D_MODEL]
    return pl.pallas_call(
        _inproj_kernel,
        grid=(b, s // t),
        in_specs=[tok(d), _const_spec((1, d)), _const_spec((d, IN_WIDTH)), tab, tab, tab, tab],
        out_specs=[tok(wd) for wd in widths],
        out_shape=[jax.ShapeDtypeStruct((b, s, wd), BF16) for wd in widths],
        compiler_params=pltpu.CompilerParams(
            dimension_semantics=("arbitrary", "arbitrary"), vmem_limit_bytes=VMEM_LIMIT),
        name="inproj",
    )(x, g, w, ca, sa, cr, sr)


def _state_kernel(k_ref, v_ref, lb_ref, rb_ref, st_ref):
    @pl.when(pl.program_id(1) == 0)
    def _():
        st_ref[...] = jnp.zeros_like(st_ref)

    row = lax.broadcasted_iota(jnp.int32, (CHUNK, RET_WIDTH), 0).astype(F32)
    lb = lb_ref[...]
    zeta = jnp.exp(lb * row)
    dec = jnp.exp(lb * float(CHUNK))
    for c in reversed(range(k_ref.shape[1] // CHUNK)):
        rows = slice(c * CHUNK, (c + 1) * CHUNK)
        for h in range(RET_HEADS):
            cols = slice(h * RET_HEAD_DIM, (h + 1) * RET_HEAD_DIM)
            st = st_ref[h]
            rb_ref[0, c, h] = st.astype(BF16)
            zv = (v_ref[0, rows, cols].astype(F32) * zeta[:, cols]).astype(BF16)
            st_ref[h] = st * dec[:, cols] + _dot_tn(k_ref[0, rows, cols], zv)


def _states(kr, vr, lb_l):
    b, s, _ = kr.shape
    nc = s // CHUNK
    per_step = min(STATE_CHUNKS, nc)
    ns = nc // per_step
    t = per_step * CHUNK
    bwd = pl.BlockSpec((1, t, RET_WIDTH), lambda bi, i: (bi, ns - 1 - i, 0))
    out_b = pl.BlockSpec((1, per_step, RET_HEADS, RET_HEAD_DIM, RET_HEAD_DIM),
                         lambda bi, i: (bi, ns - 1 - i, 0, 0, 0))
    return pl.pallas_call(
        _state_kernel,
        grid=(b, ns),
        in_specs=[bwd, bwd, _const_spec((1, RET_WIDTH))],
        out_specs=out_b,
        out_shape=jax.ShapeDtypeStruct((b, nc, RET_HEADS, RET_HEAD_DIM, RET_HEAD_DIM), BF16),
        scratch_shapes=[pltpu.VMEM((RET_HEADS, RET_HEAD_DIM, RET_HEAD_DIM), F32)],
        compiler_params=pltpu.CompilerParams(
            dimension_semantics=("arbitrary", "arbitrary"), vmem_limit_bytes=VMEM_LIMIT),
        name="ret_states",
    )(kr, vr, lb_l)


def _mixer_kernel(qa_ref, kp_ref, kc_ref, kn_ref, vp_ref, vc_ref, vn_ref,
                  qr_ref, kr_ref, vr_ref, gr_ref, gate_ref, rb_ref, x_ref,
                  sink_ref, lf_ref, lb_ref, rng_ref, wba_ref, wbr_ref, wo_ref,
                  o_ref,
                  k_s, vt_s, dm_s, xi_s, zeta_s, st_s, cap_s, s_buf, p_buf, ot_buf, in_buf, cr_buf,
                  att_s, ret_s, mrg_s):
    t = x_ref.shape[1]
    nblk = t // BLOCK
    tile = pl.program_id(1)

    k_s[0:BLOCK] = kp_ref[0]
    k_s[BLOCK:BLOCK + t] = kc_ref[0]
    k_s[BLOCK + t:] = kn_ref[0]
    for i in range(nblk + 2):
        src = vp_ref if i == 0 else (vn_ref if i == nblk + 1 else vc_ref)
        r0 = 0 if i in (0, nblk + 1) else (i - 1) * BLOCK
        vt_s[BF16_ROWS:, i * BLOCK:(i + 1) * BLOCK] = src[0, r0:r0 + BLOCK, :].astype(F32).T.astype(BF16)

    @pl.when((pl.program_id(0) == 0) & (tile == 0))
    def _():
        ones_row = lax.broadcasted_iota(jnp.int32, (BF16_ROWS, t + 2 * BLOCK), 0) == 0
        vt_s[0:BF16_ROWS, :] = jnp.where(ones_row, 1.0, 0.0).astype(BF16)
        ri = lax.broadcasted_iota(jnp.int32, (CHUNK, CHUNK), 0).astype(F32)
        ci = lax.broadcasted_iota(jnp.int32, (CHUNK, CHUNK), 1).astype(F32)
        diff = ri - ci
        for h in range(RET_HEADS):
            lf = lf_ref[h:h + 1, :]
            lb = lb_ref[h:h + 1, :]
            dm_s[h] = (jnp.where(diff >= 0, jnp.exp(lf * jnp.maximum(diff, 0.0)), 0.0)
                       + jnp.where(diff < 0, jnp.exp(lb * jnp.maximum(-diff, 0.0)), 0.0))
            xi_s[0, h] = jnp.exp(lf * (ri + 1.0))
            xi_s[1, h] = jnp.exp(lb * (float(CHUNK) - ri))
            zeta_s[h] = jnp.exp(lf * (float(CHUNK) - 1.0 - ri))
        kj = lax.broadcasted_iota(jnp.int32, (3 * BLOCK, BLOCK), 0)
        qi = lax.broadcasted_iota(jnp.int32, (3 * BLOCK, BLOCK), 1)
        band = jnp.abs(kj - BLOCK - qi) <= WINDOW
        cap_s[0] = jnp.where(band, jnp.inf, NEG_INF)
        cap_s[1] = jnp.where(band & (kj >= BLOCK), jnp.inf, NEG_INF)
        cap_s[2] = jnp.where(band & (kj < 2 * BLOCK), jnp.inf, NEG_INF)

    @pl.when(tile == 0)
    def _():
        st_s[...] = jnp.zeros_like(st_s)

    first_cap = jnp.where(tile == 0, 1, 0)
    last_cap = jnp.where(tile == pl.num_programs(1) - 1, 2, 0)
    head1 = lax.broadcasted_iota(jnp.int32, (1, 2 * BLOCK), 1) >= BLOCK

    n_slab = ATT_Q_HEADS // 2
    units = [(j, s) for j in range(nblk) for s in range(n_slab)]

    def att_scores(u):
        j, s = units[u]
        rows = slice(j * BLOCK, (j + 1) * BLOCK)
        q2 = jnp.concatenate([qa_ref[0, rows, (2 * s) * LANES:(2 * s + 1) * LANES],
                              qa_ref[0, rows, (2 * s + 1) * LANES:(2 * s + 2) * LANES]], axis=0)
        sc = _dot_nt(k_s[j * BLOCK:(j + 3) * BLOCK, :], q2)
        cap = cap_s[first_cap if j == 0 else (last_cap if j == nblk - 1 else 0)]
        for cs in (slice(0, BLOCK), slice(BLOCK, 2 * BLOCK)):
            s_buf[u % 2, 0:BLOCK, cs] = jnp.minimum(sc[:BLOCK, cs], cap[:BLOCK])
            s_buf[u % 2, BLOCK:2 * BLOCK, cs] = sc[BLOCK:2 * BLOCK, cs]
            s_buf[u % 2, 2 * BLOCK:, cs] = jnp.minimum(sc[2 * BLOCK:, cs], cap[2 * BLOCK:])

    def att_softmax(u):
        j, s = units[u]
        sink = jnp.where(head1, sink_ref[2 * s + 1], sink_ref[2 * s]) * LOG2E
        m = jnp.maximum(jnp.max(s_buf[u % 2], axis=0, keepdims=True), sink)
        p_buf[u % 2] = jnp.exp2(s_buf[u % 2] - m).astype(BF16)
        return jnp.exp2(sink - m)

    def att_values(u, sink_term):
        j, s = units[u]
        res = _dot(vt_s[:, j * BLOCK:(j + 3) * BLOCK], p_buf[u % 2])
        den = res[0:1, :] + sink_term
        v0 = BF16_ROWS + ((2 * s) // ATT_GROUP) * HEAD_DIM
        ot_buf[u % 2] = res[v0:v0 + HEAD_DIM] * (1.0 / den)

    def att_store(u):
        j, s = units[u]
        rows = slice(j * BLOCK, (j + 1) * BLOCK)
        ot = ot_buf[u % 2]
        o = jnp.concatenate([ot[:, :BLOCK], ot[:, BLOCK:]], axis=0).T
        att_s[rows, s * LANES:(s + 1) * LANES] = o.astype(BF16)

    ret_units = [(j, h) for j in range(nblk) for h in range(RET_HEADS)]

    def ret_products(u):
        j, h = ret_units[u]
        rows = slice(j * BLOCK, (j + 1) * BLOCK)
        cols = slice(h * LANES, (h + 1) * LANES)
        q = qr_ref[0, rows, cols]
        k = kr_ref[0, rows, cols]
        in_buf[u % 2] = (_dot_nt(q, k) * dm_s[h]).astype(BF16)
        st = st_s[h]
        cr_buf[u % 2] = _dot(q, jnp.concatenate([st.astype(BF16), rb_ref[0, j, h]], axis=1))
        zv = (vr_ref[0, rows, cols].astype(F32) * zeta_s[h]).astype(BF16)
        st_s[h] = st * jnp.exp(lf_ref[h:h + 1, :] * float(CHUNK)) + _dot_tn(k, zv)

    def ret_finish(u):
        j, h = ret_units[u]
        rows = slice(j * BLOCK, (j + 1) * BLOCK)
        cols = slice(h * LANES, (h + 1) * LANES)
        cross = cr_buf[u % 2]
        o = _dot(in_buf[u % 2], vr_ref[0, rows, cols])
        o = o + cross[:, :LANES] * xi_s[0, h] + cross[:, LANES:] * xi_s[1, h]
        o = o * lax.rsqrt(jnp.mean(o * o, axis=-1, keepdims=True) + EPS)
        g = gr_ref[0, rows, cols].astype(F32)
        o = (o * rng_ref[:, cols]) * (g * jax.nn.sigmoid(g))
        ret_s[rows, cols] = o.astype(BF16)

    n_units = len(units)
    sink_terms = {}
    for k in range(n_units + 3):
        if k < n_units:
            att_scores(k)
        if 2 <= k < n_units + 2:
            att_values(k - 2, sink_terms.pop(k - 2))
        if k < len(ret_units):
            ret_products(k)
        if 1 <= k <= len(ret_units):
            ret_finish(k - 1)
        if k >= 3:
            att_store(k - 3)
        if 1 <= k <= n_units:
            sink_terms[k - 1] = att_softmax(k - 1)

    for c in range(D_MODEL // MERGE_COLS):
        cols = slice(c * MERGE_COLS, (c + 1) * MERGE_COLS)
        ba = _dot(att_s[...], wba_ref[:, cols])
        br = _dot(ret_s[...], wbr_ref[:, cols])
        g0 = jax.nn.sigmoid(gate_ref[0, :, c * MERGE_COLS:(c + 1) * MERGE_COLS].astype(F32))
        g1 = jax.nn.sigmoid(gate_ref[0, :, D_MODEL + c * MERGE_COLS:D_MODEL + (c + 1) * MERGE_COLS].astype(F32))
        mrg_s[:, cols] = (g0 * ba + g1 * br).astype(BF16)
    for r in range(t // OUT_ROWS):
        rows = slice(r * OUT_ROWS, (r + 1) * OUT_ROWS)
        o_ref[0, rows, :] = x_ref[0, rows, :] + _dot(mrg_s[rows, :], wo_ref[...])


def _mixer(qa, ka, va, qr, kr, vr, gr, gate, rb, x, sink, lf_h, lb_h, rng, wba, wbr, wo):
    b, s, d = x.shape
    t = min(MIXER_TILE, s)
    nblk = t // BLOCK
    nb_total = s // BLOCK
    tok = lambda width: pl.BlockSpec((1, t, width), lambda bi, i: (bi, i, 0))
    prev = pl.BlockSpec((1, BLOCK, ATT_KV_W), lambda bi, i: (bi, jnp.maximum(i * nblk - 1, 0), 0))
    nxt = pl.BlockSpec((1, BLOCK, ATT_KV_W), lambda bi, i: (bi, jnp.minimum((i + 1) * nblk, nb_total - 1), 0))
    st = pl.BlockSpec((1, nblk, RET_HEADS, RET_HEAD_DIM, RET_HEAD_DIM), lambda bi, i: (bi, i, 0, 0, 0))
    smem = pl.BlockSpec(memory_space=pltpu.SMEM)
    return pl.pallas_call(
        _mixer_kernel,
        grid=(b, s // t),
        in_specs=[tok(ATT_Q_HEADS * LANES), prev, tok(ATT_KV_W), nxt, prev, tok(ATT_KV_W), nxt,
                  tok(RET_WIDTH), tok(RET_WIDTH), tok(RET_WIDTH), tok(RET_WIDTH), tok(2 * D_MODEL),
                  st, tok(d),
                  smem, _const_spec((RET_HEADS, LANES)), _const_spec((RET_HEADS, LANES)),
                  _const_spec((1, RET_WIDTH)),
                  _const_spec((ATT_Q_W, d)), _const_spec((RET_WIDTH, d)), _const_spec((d, d))],
        out_specs=tok(d),
        out_shape=jax.ShapeDtypeStruct((b, s, d), F32),
        scratch_shapes=[
            pltpu.VMEM((t + 2 * BLOCK, ATT_KV_W), BF16),
            pltpu.VMEM((ATT_KV_W + BF16_ROWS, t + 2 * BLOCK), BF16),
            pltpu.VMEM((RET_HEADS, CHUNK, CHUNK), F32),
            pltpu.VMEM((2, RET_HEADS, CHUNK, CHUNK), F32),
            pltpu.VMEM((RET_HEADS, CHUNK, CHUNK), F32),
            pltpu.VMEM((RET_HEADS, RET_HEAD_DIM, RET_HEAD_DIM), F32),
            pltpu.VMEM((3, 3 * BLOCK, BLOCK), F32),
            pltpu.VMEM((2, 3 * BLOCK, 2 * BLOCK), F32),
            pltpu.VMEM((2, 3 * BLOCK, 2 * BLOCK), BF16),
            pltpu.VMEM((2, HEAD_DIM, 2 * BLOCK), F32),
            pltpu.VMEM((2, CHUNK, CHUNK), BF16),
            pltpu.VMEM((2, CHUNK, 2 * RET_HEAD_DIM), F32),
            pltpu.VMEM((t, ATT_Q_W), BF16),
            pltpu.VMEM((t, RET_WIDTH), BF16),
            pltpu.VMEM((t, d), BF16),
        ],
        compiler_params=pltpu.CompilerParams(
            dimension_semantics=("arbitrary", "arbitrary"), vmem_limit_bytes=VMEM_LIMIT),
        name="mixer",
    )(qa, ka, ka, ka, va, va, va, qr, kr, vr, gr, gate, rb, x, sink, lf_h, lb_h, rng, wba, wbr, wo)


def _ffn_kernel(final_norm, xp_ref, x_ref, xn_ref, g_ref, wi_ref, cw_ref, cb_ref, wo_ref, fg_ref,
                o_ref, gu_s):
    t = x_ref.shape[1]
    i = pl.program_id(1)
    n = pl.num_programs(1)
    g = g_ref[...]
    x = x_ref[0]
    hp = jnp.where(i > 0, _rmsnorm(xp_ref[0], g), 0.0)
    hn = jnp.where(i < n - 1, _rmsnorm(xn_ref[0], g), 0.0)
    h = _rmsnorm(x, g).astype(BF16)
    h_ext = jnp.concatenate([hp.astype(BF16), h, hn.astype(BF16)], axis=0)

    lo = BF16_ROWS
    for c in range(D_FF // FF_CHUNK):
        cols = slice(c * FF_CHUNK, (c + 1) * FF_CHUNK)
        a_ext = _dot(h_ext, wi_ref[:, cols])
        u = _dot(h, wi_ref[:, D_FF + c * FF_CHUNK:D_FF + (c + 1) * FF_CHUNK])
        w = cw_ref[:, cols]
        a = (cb_ref[:, cols]
             + pltpu.roll(a_ext, 1, 0)[lo:lo + t] * w[0:1]
             + a_ext[lo:lo + t] * w[1:2]
             + pltpu.roll(a_ext, t + 2 * lo - 1, 0)[lo:lo + t] * w[2:3])
        gelu = 0.5 * a * (1.0 + lax.erf(a * (2.0 ** -0.5)))
        gu_s[:, cols] = (gelu * u).astype(BF16)
    for r in range(t // OUT_ROWS):
        rows = slice(r * OUT_ROWS, (r + 1) * OUT_ROWS)
        y = x_ref[0, rows, :] + _dot(gu_s[rows, :], wo_ref[...])
        if final_norm:
            y = _rmsnorm(y, fg_ref[...])
        o_ref[0, rows, :] = y


def _ffn(x, g, wi, cw, cb, wo, fg, final_norm):
    b, s, d = x.shape
    t = min(FFN_TILE, s)
    hb = t // BF16_ROWS
    nh = s // BF16_ROWS
    tok = pl.BlockSpec((1, t, d), lambda bi, i: (bi, i, 0))
    prev = pl.BlockSpec((1, BF16_ROWS, d), lambda bi, i: (bi, jnp.maximum(i * hb - 1, 0), 0))
    nxt = pl.BlockSpec((1, BF16_ROWS, d), lambda bi, i: (bi, jnp.minimum((i + 1) * hb, nh - 1), 0))
    return pl.pallas_call(
        functools.partial(_ffn_kernel, final_norm),
        grid=(b, s // t),
        in_specs=[prev, tok, nxt, _const_spec((1, d)), _const_spec((d, 2 * D_FF)),
                  _const_spec((CONV_WIDTH, D_FF)), _const_spec((1, D_FF)), _const_spec((D_FF, d)),
                  _const_spec((1, d))],
        out_specs=tok,
        out_shape=jax.ShapeDtypeStruct((b, s, d), F32),
        scratch_shapes=[pltpu.VMEM((t, D_FF), BF16)],
        compiler_params=pltpu.CompilerParams(
            dimension_semantics=("arbitrary", "arbitrary"), vmem_limit_bytes=VMEM_LIMIT),
        name="ffn",
    )(x, x, x, g, wi, cw, cb, wo, fg)


def _rope_tables(s, half, reps):
    freqs = ROPE_THETA ** (-np.arange(half, dtype=np.float64) / half)
    ang = np.arange(s, dtype=np.float64)[:, None] * freqs[None, :]
    cos = np.cos(ang).astype(np.float32)
    sin = np.sin(ang).astype(np.float32)
    cos_t = np.tile(np.concatenate([cos, cos], axis=-1), (1, reps))
    sin_t = np.tile(np.concatenate([-sin, sin], axis=-1), (1, reps))
    return jnp.asarray(cos_t), jnp.asarray(sin_t)


def _layer(x, p, tabs, final_g, final_norm):
    ca, sa, cr, sr = tabs
    qa, ka, va, qr, kr, vr, gr, gate = _inproj(x, p["norm_mix_g"], p["w_in"], ca, sa, cr, sr)
    rb = _states(kr, vr, p["lb_lanes"])
    x = _mixer(qa, ka, va, qr, kr, vr, gr, gate, rb, x, p["sink"], p["lf_heads"], p["lb_heads"],
               p["ret_norm_g"], p["w_branch_attn"], p["w_branch_ret"], p["w_out"])
    return _ffn(x, p["norm_ffn_g"], p["w_ffn_in"], p["conv_w"], p["conv_b"], p["w_ffn_out"],
                final_g, final_norm)


def kernel(x_prompt, x_sample, norm_mix_g, w_in, attn_sink, ret_log_decay_f, ret_log_decay_b, ret_norm_g,
           w_branch_attn, w_branch_ret, w_out, norm_ffn_g, w_ffn_in, conv_w, conv_b, w_ffn_out, final_norm_g):
    depth = w_in.shape[0]
    layers = []
    for l in range(depth):
        lf = ret_log_decay_f[l].astype(F32)
        lb = ret_log_decay_b[l].astype(F32)
        layers.append(dict(
            norm_mix_g=norm_mix_g[l].reshape(1, D_MODEL),
            w_in=w_in[l].astype(BF16),
            sink=attn_sink[l].astype(F32),
            lb_lanes=jnp.repeat(lb, RET_HEAD_DIM).reshape(1, RET_WIDTH),
            lf_heads=jnp.broadcast_to(lf[:, None], (RET_HEADS, LANES)),
            lb_heads=jnp.broadcast_to(lb[:, None], (RET_HEADS, LANES)),
            ret_norm_g=ret_norm_g[l].reshape(1, RET_WIDTH),
            w_branch_attn=w_branch_attn[l].astype(BF16),
            w_branch_ret=w_branch_ret[l].astype(BF16),
            w_out=w_out[l].astype(BF16),
            norm_ffn_g=norm_ffn_g[l].reshape(1, D_MODEL),
            w_ffn_in=w_ffn_in[l].astype(BF16),
            conv_w=conv_w[l],
            conv_b=conv_b[l].reshape(1, D_FF),
            w_ffn_out=w_ffn_out[l].astype(BF16),
        ))
    fg = final_norm_g.reshape(1, D_MODEL)
    outs = []
    for x in (x_prompt, x_sample):
        s = x.shape[1]
        tabs = _rope_tables(s, HEAD_DIM // 2, LANES // HEAD_DIM) + _rope_tables(s, RET_HEAD_DIM // 2, 1)
        for l in range(depth):
            x = _layer(x, layers[l], tabs, fg, l == depth - 1)
        outs.append(x)
    return tuple(outs)
```

```python
import functools

import jax
import jax.numpy as jnp
import numpy as np
from jax import lax
from jax.experimental import pallas as pl
from jax.experimental.pallas import tpu as pltpu

D_MODEL = 1024
HEAD_DIM = 64
ATT_Q_HEADS = 8
ATT_KV_HEADS = 2
ATT_GROUP = ATT_Q_HEADS // ATT_KV_HEADS
ATT_Q_W = ATT_Q_HEADS * HEAD_DIM
ATT_KV_W = ATT_KV_HEADS * HEAD_DIM
WINDOW = 128
BLOCK = 128
RET_HEADS = 4
RET_HEAD_DIM = 128
RET_WIDTH = RET_HEADS * RET_HEAD_DIM
CHUNK = 128
D_FF = 2816
CONV_WIDTH = 3
ROPE_THETA = 10000.0
EPS = 1e-6
NEG_INF = -1e30
LOG2E = 1.4426950408889634
IN_WIDTH = ATT_Q_W + 2 * ATT_KV_W + 4 * RET_WIDTH + 2 * D_MODEL

LANES = 128
BF16_ROWS = 16
VMEM_LIMIT = 56 * 1024 * 1024

INPROJ_TILE = 1024
MIXER_TILE = 1024
FFN_TILE = 1024
FF_CHUNK = 256
STATE_CHUNKS = 8
MERGE_COLS = 256
OUT_ROWS = 256

F32 = jnp.float32
BF16 = jnp.bfloat16

_C_QA = 0
_C_KA = _C_QA + ATT_Q_W
_C_VA = _C_KA + ATT_KV_W
_C_QR = _C_VA + ATT_KV_W
_C_KR = _C_QR + RET_WIDTH
_C_VR = _C_KR + RET_WIDTH
_C_GR = _C_VR + RET_WIDTH
_C_GATE = _C_GR + RET_WIDTH


def _rmsnorm(x, g):
    return (x * lax.rsqrt(jnp.mean(x * x, axis=-1, keepdims=True) + EPS)) * g


def _dot(a, b):
    return jnp.dot(a, b, preferred_element_type=F32)


def _dot_nt(a, b):
    return lax.dot_general(a, b, (((1,), (1,)), ((), ())), preferred_element_type=F32)


def _dot_tn(a, b):
    return lax.dot_general(a, b, (((0,), (0,)), ((), ())), preferred_element_type=F32)


def _const_spec(shape):
    nd = len(shape)
    return pl.BlockSpec(shape, lambda *_: (0,) * nd, pipeline_mode=pl.Buffered(1))


def _layer_spec(shape, layer):
    nd = len(shape)
    return pl.BlockSpec((None,) + tuple(shape), lambda *_: (layer,) + (0,) * nd, pipeline_mode=pl.Buffered(1))


def _inproj_kernel(x_ref, g_ref, w_ref, ca_ref, sa_ref, cr_ref, sr_ref,
                   qa_ref, ka_ref, va_ref, qr_ref, kr_ref, vr_ref, gr_ref, gate_ref):
    x = x_ref[0]
    hb = _rmsnorm(x, g_ref[...]).astype(BF16)
    t = x.shape[0]
    lane = lax.broadcasted_iota(jnp.int32, (t, LANES), 1)
    first_half = (lane % HEAD_DIM) < (HEAD_DIM // 2)
    ca, sa, cr, sr = ca_ref[...], sa_ref[...], cr_ref[...], sr_ref[...]

    def mm(c0, n):
        return _dot(hb, w_ref[:, c0:c0 + n])

    def rope_a(y):
        rot = jnp.where(first_half, pltpu.roll(y, LANES - HEAD_DIM // 2, 1),
                        pltpu.roll(y, HEAD_DIM // 2, 1))
        return y * ca + rot * sa

    def rope_r(y):
        return y * cr + pltpu.roll(y, RET_HEAD_DIM // 2, 1) * sr

    def slab(y, s):
        return y[:, s * LANES:(s + 1) * LANES]

    low = lane < HEAD_DIM
    y = mm(_C_QA, ATT_Q_W)
    for s in range(ATT_Q_W // LANES):
        r = rope_a(slab(y, s)) * (HEAD_DIM ** -0.5 * LOG2E)
        r_sw = pltpu.roll(r, HEAD_DIM, 1)
        kv_low = (2 * s) // ATT_GROUP == 0
        h0 = jnp.where(low, r, 0.0) if kv_low else jnp.where(low, 0.0, r_sw)
        h1 = jnp.where(low, r_sw, 0.0) if kv_low else jnp.where(low, 0.0, r)
        qa_ref[0, :, (2 * s) * LANES:(2 * s + 1) * LANES] = h0.astype(BF16)
        qa_ref[0, :, (2 * s + 1) * LANES:(2 * s + 2) * LANES] = h1.astype(BF16)
    y = mm(_C_KA, 2 * ATT_KV_W)
    ka_ref[0] = rope_a(slab(y, 0)).astype(BF16)
    va_ref[0] = slab(y, 1).astype(BF16)
    y = mm(_C_QR, RET_WIDTH)
    for s in range(RET_HEADS):
        qr_ref[0, :, s * LANES:(s + 1) * LANES] = rope_r(slab(y, s)).astype(BF16)
    y = mm(_C_KR, RET_WIDTH)
    for s in range(RET_HEADS):
        kr_ref[0, :, s * LANES:(s + 1) * LANES] = (rope_r(slab(y, s)) * (RET_HEAD_DIM ** -0.5)).astype(BF16)
    vr_ref[0] = mm(_C_VR, RET_WIDTH).astype(BF16)
    gr_ref[0] = mm(_C_GR, RET_WIDTH).astype(BF16)
    for c in range(4):
        n = 2 * D_MODEL // 4
        gate_ref[0, :, c * n:(c + 1) * n] = mm(_C_GATE + c * n, n).astype(BF16)


def _inproj(x, g, w, layer, ca, sa, cr, sr):
    b, s, d = x.shape
    t = min(INPROJ_TILE, s)
    tok = lambda width: pl.BlockSpec((1, t, width), lambda bi, i: (bi, i, 0))
    tab = pl.BlockSpec((t, LANES), lambda bi, i: (i, 0))
    widths = [ATT_Q_HEADS * LANES, ATT_KV_W, ATT_KV_W, RET_WIDTH, RET_WIDTH, RET_WIDTH, RET_WIDTH, 2 * D_MODEL]
    return pl.pallas_call(
        _inproj_kernel,
        grid=(b, s // t),
        in_specs=[tok(d), _const_spec((1, d)), _layer_spec((d, IN_WIDTH), layer), tab, tab, tab, tab],
        out_specs=[tok(wd) for wd in widths],
        out_shape=[jax.ShapeDtypeStruct((b, s, wd), BF16) for wd in widths],
        compiler_params=pltpu.CompilerParams(
            dimension_semantics=("arbitrary", "arbitrary"), vmem_limit_bytes=VMEM_LIMIT),
        name="inproj",
    )(x, g, w, ca, sa, cr, sr)


def _state_kernel(k_ref, v_ref, lb_ref, rb_ref, st_ref):
    @pl.when(pl.program_id(1) == 0)
    def _():
        st_ref[...] = jnp.zeros_like(st_ref)

    row = lax.broadcasted_iota(jnp.int32, (CHUNK, RET_WIDTH), 0).astype(F32)
    lb = lb_ref[...]
    zeta = jnp.exp(lb * row)
    dec = jnp.exp(lb * float(CHUNK))
    for c in reversed(range(k_ref.shape[1] // CHUNK)):
        rows = slice(c * CHUNK, (c + 1) * CHUNK)
        for h in range(RET_HEADS):
            cols = slice(h * RET_HEAD_DIM, (h + 1) * RET_HEAD_DIM)
            st = st_ref[h]
            rb_ref[0, c, h] = st.astype(BF16)
            zv = (v_ref[0, rows, cols].astype(F32) * zeta[:, cols]).astype(BF16)
            st_ref[h] = st * dec[:, cols] + _dot_tn(k_ref[0, rows, cols], zv)


def _states(kr, vr, lb_l):
    b, s, _ = kr.shape
    nc = s // CHUNK
    per_step = min(STATE_CHUNKS, nc)
    ns = nc // per_step
    t = per_step * CHUNK
    bwd = pl.BlockSpec((1, t, RET_WIDTH), lambda bi, i: (bi, ns - 1 - i, 0))
    out_b = pl.BlockSpec((1, per_step, RET_HEADS, RET_HEAD_DIM, RET_HEAD_DIM),
                         lambda bi, i: (bi, ns - 1 - i, 0, 0, 0))
    return pl.pallas_call(
        _state_kernel,
        grid=(b, ns),
        in_specs=[bwd, bwd, _const_spec((1, RET_WIDTH))],
        out_specs=out_b,
        out_shape=jax.ShapeDtypeStruct((b, nc, RET_HEADS, RET_HEAD_DIM, RET_HEAD_DIM), BF16),
        scratch_shapes=[pltpu.VMEM((RET_HEADS, RET_HEAD_DIM, RET_HEAD_DIM), F32)],
        compiler_params=pltpu.CompilerParams(
            dimension_semantics=("arbitrary", "arbitrary"), vmem_limit_bytes=VMEM_LIMIT),
        name="ret_states",
    )(kr, vr, lb_l)


def _mixer_kernel(qa_ref, kp_ref, kc_ref, kn_ref, vp_ref, vc_ref, vn_ref,
                  qr_ref, kr_ref, vr_ref, gr_ref, gate_ref, rb_ref, x_ref,
                  sink_ref, lf_ref, lb_ref, rng_ref, wba_ref, wbr_ref, wo_ref,
                  o_ref,
                  k_s, vt_s, dm_s, xi_s, zeta_s, st_s, cap_s, s_buf, p_buf, ot_buf, in_buf, cr_buf,
                  att_s, ret_s, mrg_s):
    t = x_ref.shape[1]
    nblk = t // BLOCK
    tile = pl.program_id(1)

    k_s[0:BLOCK] = kp_ref[0]
    k_s[BLOCK:BLOCK + t] = kc_ref[0]
    k_s[BLOCK + t:] = kn_ref[0]
    for i in range(nblk + 2):
        src = vp_ref if i == 0 else (vn_ref if i == nblk + 1 else vc_ref)
        r0 = 0 if i in (0, nblk + 1) else (i - 1) * BLOCK
        vt_s[BF16_ROWS:, i * BLOCK:(i + 1) * BLOCK] = src[0, r0:r0 + BLOCK, :].astype(F32).T.astype(BF16)

    @pl.when((pl.program_id(0) == 0) & (tile == 0))
    def _():
        ones_row = lax.broadcasted_iota(jnp.int32, (BF16_ROWS, t + 2 * BLOCK), 0) == 0
        vt_s[0:BF16_ROWS, :] = jnp.where(ones_row, 1.0, 0.0).astype(BF16)
        ri = lax.broadcasted_iota(jnp.int32, (CHUNK, CHUNK), 0).astype(F32)
        ci = lax.broadcasted_iota(jnp.int32, (CHUNK, CHUNK), 1).astype(F32)
        diff = ri - ci
        for h in range(RET_HEADS):
            lf = lf_ref[h:h + 1, :]
            lb = lb_ref[h:h + 1, :]
            dm_s[h] = (jnp.where(diff >= 0, jnp.exp(lf * jnp.maximum(diff, 0.0)), 0.0)
                       + jnp.where(diff < 0, jnp.exp(lb * jnp.maximum(-diff, 0.0)), 0.0))
            xi_s[0, h] = jnp.exp(lf * (ri + 1.0))
            xi_s[1, h] = jnp.exp(lb * (float(CHUNK) - ri))
            zeta_s[h] = jnp.exp(lf * (float(CHUNK) - 1.0 - ri))
        kj = lax.broadcasted_iota(jnp.int32, (3 * BLOCK, BLOCK), 0)
        qi = lax.broadcasted_iota(jnp.int32, (3 * BLOCK, BLOCK), 1)
        band = jnp.abs(kj - BLOCK - qi) <= WINDOW
        cap_s[0] = jnp.where(band, jnp.inf, NEG_INF)
        cap_s[1] = jnp.where(band & (kj >= BLOCK), jnp.inf, NEG_INF)
        cap_s[2] = jnp.where(band & (kj < 2 * BLOCK), jnp.inf, NEG_INF)

    @pl.when(tile == 0)
    def _():
        st_s[...] = jnp.zeros_like(st_s)

    first_cap = jnp.where(tile == 0, 1, 0)
    last_cap = jnp.where(tile == pl.num_programs(1) - 1, 2, 0)
    head1 = lax.broadcasted_iota(jnp.int32, (1, 2 * BLOCK), 1) >= BLOCK

    n_slab = ATT_Q_HEADS // 2
    units = [(j, s) for j in range(nblk) for s in range(n_slab)]

    def att_scores(u):
        j, s = units[u]
        rows = slice(j * BLOCK, (j + 1) * BLOCK)
        q2 = jnp.concatenate([qa_ref[0, rows, (2 * s) * LANES:(2 * s + 1) * LANES],
                              qa_ref[0, rows, (2 * s + 1) * LANES:(2 * s + 2) * LANES]], axis=0)
        sc = _dot_nt(k_s[j * BLOCK:(j + 3) * BLOCK, :], q2)
        cap = cap_s[first_cap if j == 0 else (last_cap if j == nblk - 1 else 0)]
        for cs in (slice(0, BLOCK), slice(BLOCK, 2 * BLOCK)):
            s_buf[u % 2, 0:BLOCK, cs] = jnp.minimum(sc[:BLOCK, cs], cap[:BLOCK])
            s_buf[u % 2, BLOCK:2 * BLOCK, cs] = sc[BLOCK:2 * BLOCK, cs]
            s_buf[u % 2, 2 * BLOCK:, cs] = jnp.minimum(sc[2 * BLOCK:, cs], cap[2 * BLOCK:])

    def att_softmax(u):
        j, s = units[u]
        sink = jnp.where(head1, sink_ref[2 * s + 1], sink_ref[2 * s]) * LOG2E
        m = jnp.maximum(jnp.max(s_buf[u % 2], axis=0, keepdims=True), sink)
        p_buf[u % 2] = jnp.exp2(s_buf[u % 2] - m).astype(BF16)
        return jnp.exp2(sink - m)

    def att_values(u, sink_term):
        j, s = units[u]
        res = _dot(vt_s[:, j * BLOCK:(j + 3) * BLOCK], p_buf[u % 2])
        den = res[0:1, :] + sink_term
        v0 = BF16_ROWS + ((2 * s) // ATT_GROUP) * HEAD_DIM
        ot_buf[u % 2] = res[v0:v0 + HEAD_DIM] * (1.0 / den)

    def att_store(u):
        j, s = units[u]
        rows = slice(j * BLOCK, (j + 1) * BLOCK)
        ot = ot_buf[u % 2]
        o = jnp.concatenate([ot[:, :BLOCK], ot[:, BLOCK:]], axis=0).T
        att_s[rows, s * LANES:(s + 1) * LANES] = o.astype(BF16)

    ret_units = [(j, h) for j in range(nblk) for h in range(RET_HEADS)]

    def ret_products(u):
        j, h = ret_units[u]
        rows = slice(j * BLOCK, (j + 1) * BLOCK)
        cols = slice(h * LANES, (h + 1) * LANES)
        q = qr_ref[0, rows, cols]
        k = kr_ref[0, rows, cols]
        in_buf[u % 2] = (_dot_nt(q, k) * dm_s[h]).astype(BF16)
        st = st_s[h]
        cr_buf[u % 2] = _dot(q, jnp.concatenate([st.astype(BF16), rb_ref[0, j, h]], axis=1))
        zv = (vr_ref[0, rows, cols].astype(F32) * zeta_s[h]).astype(BF16)
        st_s[h] = st * jnp.exp(lf_ref[h:h + 1, :] * float(CHUNK)) + _dot_tn(k, zv)

    def ret_finish(u):
        j, h = ret_units[u]
        rows = slice(j * BLOCK, (j + 1) * BLOCK)
        cols = slice(h * LANES, (h + 1) * LANES)
        cross = cr_buf[u % 2]
        o = _dot(in_buf[u % 2], vr_ref[0, rows, cols])
        o = o + cross[:, :LANES] * xi_s[0, h] + cross[:, LANES:] * xi_s[1, h]
        o = o * lax.rsqrt(jnp.mean(o * o, axis=-1, keepdims=True) + EPS)
        g = gr_ref[0, rows, cols].astype(F32)
        o = (o * rng_ref[:, cols]) * (g * jax.nn.sigmoid(g))
        ret_s[rows, cols] = o.astype(BF16)

    n_units = len(units)
    sink_terms = {}
    for k in range(n_units + 3):
        if k < n_units:
            att_scores(k)
        if 2 <= k < n_units + 2:
            att_values(k - 2, sink_terms.pop(k - 2))
        if k < len(ret_units):
            ret_products(k)
        if 1 <= k <= len(ret_units):
            ret_finish(k - 1)
        if k >= 3:
            att_store(k - 3)
        if 1 <= k <= n_units:
            sink_terms[k - 1] = att_softmax(k - 1)

    for c in range(D_MODEL // MERGE_COLS):
        cols = slice(c * MERGE_COLS, (c + 1) * MERGE_COLS)
        ba = _dot(att_s[...], wba_ref[:, cols])
        br = _dot(ret_s[...], wbr_ref[:, cols])
        g0 = jax.nn.sigmoid(gate_ref[0, :, c * MERGE_COLS:(c + 1) * MERGE_COLS].astype(F32))
        g1 = jax.nn.sigmoid(gate_ref[0, :, D_MODEL + c * MERGE_COLS:D_MODEL + (c + 1) * MERGE_COLS].astype(F32))
        mrg_s[:, cols] = (g0 * ba + g1 * br).astype(BF16)
    for r in range(t // OUT_ROWS):
        rows = slice(r * OUT_ROWS, (r + 1) * OUT_ROWS)
        o_ref[0, rows, :] = x_ref[0, rows, :] + _dot(mrg_s[rows, :], wo_ref[...])


def _mixer(qa, ka, va, qr, kr, vr, gr, gate, rb, x, sink, lf_h, lb_h, rng, wba, wbr, wo, layer):
    b, s, d = x.shape
    t = min(MIXER_TILE, s)
    nblk = t // BLOCK
    nb_total = s // BLOCK
    tok = lambda width: pl.BlockSpec((1, t, width), lambda bi, i: (bi, i, 0))
    prev = pl.BlockSpec((1, BLOCK, ATT_KV_W), lambda bi, i: (bi, jnp.maximum(i * nblk - 1, 0), 0))
    nxt = pl.BlockSpec((1, BLOCK, ATT_KV_W), lambda bi, i: (bi, jnp.minimum((i + 1) * nblk, nb_total - 1), 0))
    st = pl.BlockSpec((1, nblk, RET_HEADS, RET_HEAD_DIM, RET_HEAD_DIM), lambda bi, i: (bi, i, 0, 0, 0))
    smem = pl.BlockSpec(memory_space=pltpu.SMEM)
    return pl.pallas_call(
        _mixer_kernel,
        grid=(b, s // t),
        in_specs=[tok(ATT_Q_HEADS * LANES), prev, tok(ATT_KV_W), nxt, prev, tok(ATT_KV_W), nxt,
                  tok(RET_WIDTH), tok(RET_WIDTH), tok(RET_WIDTH), tok(RET_WIDTH), tok(2 * D_MODEL),
                  st, tok(d),
                  smem, _const_spec((RET_HEADS, LANES)), _const_spec((RET_HEADS, LANES)),
                  _const_spec((1, RET_WIDTH)),
                  _layer_spec((ATT_Q_W, d), layer), _layer_spec((RET_WIDTH, d), layer),
                  _layer_spec((d, d), layer)],
        out_specs=tok(d),
        out_shape=jax.ShapeDtypeStruct((b, s, d), F32),
        scratch_shapes=[
            pltpu.VMEM((t + 2 * BLOCK, ATT_KV_W), BF16),
            pltpu.VMEM((ATT_KV_W + BF16_ROWS, t + 2 * BLOCK), BF16),
            pltpu.VMEM((RET_HEADS, CHUNK, CHUNK), F32),
            pltpu.VMEM((2, RET_HEADS, CHUNK, CHUNK), F32),
            pltpu.VMEM((RET_HEADS, CHUNK, CHUNK), F32),
            pltpu.VMEM((RET_HEADS, RET_HEAD_DIM, RET_HEAD_DIM), F32),
            pltpu.VMEM((3, 3 * BLOCK, BLOCK), F32),
            pltpu.VMEM((2, 3 * BLOCK, 2 * BLOCK), F32),
            pltpu.VMEM((2, 3 * BLOCK, 2 * BLOCK), BF16),
            pltpu.VMEM((2, HEAD_DIM, 2 * BLOCK), F32),
            pltpu.VMEM((2, CHUNK, CHUNK), BF16),
            pltpu.VMEM((2, CHUNK, 2 * RET_HEAD_DIM), F32),
            pltpu.VMEM((t, ATT_Q_W), BF16),
            pltpu.VMEM((t, RET_WIDTH), BF16),
            pltpu.VMEM((t, d), BF16),
        ],
        compiler_params=pltpu.CompilerParams(
            dimension_semantics=("arbitrary", "arbitrary"), vmem_limit_bytes=VMEM_LIMIT),
        name="mixer",
    )(qa, ka, ka, ka, va, va, va, qr, kr, vr, gr, gate, rb, x, sink, lf_h, lb_h, rng, wba, wbr, wo)


def _ffn_kernel(final_norm, xp_ref, x_ref, xn_ref, g_ref, wi_ref, cw_ref, cb_ref, wo_ref, fg_ref,
                o_ref, gu_s):
    t = x_ref.shape[1]
    i = pl.program_id(1)
    n = pl.num_programs(1)
    g = g_ref[...]
    x = x_ref[0]
    hp = jnp.where(i > 0, _rmsnorm(xp_ref[0], g), 0.0)
    hn = jnp.where(i < n - 1, _rmsnorm(xn_ref[0], g), 0.0)
    h = _rmsnorm(x, g).astype(BF16)
    h_ext = jnp.concatenate([hp.astype(BF16), h, hn.astype(BF16)], axis=0)

    lo = BF16_ROWS
    for c in range(D_FF // FF_CHUNK):
        cols = slice(c * FF_CHUNK, (c + 1) * FF_CHUNK)
        a_ext = _dot(h_ext, wi_ref[:, cols])
        u = _dot(h, wi_ref[:, D_FF + c * FF_CHUNK:D_FF + (c + 1) * FF_CHUNK])
        w = cw_ref[:, cols]
        a = (cb_ref[:, cols]
             + pltpu.roll(a_ext, 1, 0)[lo:lo + t] * w[0:1]
             + a_ext[lo:lo + t] * w[1:2]
             + pltpu.roll(a_ext, t + 2 * lo - 1, 0)[lo:lo + t] * w[2:3])
        gelu = 0.5 * a * (1.0 + lax.erf(a * (2.0 ** -0.5)))
        gu_s[:, cols] = (gelu * u).astype(BF16)
    for r in range(t // OUT_ROWS):
        rows = slice(r * OUT_ROWS, (r + 1) * OUT_ROWS)
        y = x_ref[0, rows, :] + _dot(gu_s[rows, :], wo_ref[...])
        if final_norm:
            y = _rmsnorm(y, fg_ref[...])
        o_ref[0, rows, :] = y


def _ffn(x, g, wi, cw, cb, wo, fg, layer, final_norm):
    b, s, d = x.shape
    t = min(FFN_TILE, s)
    hb = t // BF16_ROWS
    nh = s // BF16_ROWS
    tok = pl.BlockSpec((1, t, d), lambda bi, i: (bi, i, 0))
    prev = pl.BlockSpec((1, BF16_ROWS, d), lambda bi, i: (bi, jnp.maximum(i * hb - 1, 0), 0))
    nxt = pl.BlockSpec((1, BF16_ROWS, d), lambda bi, i: (bi, jnp.minimum((i + 1) * hb, nh - 1), 0))
    return pl.pallas_call(
        functools.partial(_ffn_kernel, final_norm),
        grid=(b, s // t),
        in_specs=[prev, tok, nxt, _const_spec((1, d)), _layer_spec((d, 2 * D_FF), layer),
                  _const_spec((CONV_WIDTH, D_FF)), _const_spec((1, D_FF)), _layer_spec((D_FF, d), layer),
                  _const_spec((1, d))],
        out_specs=tok,
        out_shape=jax.ShapeDtypeStruct((b, s, d), F32),
        scratch_shapes=[pltpu.VMEM((t, D_FF), BF16)],
        compiler_params=pltpu.CompilerParams(
            dimension_semantics=("arbitrary", "arbitrary"), vmem_limit_bytes=VMEM_LIMIT),
        name="ffn",
    )(x, x, x, g, wi, cw, cb, wo, fg)


def _rope_tables(s, half, reps):
    freqs = ROPE_THETA ** (-np.arange(half, dtype=np.float64) / half)
    ang = np.arange(s, dtype=np.float64)[:, None] * freqs[None, :]
    cos = np.cos(ang).astype(np.float32)
    sin = np.sin(ang).astype(np.float32)
    cos_t = np.tile(np.concatenate([cos, cos], axis=-1), (1, reps))
    sin_t = np.tile(np.concatenate([-sin, sin], axis=-1), (1, reps))
    return jnp.asarray(cos_t), jnp.asarray(sin_t)


def _layer(x, p, wts, layer, tabs, final_g, final_norm):
    ca, sa, cr, sr = tabs
    qa, ka, va, qr, kr, vr, gr, gate = _inproj(x, p["norm_mix_g"], wts["w_in"], layer, ca, sa, cr, sr)
    rb = _states(kr, vr, p["lb_lanes"])
    x = _mixer(qa, ka, va, qr, kr, vr, gr, gate, rb, x, p["sink"], p["lf_heads"], p["lb_heads"],
               p["ret_norm_g"], wts["w_branch_attn"], wts["w_branch_ret"], wts["w_out"], layer)
    return _ffn(x, p["norm_ffn_g"], wts["w_ffn_in"], p["conv_w"], p["conv_b"], wts["w_ffn_out"],
                final_g, layer, final_norm)


def kernel(x_prompt, x_sample, norm_mix_g, w_in, attn_sink, ret_log_decay_f, ret_log_decay_b, ret_norm_g,
           w_branch_attn, w_branch_ret, w_out, norm_ffn_g, w_ffn_in, conv_w, conv_b, w_ffn_out, final_norm_g):
    depth = w_in.shape[0]
    wts = dict(w_in=w_in.astype(BF16), w_branch_attn=w_branch_attn.astype(BF16),
               w_branch_ret=w_branch_ret.astype(BF16), w_out=w_out.astype(BF16),
               w_ffn_in=w_ffn_in.astype(BF16), w_ffn_out=w_ffn_out.astype(BF16))
    layers = []
    for l in range(depth):
        lf = ret_log_decay_f[l].astype(F32)
        lb = ret_log_decay_b[l].astype(F32)
        layers.append(dict(
            norm_mix_g=norm_mix_g[l].reshape(1, D_MODEL),
            sink=attn_sink[l].astype(F32),
            lb_lanes=jnp.repeat(lb, RET_HEAD_DIM).reshape(1, RET_WIDTH),
            lf_heads=jnp.broadcast_to(lf[:, None], (RET_HEADS, LANES)),
            lb_heads=jnp.broadcast_to(lb[:, None], (RET_HEADS, LANES)),
            ret_norm_g=ret_norm_g[l].reshape(1, RET_WIDTH),
            norm_ffn_g=norm_ffn_g[l].reshape(1, D_MODEL),
            conv_w=conv_w[l],
            conv_b=conv_b[l].reshape(1, D_FF),
        ))
    fg = final_norm_g.reshape(1, D_MODEL)
    outs = []
    for x in (x_prompt, x_sample):
        s = x.shape[1]
        tabs = _rope_tables(s, HEAD_DIM // 2, LANES // HEAD_DIM) + _rope_tables(s, RET_HEAD_DIM // 2, 1)
        for l in range(depth):
            x = _layer(x, layers[l], wts, l, tabs, fg, l == depth - 1)
        outs.append(x)
    return tuple(outs)
```

```python
import functools

import jax
import jax.numpy as jnp
import numpy as np
from jax import lax
from jax.experimental import pallas as pl
from jax.experimental.pallas import tpu as pltpu

D_MODEL = 1024
HEAD_DIM = 64
ATT_Q_HEADS = 8
ATT_KV_HEADS = 2
ATT_GROUP = ATT_Q_HEADS // ATT_KV_HEADS
ATT_Q_W = ATT_Q_HEADS * HEAD_DIM
ATT_KV_W = ATT_KV_HEADS * HEAD_DIM
WINDOW = 128
BLOCK = 128
RET_HEADS = 4
RET_HEAD_DIM = 128
RET_WIDTH = RET_HEADS * RET_HEAD_DIM
CHUNK = 128
D_FF = 2816
CONV_WIDTH = 3
ROPE_THETA = 10000.0
EPS = 1e-6
NEG_INF = -1e30
LOG2E = 1.4426950408889634
IN_WIDTH = ATT_Q_W + 2 * ATT_KV_W + 4 * RET_WIDTH + 2 * D_MODEL

LANES = 128
BF16_ROWS = 16
VMEM_LIMIT = 56 * 1024 * 1024

INPROJ_TILE = 1024
MIXER_TILE = 1024
FFN_TILE = 1024
FF_CHUNK = 256
STATE_CHUNKS = 16
MERGE_COLS = 256
OUT_ROWS = 256

F32 = jnp.float32
BF16 = jnp.bfloat16

_C_QA = 0
_C_KA = _C_QA + ATT_Q_W
_C_VA = _C_KA + ATT_KV_W
_C_QR = _C_VA + ATT_KV_W
_C_KR = _C_QR + RET_WIDTH
_C_VR = _C_KR + RET_WIDTH
_C_GR = _C_VR + RET_WIDTH
_C_GATE = _C_GR + RET_WIDTH


def _rmsnorm(x, g):
    return (x * lax.rsqrt(jnp.mean(x * x, axis=-1, keepdims=True) + EPS)) * g


def _dot(a, b):
    return jnp.dot(a, b, preferred_element_type=F32)


def _dot_nt(a, b):
    return lax.dot_general(a, b, (((1,), (1,)), ((), ())), preferred_element_type=F32)


def _dot_tn(a, b):
    return lax.dot_general(a, b, (((0,), (0,)), ((), ())), preferred_element_type=F32)


def _const_spec(shape):
    nd = len(shape)
    return pl.BlockSpec(shape, lambda *_: (0,) * nd, pipeline_mode=pl.Buffered(1))


def _layer_spec(shape, layer):
    nd = len(shape)
    return pl.BlockSpec((None,) + tuple(shape), lambda *_: (layer,) + (0,) * nd, pipeline_mode=pl.Buffered(1))


def _inproj_kernel(x_ref, g_ref, w_ref, ca_ref, sa_ref, cr_ref, sr_ref,
                   qa_ref, ka_ref, va_ref, qr_ref, kr_ref, vr_ref, gr_ref, gate_ref):
    x = x_ref[0]
    hb = _rmsnorm(x, g_ref[...]).astype(BF16)
    t = x.shape[0]
    lane = lax.broadcasted_iota(jnp.int32, (t, LANES), 1)
    first_half = (lane % HEAD_DIM) < (HEAD_DIM // 2)
    ca, sa, cr, sr = ca_ref[...], sa_ref[...], cr_ref[...], sr_ref[...]

    def mm(c0, n):
        return _dot(hb, w_ref[:, c0:c0 + n])

    def rope_a(y):
        rot = jnp.where(first_half, pltpu.roll(y, LANES - HEAD_DIM // 2, 1),
                        pltpu.roll(y, HEAD_DIM // 2, 1))
        return y * ca + rot * sa

    def rope_r(y):
        return y * cr + pltpu.roll(y, RET_HEAD_DIM // 2, 1) * sr

    def slab(y, s):
        return y[:, s * LANES:(s + 1) * LANES]

    low = lane < HEAD_DIM
    y = mm(_C_QA, ATT_Q_W)
    for s in range(ATT_Q_W // LANES):
        r = rope_a(slab(y, s)) * (HEAD_DIM ** -0.5 * LOG2E)
        r_sw = pltpu.roll(r, HEAD_DIM, 1)
        kv_low = (2 * s) // ATT_GROUP == 0
        h0 = jnp.where(low, r, 0.0) if kv_low else jnp.where(low, 0.0, r_sw)
        h1 = jnp.where(low, r_sw, 0.0) if kv_low else jnp.where(low, 0.0, r)
        qa_ref[0, :, (2 * s) * LANES:(2 * s + 1) * LANES] = h0.astype(BF16)
        qa_ref[0, :, (2 * s + 1) * LANES:(2 * s + 2) * LANES] = h1.astype(BF16)
    y = mm(_C_KA, 2 * ATT_KV_W)
    ka_ref[0] = rope_a(slab(y, 0)).astype(BF16)
    va_ref[0] = slab(y, 1).astype(BF16)
    y = mm(_C_QR, RET_WIDTH)
    for s in range(RET_HEADS):
        qr_ref[0, :, s * LANES:(s + 1) * LANES] = rope_r(slab(y, s)).astype(BF16)
    y = mm(_C_KR, RET_WIDTH)
    for s in range(RET_HEADS):
        kr_ref[0, :, s * LANES:(s + 1) * LANES] = (rope_r(slab(y, s)) * (RET_HEAD_DIM ** -0.5)).astype(BF16)
    vr_ref[0] = mm(_C_VR, RET_WIDTH).astype(BF16)
    gr_ref[0] = mm(_C_GR, RET_WIDTH).astype(BF16)
    for c in range(4):
        n = 2 * D_MODEL // 4
        gate_ref[0, :, c * n:(c + 1) * n] = mm(_C_GATE + c * n, n).astype(BF16)


def _inproj(x, g, w, layer, ca, sa, cr, sr):
    b, s, d = x.shape
    t = min(INPROJ_TILE, s)
    tok = lambda width: pl.BlockSpec((1, t, width), lambda bi, i: (bi, i, 0))
    tab = pl.BlockSpec((t, LANES), lambda bi, i: (i, 0))
    widths = [ATT_Q_HEADS * LANES, ATT_KV_W, ATT_KV_W, RET_WIDTH, RET_WIDTH, RET_WIDTH, RET_WIDTH, 2 * D_MODEL]
    return pl.pallas_call(
        _inproj_kernel,
        grid=(b, s // t),
        in_specs=[tok(d), _const_spec((1, d)), _layer_spec((d, IN_WIDTH), layer), tab, tab, tab, tab],
        out_specs=[tok(wd) for wd in widths],
        out_shape=[jax.ShapeDtypeStruct((b, s, wd), BF16) for wd in widths],
        compiler_params=pltpu.CompilerParams(
            dimension_semantics=("arbitrary", "arbitrary"), vmem_limit_bytes=VMEM_LIMIT),
        name="inproj",
    )(x, g, w, ca, sa, cr, sr)


def _state_kernel(k_ref, v_ref, lb_ref, rb_ref, st_ref):
    @pl.when(pl.program_id(1) == 0)
    def _():
        st_ref[...] = jnp.zeros_like(st_ref)

    row = lax.broadcasted_iota(jnp.int32, (CHUNK, RET_WIDTH), 0).astype(F32)
    lb = lb_ref[...]
    zeta = jnp.exp(lb * row)
    dec = jnp.exp(lb * float(CHUNK))
    for c in reversed(range(k_ref.shape[1] // CHUNK)):
        rows = slice(c * CHUNK, (c + 1) * CHUNK)
        for h in range(RET_HEADS):
            cols = slice(h * RET_HEAD_DIM, (h + 1) * RET_HEAD_DIM)
            st = st_ref[h]
            rb_ref[0, c, h] = st.astype(BF16)
            zv = (v_ref[0, rows, cols].astype(F32) * zeta[:, cols]).astype(BF16)
            st_ref[h] = st * dec[:, cols] + _dot_tn(k_ref[0, rows, cols], zv)


def _states(kr, vr, lb_l):
    b, s, _ = kr.shape
    nc = s // CHUNK
    per_step = min(STATE_CHUNKS, nc)
    ns = nc // per_step
    t = per_step * CHUNK
    bwd = pl.BlockSpec((1, t, RET_WIDTH), lambda bi, i: (bi, ns - 1 - i, 0))
    out_b = pl.BlockSpec((1, per_step, RET_HEADS, RET_HEAD_DIM, RET_HEAD_DIM),
                         lambda bi, i: (bi, ns - 1 - i, 0, 0, 0))
    return pl.pallas_call(
        _state_kernel,
        grid=(b, ns),
        in_specs=[bwd, bwd, _const_spec((1, RET_WIDTH))],
        out_specs=out_b,
        out_shape=jax.ShapeDtypeStruct((b, nc, RET_HEADS, RET_HEAD_DIM, RET_HEAD_DIM), BF16),
        scratch_shapes=[pltpu.VMEM((RET_HEADS, RET_HEAD_DIM, RET_HEAD_DIM), F32)],
        compiler_params=pltpu.CompilerParams(
            dimension_semantics=("arbitrary", "arbitrary"), vmem_limit_bytes=VMEM_LIMIT),
        name="ret_states",
    )(kr, vr, lb_l)


def _mixer_kernel(qa_ref, kp_ref, kc_ref, kn_ref, vp_ref, vc_ref, vn_ref,
                  qr_ref, kr_ref, vr_ref, gr_ref, gate_ref, rb_ref, x_ref,
                  sink_ref, lf_ref, lb_ref, rng_ref, wba_ref, wbr_ref, wo_ref,
                  o_ref,
                  k_s, vt_s, dm_s, xi_s, zeta_s, st_s, cap_s, s_buf, p_buf, ot_buf, in_buf, cr_buf,
                  att_s, ret_s, mrg_s):
    t = x_ref.shape[1]
    nblk = t // BLOCK
    tile = pl.program_id(1)

    k_s[0:BLOCK] = kp_ref[0]
    k_s[BLOCK:BLOCK + t] = kc_ref[0]
    k_s[BLOCK + t:] = kn_ref[0]
    for i in range(nblk + 2):
        src = vp_ref if i == 0 else (vn_ref if i == nblk + 1 else vc_ref)
        r0 = 0 if i in (0, nblk + 1) else (i - 1) * BLOCK
        vt_s[BF16_ROWS:, i * BLOCK:(i + 1) * BLOCK] = src[0, r0:r0 + BLOCK, :].astype(F32).T.astype(BF16)

    @pl.when((pl.program_id(0) == 0) & (tile == 0))
    def _():
        ones_row = lax.broadcasted_iota(jnp.int32, (BF16_ROWS, t + 2 * BLOCK), 0) == 0
        vt_s[0:BF16_ROWS, :] = jnp.where(ones_row, 1.0, 0.0).astype(BF16)
        ri = lax.broadcasted_iota(jnp.int32, (CHUNK, CHUNK), 0).astype(F32)
        ci = lax.broadcasted_iota(jnp.int32, (CHUNK, CHUNK), 1).astype(F32)
        diff = ri - ci
        for h in range(RET_HEADS):
            lf = lf_ref[h:h + 1, :]
            lb = lb_ref[h:h + 1, :]
            dm_s[h] = (jnp.where(diff >= 0, jnp.exp(lf * jnp.maximum(diff, 0.0)), 0.0)
                       + jnp.where(diff < 0, jnp.exp(lb * jnp.maximum(-diff, 0.0)), 0.0))
            xi_s[0, h] = jnp.exp(lf * (ri + 1.0))
            xi_s[1, h] = jnp.exp(lb * (float(CHUNK) - ri))
            zeta_s[h] = jnp.exp(lf * (float(CHUNK) - 1.0 - ri))
        kj = lax.broadcasted_iota(jnp.int32, (3 * BLOCK, BLOCK), 0)
        qi = lax.broadcasted_iota(jnp.int32, (3 * BLOCK, BLOCK), 1)
        band = jnp.abs(kj - BLOCK - qi) <= WINDOW
        cap_s[0] = jnp.where(band, jnp.inf, NEG_INF)
        cap_s[1] = jnp.where(band & (kj >= BLOCK), jnp.inf, NEG_INF)
        cap_s[2] = jnp.where(band & (kj < 2 * BLOCK), jnp.inf, NEG_INF)

    @pl.when(tile == 0)
    def _():
        st_s[...] = jnp.zeros_like(st_s)

    first_cap = jnp.where(tile == 0, 1, 0)
    last_cap = jnp.where(tile == pl.num_programs(1) - 1, 2, 0)
    head1 = lax.broadcasted_iota(jnp.int32, (1, 2 * BLOCK), 1) >= BLOCK

    n_slab = ATT_Q_HEADS // 2
    units = [(j, s) for j in range(nblk) for s in range(n_slab)]

    def att_scores(u):
        j, s = units[u]
        rows = slice(j * BLOCK, (j + 1) * BLOCK)
        q2 = jnp.concatenate([qa_ref[0, rows, (2 * s) * LANES:(2 * s + 1) * LANES],
                              qa_ref[0, rows, (2 * s + 1) * LANES:(2 * s + 2) * LANES]], axis=0)
        sc = _dot_nt(k_s[j * BLOCK:(j + 3) * BLOCK, :], q2)
        cap = cap_s[first_cap if j == 0 else (last_cap if j == nblk - 1 else 0)]
        for cs in (slice(0, BLOCK), slice(BLOCK, 2 * BLOCK)):
            s_buf[u % 2, 0:BLOCK, cs] = jnp.minimum(sc[:BLOCK, cs], cap[:BLOCK])
            s_buf[u % 2, BLOCK:2 * BLOCK, cs] = sc[BLOCK:2 * BLOCK, cs]
            s_buf[u % 2, 2 * BLOCK:, cs] = jnp.minimum(sc[2 * BLOCK:, cs], cap[2 * BLOCK:])

    def att_softmax(u):
        j, s = units[u]
        sink = jnp.where(head1, sink_ref[2 * s + 1], sink_ref[2 * s]) * LOG2E
        m = jnp.maximum(jnp.max(s_buf[u % 2], axis=0, keepdims=True), sink)
        p_buf[u % 2] = jnp.exp2(s_buf[u % 2] - m).astype(BF16)
        return jnp.exp2(sink - m)

    def att_values(u, sink_term):
        j, s = units[u]
        res = _dot(vt_s[:, j * BLOCK:(j + 3) * BLOCK], p_buf[u % 2])
        den = res[0:1, :] + sink_term
        v0 = BF16_ROWS + ((2 * s) // ATT_GROUP) * HEAD_DIM
        ot_buf[u % 2] = res[v0:v0 + HEAD_DIM] * (1.0 / den)

    def att_store(u):
        j, s = units[u]
        rows = slice(j * BLOCK, (j + 1) * BLOCK)
        ot = ot_buf[u % 2]
        o = jnp.concatenate([ot[:, :BLOCK], ot[:, BLOCK:]], axis=0).T
        att_s[rows, s * LANES:(s + 1) * LANES] = o.astype(BF16)

    ret_units = [(j, h) for j in range(nblk) for h in range(RET_HEADS)]

    def ret_products(u):
        j, h = ret_units[u]
        rows = slice(j * BLOCK, (j + 1) * BLOCK)
        cols = slice(h * LANES, (h + 1) * LANES)
        q = qr_ref[0, rows, cols]
        k = kr_ref[0, rows, cols]
        in_buf[u % 2] = (_dot_nt(q, k) * dm_s[h]).astype(BF16)
        st = st_s[h]
        cr_buf[u % 2] = _dot(q, jnp.concatenate([st.astype(BF16), rb_ref[0, j, h]], axis=1))
        zv = (vr_ref[0, rows, cols].astype(F32) * zeta_s[h]).astype(BF16)
        st_s[h] = st * jnp.exp(lf_ref[h:h + 1, :] * float(CHUNK)) + _dot_tn(k, zv)

    def ret_finish(u):
        j, h = ret_units[u]
        rows = slice(j * BLOCK, (j + 1) * BLOCK)
        cols = slice(h * LANES, (h + 1) * LANES)
        cross = cr_buf[u % 2]
        o = _dot(in_buf[u % 2], vr_ref[0, rows, cols])
        o = o + cross[:, :LANES] * xi_s[0, h] + cross[:, LANES:] * xi_s[1, h]
        o = o * lax.rsqrt(jnp.mean(o * o, axis=-1, keepdims=True) + EPS)
        g = gr_ref[0, rows, cols].astype(F32)
        o = (o * rng_ref[:, cols]) * (g * jax.nn.sigmoid(g))
        ret_s[rows, cols] = o.astype(BF16)

    n_units = len(units)
    sink_terms = {}
    for k in range(n_units + 3):
        if k < n_units:
            att_scores(k)
        if 2 <= k < n_units + 2:
            att_values(k - 2, sink_terms.pop(k - 2))
        if k < len(ret_units):
            ret_products(k)
        if 1 <= k <= len(ret_units):
            ret_finish(k - 1)
        if k >= 3:
            att_store(k - 3)
        if 1 <= k <= n_units:
            sink_terms[k - 1] = att_softmax(k - 1)

    for c in range(D_MODEL // MERGE_COLS):
        cols = slice(c * MERGE_COLS, (c + 1) * MERGE_COLS)
        ba = _dot(att_s[...], wba_ref[:, cols])
        br = _dot(ret_s[...], wbr_ref[:, cols])
        g0 = jax.nn.sigmoid(gate_ref[0, :, c * MERGE_COLS:(c + 1) * MERGE_COLS].astype(F32))
        g1 = jax.nn.sigmoid(gate_ref[0, :, D_MODEL + c * MERGE_COLS:D_MODEL + (c + 1) * MERGE_COLS].astype(F32))
        mrg_s[:, cols] = (g0 * ba + g1 * br).astype(BF16)
    for r in range(t // OUT_ROWS):
        rows = slice(r * OUT_ROWS, (r + 1) * OUT_ROWS)
        o_ref[0, rows, :] = x_ref[0, rows, :] + _dot(mrg_s[rows, :], wo_ref[...])


def _mixer(qa, ka, va, qr, kr, vr, gr, gate, rb, x, sink, lf_h, lb_h, rng, wba, wbr, wo, layer):
    b, s, d = x.shape
    t = min(MIXER_TILE, s)
    nblk = t // BLOCK
    nb_total = s // BLOCK
    tok = lambda width: pl.BlockSpec((1, t, width), lambda bi, i: (bi, i, 0))
    prev = pl.BlockSpec((1, BLOCK, ATT_KV_W), lambda bi, i: (bi, jnp.maximum(i * nblk - 1, 0), 0))
    nxt = pl.BlockSpec((1, BLOCK, ATT_KV_W), lambda bi, i: (bi, jnp.minimum((i + 1) * nblk, nb_total - 1), 0))
    st = pl.BlockSpec((1, nblk, RET_HEADS, RET_HEAD_DIM, RET_HEAD_DIM), lambda bi, i: (bi, i, 0, 0, 0))
    smem = pl.BlockSpec(memory_space=pltpu.SMEM)
    return pl.pallas_call(
        _mixer_kernel,
        grid=(b, s // t),
        in_specs=[tok(ATT_Q_HEADS * LANES), prev, tok(ATT_KV_W), nxt, prev, tok(ATT_KV_W), nxt,
                  tok(RET_WIDTH), tok(RET_WIDTH), tok(RET_WIDTH), tok(RET_WIDTH), tok(2 * D_MODEL),
                  st, tok(d),
                  smem, _const_spec((RET_HEADS, LANES)), _const_spec((RET_HEADS, LANES)),
                  _const_spec((1, RET_WIDTH)),
                  _layer_spec((ATT_Q_W, d), layer), _layer_spec((RET_WIDTH, d), layer),
                  _layer_spec((d, d), layer)],
        out_specs=tok(d),
        out_shape=jax.ShapeDtypeStruct((b, s, d), F32),
        scratch_shapes=[
            pltpu.VMEM((t + 2 * BLOCK, ATT_KV_W), BF16),
            pltpu.VMEM((ATT_KV_W + BF16_ROWS, t + 2 * BLOCK), BF16),
            pltpu.VMEM((RET_HEADS, CHUNK, CHUNK), F32),
            pltpu.VMEM((2, RET_HEADS, CHUNK, CHUNK), F32),
            pltpu.VMEM((RET_HEADS, CHUNK, CHUNK), F32),
            pltpu.VMEM((RET_HEADS, RET_HEAD_DIM, RET_HEAD_DIM), F32),
            pltpu.VMEM((3, 3 * BLOCK, BLOCK), F32),
            pltpu.VMEM((2, 3 * BLOCK, 2 * BLOCK), F32),
            pltpu.VMEM((2, 3 * BLOCK, 2 * BLOCK), BF16),
            pltpu.VMEM((2, HEAD_DIM, 2 * BLOCK), F32),
            pltpu.VMEM((2, CHUNK, CHUNK), BF16),
            pltpu.VMEM((2, CHUNK, 2 * RET_HEAD_DIM), F32),
            pltpu.VMEM((t, ATT_Q_W), BF16),
            pltpu.VMEM((t, RET_WIDTH), BF16),
            pltpu.VMEM((t, d), BF16),
        ],
        compiler_params=pltpu.CompilerParams(
            dimension_semantics=("arbitrary", "arbitrary"), vmem_limit_bytes=VMEM_LIMIT),
        name="mixer",
    )(qa, ka, ka, ka, va, va, va, qr, kr, vr, gr, gate, rb, x, sink, lf_h, lb_h, rng, wba, wbr, wo)


def _ffn_kernel(final_norm, xp_ref, x_ref, xn_ref, g_ref, wi_ref, cw_ref, cb_ref, wo_ref, fg_ref,
                o_ref, gu_s):
    t = x_ref.shape[1]
    i = pl.program_id(1)
    n = pl.num_programs(1)
    g = g_ref[...]
    x = x_ref[0]
    hp = jnp.where(i > 0, _rmsnorm(xp_ref[0], g), 0.0)
    hn = jnp.where(i < n - 1, _rmsnorm(xn_ref[0], g), 0.0)
    h = _rmsnorm(x, g).astype(BF16)
    h_ext = jnp.concatenate([hp.astype(BF16), h, hn.astype(BF16)], axis=0)

    lo = BF16_ROWS
    for c in range(D_FF // FF_CHUNK):
        cols = slice(c * FF_CHUNK, (c + 1) * FF_CHUNK)
        a_ext = _dot(h_ext, wi_ref[:, cols])
        u = _dot(h, wi_ref[:, D_FF + c * FF_CHUNK:D_FF + (c + 1) * FF_CHUNK])
        w = cw_ref[:, cols]
        a = (cb_ref[:, cols]
             + pltpu.roll(a_ext, 1, 0)[lo:lo + t] * w[0:1]
             + a_ext[lo:lo + t] * w[1:2]
             + pltpu.roll(a_ext, t + 2 * lo - 1, 0)[lo:lo + t] * w[2:3])
        gelu = 0.5 * a * (1.0 + lax.erf(a * (2.0 ** -0.5)))
        gu_s[:, cols] = (gelu * u).astype(BF16)
    for r in range(t // OUT_ROWS):
        rows = slice(r * OUT_ROWS, (r + 1) * OUT_ROWS)
        y = x_ref[0, rows, :] + _dot(gu_s[rows, :], wo_ref[...])
        if final_norm:
            y = _rmsnorm(y, fg_ref[...])
        o_ref[0, rows, :] = y


def _ffn(x, g, wi, cw, cb, wo, fg, layer, final_norm):
    b, s, d = x.shape
    t = min(FFN_TILE, s)
    hb = t // BF16_ROWS
    nh = s // BF16_ROWS
    tok = pl.BlockSpec((1, t, d), lambda bi, i: (bi, i, 0))
    prev = pl.BlockSpec((1, BF16_ROWS, d), lambda bi, i: (bi, jnp.maximum(i * hb - 1, 0), 0))
    nxt = pl.BlockSpec((1, BF16_ROWS, d), lambda bi, i: (bi, jnp.minimum((i + 1) * hb, nh - 1), 0))
    return pl.pallas_call(
        functools.partial(_ffn_kernel, final_norm),
        grid=(b, s // t),
        in_specs=[prev, tok, nxt, _const_spec((1, d)), _layer_spec((d, 2 * D_FF), layer),
                  _const_spec((CONV_WIDTH, D_FF)), _const_spec((1, D_FF)), _layer_spec((D_FF, d), layer),
                  _const_spec((1, d))],
        out_specs=tok,
        out_shape=jax.ShapeDtypeStruct((b, s, d), F32),
        scratch_shapes=[pltpu.VMEM((t, D_FF), BF16)],
        compiler_params=pltpu.CompilerParams(
            dimension_semantics=("arbitrary", "arbitrary"), vmem_limit_bytes=VMEM_LIMIT),
        name="ffn",
    )(x, x, x, g, wi, cw, cb, wo, fg)


def _rope_tables(s, half, reps):
    freqs = ROPE_THETA ** (-np.arange(half, dtype=np.float64) / half)
    ang = np.arange(s, dtype=np.float64)[:, None] * freqs[None, :]
    cos = np.cos(ang).astype(np.float32)
    sin = np.sin(ang).astype(np.float32)
    cos_t = np.tile(np.concatenate([cos, cos], axis=-1), (1, reps))
    sin_t = np.tile(np.concatenate([-sin, sin], axis=-1), (1, reps))
    return jnp.asarray(cos_t), jnp.asarray(sin_t)


def _layer(x, p, wts, layer, tabs, final_g, final_norm):
    ca, sa, cr, sr = tabs
    qa, ka, va, qr, kr, vr, gr, gate = _inproj(x, p["norm_mix_g"], wts["w_in"], layer, ca, sa, cr, sr)
    rb = _states(kr, vr, p["lb_lanes"])
    x = _mixer(qa, ka, va, qr, kr, vr, gr, gate, rb, x, p["sink"], p["lf_heads"], p["lb_heads"],
               p["ret_norm_g"], wts["w_branch_attn"], wts["w_branch_ret"], wts["w_out"], layer)
    return _ffn(x, p["norm_ffn_g"], wts["w_ffn_in"], p["conv_w"], p["conv_b"], wts["w_ffn_out"],
                final_g, layer, final_norm)


def kernel(x_prompt, x_sample, norm_mix_g, w_in, attn_sink, ret_log_decay_f, ret_log_decay_b, ret_norm_g,
           w_branch_attn, w_branch_ret, w_out, norm_ffn_g, w_ffn_in, conv_w, conv_b, w_ffn_out, final_norm_g):
    depth = w_in.shape[0]
    wts = dict(w_in=w_in.astype(BF16), w_branch_attn=w_branch_attn.astype(BF16),
               w_branch_ret=w_branch_ret.astype(BF16), w_out=w_out.astype(BF16),
               w_ffn_in=w_ffn_in.astype(BF16), w_ffn_out=w_ffn_out.astype(BF16))
    layers = []
    for l in range(depth):
        lf = ret_log_decay_f[l].astype(F32)
        lb = ret_log_decay_b[l].astype(F32)
        layers.append(dict(
            norm_mix_g=norm_mix_g[l].reshape(1, D_MODEL),
            sink=attn_sink[l].astype(F32),
            lb_lanes=jnp.repeat(lb, RET_HEAD_DIM).reshape(1, RET_WIDTH),
            lf_heads=jnp.broadcast_to(lf[:, None], (RET_HEADS, LANES)),
            lb_heads=jnp.broadcast_to(lb[:, None], (RET_HEADS, LANES)),
            ret_norm_g=ret_norm_g[l].reshape(1, RET_WIDTH),
            norm_ffn_g=norm_ffn_g[l].reshape(1, D_MODEL),
            conv_w=conv_w[l],
            conv_b=conv_b[l].reshape(1, D_FF),
        ))
    fg = final_norm_g.reshape(1, D_MODEL)
    outs = []
    for x in (x_prompt, x_sample):
        s = x.shape[1]
        tabs = _rope_tables(s, HEAD_DIM // 2, LANES // HEAD_DIM) + _rope_tables(s, RET_HEAD_DIM // 2, 1)
        for l in range(depth):
            x = _layer(x, layers[l], wts, l, tabs, fg, l == depth - 1)
        outs.append(x)
    return tuple(outs)
```
